```python
import math
import jax, jax.numpy as jnp
from jax import lax
import numpy as np

D_MODEL = 4096
BATCH = 4
SEQ = 2048
DEPTH = 2
DEC_BATCH = 8
DEC_SEQ = 4
PAST_LEN = 16384
PAGE_SIZE = 128

N_A_LAYERS = DEPTH // 2
N_B_LAYERS = DEPTH - N_A_LAYERS
D_FF = 256 * ((8 * D_MODEL // 3 + 255) // 256)
MLSTM_HEADS = 8
MLSTM_DQK = D_MODEL // (2 * MLSTM_HEADS)
MLSTM_DV = D_MODEL // MLSTM_HEADS
MLSTM_CHUNK = 64
ATT_HEAD_DIM = 128
ATT_KV_HEADS = D_MODEL // 512
DILATED_GROUPS = ((128, 1), (512, 4), (2048, 16))
N_GROUPS = len(DILATED_GROUPS)
HEADS_PER_GROUP = ATT_KV_HEADS
MAX_WINDOW = max(w for w, _ in DILATED_GROUPS)
ROT_DIM = ATT_HEAD_DIM // 4
ROPE_THETA = 500000.0
Q_BLOCK = 128
EPS = 1e-6
A_IN_WIDTH = 2 * MLSTM_HEADS * MLSTM_DQK + MLSTM_HEADS * MLSTM_DV + D_MODEL + 2 * MLSTM_HEADS

kernel_name = 'yoco_mlstm_dilated_swa_macaron'


def rmsnorm(x, g):
    xf = x.astype(jnp.float32)
    y = xf * lax.rsqrt(jnp.mean(xf * xf, axis=-1, keepdims=True) + EPS) * g.astype(jnp.float32)
    return y.astype(x.dtype)


def swiglu(x, w_gate, w_up, w_down):
    return (jax.nn.silu(x @ w_gate) * (x @ w_up)) @ w_down


def partial_rope(x, pos):
    half = ROT_DIM // 2
    inv_freq = ROPE_THETA ** (-jnp.arange(half, dtype=jnp.float32) / half)
    ang = pos.astype(jnp.float32)[:, None] * inv_freq[None, :]
    cos = jnp.cos(ang)[None, :, None, :]
    sin = jnp.sin(ang)[None, :, None, :]
    xf = x.astype(jnp.float32)
    x1 = xf[..., :half]
    x2 = xf[..., half:ROT_DIM]
    out = jnp.concatenate([x1 * cos - x2 * sin, x2 * cos + x1 * sin, xf[..., ROT_DIM:]], axis=-1)
    return out.astype(x.dtype)


def mlstm_mixer(h, w_in, b_i, b_f, g_mh, w_out, C0, n0, m0):
    B, T, _ = h.shape
    H, DK, DV = MLSTM_HEADS, MLSTM_DQK, MLSTM_DV
    proj = h @ w_in
    splits = np.cumsum([H * DK, H * DK, H * DV, D_MODEL, H])
    q, k, v, o, ig, fg = jnp.split(proj, splits, axis=-1)
    q = q.reshape(B, T, H, DK).astype(jnp.float32)
    k = k.reshape(B, T, H, DK).astype(jnp.float32) * (DK ** -0.5)
    v = v.reshape(B, T, H, DV).astype(jnp.float32)
    ig = ig.astype(jnp.float32) + b_i.astype(jnp.float32)
    lf = jax.nn.log_sigmoid(fg.astype(jnp.float32) + b_f.astype(jnp.float32))
    ch = math.gcd(T, MLSTM_CHUNK)
    nc = T // ch

    def chunks(a):
        return a.reshape((B, nc, ch) + a.shape[2:]).swapaxes(0, 1)

    causal = jnp.tril(jnp.ones((ch, ch), dtype=bool))

    def step(carry, xs):
        C, n, m = carry
        qc, kc, vc, ic, fc = xs
        b = jnp.cumsum(fc, axis=1)
        logD = b[:, :, None, :] - b[:, None, :, :] + ic[:, None, :, :]
        logD = jnp.where(causal[None, :, :, None], logD, -jnp.inf)
        inter = b + m[:, None, :]
        m_t = jnp.maximum(inter, logD.max(axis=2))
        S = jnp.einsum('bthd,bshd->btsh', qc, kc) * jnp.exp(logD - m_t[:, :, None, :])
        sc = jnp.exp(inter - m_t)
        num = jnp.einsum('btsh,bshv->bthv', S, vc) + sc[..., None] * jnp.einsum('bthd,bhdv->bthv', qc, C)
        den = S.sum(axis=2) + sc * jnp.einsum('bthd,bhd->bth', qc, n)
        hc = num / jnp.maximum(jnp.abs(den), jnp.exp(-m_t))[..., None]
        m_new = m_t[:, -1]
        dec = jnp.exp(b[:, -1] + m - m_new)
        ws = jnp.exp(b[:, -1:, :] - b + ic - m_new[:, None, :])
        C = dec[..., None, None] * C + jnp.einsum('bsh,bshd,bshv->bhdv', ws, kc, vc)
        n = dec[..., None] * n + jnp.einsum('bsh,bshd->bhd', ws, kc)
        return (C, n, m_new), hc

    init = (C0.astype(jnp.float32), n0.astype(jnp.float32), m0.astype(jnp.float32))
    (C, n, m), hs = lax.scan(step, init, (chunks(q), chunks(k), chunks(v), chunks(ig), chunks(lf)))
    hs = hs.swapaxes(0, 1).reshape(B, T, H, DV)
    hn = hs * lax.rsqrt(jnp.mean(hs * hs, axis=-1, keepdims=True) + EPS) * g_mh.astype(jnp.float32).reshape(H, DV)
    gated = jax.nn.sigmoid(o.astype(jnp.float32)) * hn.reshape(B, T, H * DV)
    y = gated.astype(h.dtype) @ w_out
    return y, C, n, m


def shared_kv(x, norm_kv, w_kv, g_knorm, pos):
    B, T, _ = x.shape
    kv = rmsnorm(x, norm_kv) @ w_kv
    k, v = jnp.split(kv, 2, axis=-1)
    k = partial_rope(rmsnorm(k.reshape(B, T, ATT_KV_HEADS, ATT_HEAD_DIM), g_knorm), pos)
    v = v.reshape(B, T, ATT_KV_HEADS, ATT_HEAD_DIM)
    return k, v


def dilated_attention(q, k_new, v_new, past_k, past_v, base):
    B, T = q.shape[:2]
    pad = MAX_WINDOW - past_k.shape[1]
    zeros = jnp.zeros((B, pad, ATT_KV_HEADS, ATT_HEAD_DIM), k_new.dtype)
    k_full = jnp.concatenate([zeros, past_k.astype(k_new.dtype), k_new], axis=1)
    v_full = jnp.concatenate([zeros, past_v.astype(v_new.dtype), v_new], axis=1)
    qb = math.gcd(T, Q_BLOCK)
    nb = T // qb
    q_blocks = q.reshape(B, nb, qb, N_GROUPS, HEADS_PER_GROUP, ATT_HEAD_DIM).swapaxes(0, 1)
    scale = ATT_HEAD_DIM ** -0.5

    def block(args):
        qblk, i0 = args
        iq = i0 + jnp.arange(qb)
        lses, outs = [], []
        for g, (window, dil) in enumerate(DILATED_GROUPS):
            dist = jnp.arange(window // dil + 1) * dil
            rows = MAX_WINDOW + iq[:, None] - dist[None, :]
            valid = (base + iq[:, None] - dist[None, :]) >= 0
            kg = k_full[:, rows]
            vg = v_full[:, rows]
            s = jnp.einsum('bqhd,bqjhd->bqhj', qblk[:, :, g], kg, preferred_element_type=jnp.float32) * scale
            s = jnp.where(valid[None, :, None, :], s, -jnp.inf)
            m = s.max(axis=-1, keepdims=True)
            p = jnp.exp(s - m)
            l = p.sum(axis=-1)
            o = jnp.einsum('bqhj,bqjhd->bqhd', p, vg.astype(jnp.float32)) / l[..., None]
            lses.append(m[..., 0] + jnp.log(l))
            outs.append(o)
        wts = jax.nn.softmax(jnp.stack(lses, axis=0), axis=0)
        out = jnp.sum(wts[..., None] * jnp.stack(outs, axis=0), axis=0)
        return out.astype(qblk.dtype)

    out = lax.map(block, (q_blocks, jnp.arange(nb, dtype=jnp.int32) * qb))
    return out.swapaxes(0, 1).reshape(B, T, HEADS_PER_GROUP, ATT_HEAD_DIM)


def dilated_mixer(h, w_q, g_qnorm, w_o, k_new, v_new, past_k, past_v, pos, base):
    B, T, _ = h.shape
    nq = N_GROUPS * HEADS_PER_GROUP
    q = (h @ w_q).reshape(B, T, nq, ATT_HEAD_DIM)
    q = partial_rope(rmsnorm(q, jnp.repeat(g_qnorm, HEADS_PER_GROUP, axis=0)), pos)
    q = q.reshape(B, T, N_GROUPS, HEADS_PER_GROUP, ATT_HEAD_DIM)
    att = dilated_attention(q, k_new, v_new, past_k, past_v, base)
    return att.reshape(B, T, HEADS_PER_GROUP * ATT_HEAD_DIM) @ w_o


def trunk(x, base, past_k, past_v, C0, n0, m0, p):
    B, T, _ = x.shape
    pos = base + jnp.arange(T, dtype=jnp.int32)
    Cs, ns, ms = [], [], []
    k_sh, v_sh = None, None
    for layer in range(DEPTH):
        x = x + 0.5 * swiglu(rmsnorm(x, p['norm_ffn1'][layer]), p['w_ffn1_gate'][layer], p['w_ffn1_up'][layer], p['w_ffn1_down'][layer])
        h = rmsnorm(x, p['norm_mix'][layer])
        if layer < N_A_LAYERS:
            y, C, n, m = mlstm_mixer(h, p['w_in_a'][layer], p['b_ig'][layer], p['b_fg'][layer], p['g_mh'][layer], p['w_out_a'][layer], C0[:, layer], n0[:, layer], m0[:, layer])
            Cs.append(C)
            ns.append(n)
            ms.append(m)
        else:
            j = layer - N_A_LAYERS
            y = dilated_mixer(h, p['w_q_b'][j], p['g_qnorm'][j], p['w_o_b'][j], k_sh, v_sh, past_k, past_v, pos, base)
        x = x + y
        x = x + 0.5 * swiglu(rmsnorm(x, p['norm_ffn2'][layer]), p['w_ffn2_gate'][layer], p['w_ffn2_up'][layer], p['w_ffn2_down'][layer])
        if layer == N_A_LAYERS - 1:
            k_sh, v_sh = shared_kv(x, p['norm_kv'], p['w_kv'], p['g_knorm'], pos)
    return x, k_sh, v_sh, jnp.stack(Cs, axis=1), jnp.stack(ns, axis=1), jnp.stack(ms, axis=1)


def setup_inputs(seed: int = 0) -> dict:
    key = jax.random.key(seed)
    ks = jax.random.split(key, 32)
    f32 = jnp.float32

    def w(k, shape, fan_in):
        return jax.random.normal(k, shape, f32) * (fan_in ** -0.5)

    def gain(k, shape):
        return 1.0 + 0.02 * jax.random.normal(k, shape, f32)

    kv_buf = min(MAX_WINDOW, PAST_LEN)
    H, DK, DV = MLSTM_HEADS, MLSTM_DQK, MLSTM_DV
    hd_all = HEADS_PER_GROUP * ATT_HEAD_DIM
    return {
        'x_prompt': jax.random.normal(ks[0], (BATCH, SEQ, D_MODEL), f32),
        'x_sample': jax.random.normal(ks[1], (DEC_BATCH, DEC_SEQ, D_MODEL), f32),
        'cache_k': jax.random.normal(ks[2], (DEC_BATCH, kv_buf, ATT_KV_HEADS, ATT_HEAD_DIM), f32),
        'cache_v': jax.random.normal(ks[3], (DEC_BATCH, kv_buf, ATT_KV_HEADS, ATT_HEAD_DIM), f32),
        'state_C': 0.05 * jax.random.normal(ks[4], (DEC_BATCH, N_A_LAYERS, H, DK, DV), f32),
        'state_n': 0.05 * jnp.abs(jax.random.normal(ks[5], (DEC_BATCH, N_A_LAYERS, H, DK), f32)),
        'state_m': 0.5 * jax.random.normal(ks[6], (DEC_BATCH, N_A_LAYERS, H), f32),
        'norm_ffn1': gain(ks[7], (DEPTH, D_MODEL)),
        'w_ffn1_gate': w(ks[8], (DEPTH, D_MODEL, D_FF), D_MODEL),
        'w_ffn1_up': w(ks[9], (DEPTH, D_MODEL, D_FF), D_MODEL),
        'w_ffn1_down': w(ks[10], (DEPTH, D_FF, D_MODEL), D_FF),
        'norm_mix': gain(ks[11], (DEPTH, D_MODEL)),
        'norm_ffn2': gain(ks[12], (DEPTH, D_MODEL)),
        'w_ffn2_gate': w(ks[13], (DEPTH, D_MODEL, D_FF), D_MODEL),
        'w_ffn2_up': w(ks[14], (DEPTH, D_MODEL, D_FF), D_MODEL),
        'w_ffn2_down': w(ks[15], (DEPTH, D_FF, D_MODEL), D_FF),
        'w_in_a': w(ks[16], (N_A_LAYERS, D_MODEL, A_IN_WIDTH), D_MODEL),
        'b_ig': 0.1 * jax.random.normal(ks[17], (N_A_LAYERS, H), f32),
        'b_fg': 3.0 + 3.0 * jax.random.uniform(ks[18], (N_A_LAYERS, H), f32),
        'g_mh': gain(ks[19], (N_A_LAYERS, H * DV)),
        'w_out_a': w(ks[20], (N_A_LAYERS, H * DV, D_MODEL), H * DV),
        'norm_kv': gain(ks[21], (D_MODEL,)),
        'w_kv': w(ks[22], (D_MODEL, 2 * ATT_KV_HEADS * ATT_HEAD_DIM), D_MODEL),
        'g_knorm': gain(ks[23], (ATT_HEAD_DIM,)),
        'w_q_b': w(ks[24], (N_B_LAYERS, D_MODEL, N_GROUPS * hd_all), D_MODEL),
        'g_qnorm': gain(ks[25], (N_B_LAYERS, N_GROUPS, ATT_HEAD_DIM)),
        'w_o_b': w(ks[26], (N_B_LAYERS, hd_all, D_MODEL), hd_all),
    }


def reference(x_prompt, x_sample, cache_k, cache_v, state_C, state_n, state_m,
              norm_ffn1, w_ffn1_gate, w_ffn1_up, w_ffn1_down, norm_mix,
              norm_ffn2, w_ffn2_gate, w_ffn2_up, w_ffn2_down,
              w_in_a, b_ig, b_fg, g_mh, w_out_a,
              norm_kv, w_kv, g_knorm, w_q_b, g_qnorm, w_o_b):
    p = dict(norm_ffn1=norm_ffn1, w_ffn1_gate=w_ffn1_gate, w_ffn1_up=w_ffn1_up, w_ffn1_down=w_ffn1_down,
             norm_mix=norm_mix, norm_ffn2=norm_ffn2, w_ffn2_gate=w_ffn2_gate, w_ffn2_up=w_ffn2_up,
             w_ffn2_down=w_ffn2_down, w_in_a=w_in_a, b_ig=b_ig, b_fg=b_fg, g_mh=g_mh, w_out_a=w_out_a,
             norm_kv=norm_kv, w_kv=w_kv, g_knorm=g_knorm, w_q_b=w_q_b, g_qnorm=g_qnorm, w_o_b=w_o_b)
    B, T, _ = x_prompt.shape
    H, DK, DV = MLSTM_HEADS, MLSTM_DQK, MLSTM_DV
    empty = jnp.zeros((B, 0, ATT_KV_HEADS, ATT_HEAD_DIM), x_prompt.dtype)
    C0 = jnp.zeros((B, N_A_LAYERS, H, DK, DV), jnp.float32)
    n0 = jnp.zeros((B, N_A_LAYERS, H, DK), jnp.float32)
    m0 = jnp.zeros((B, N_A_LAYERS, H), jnp.float32)
    y_prompt, k_p, v_p, C_prompt, n_prompt, m_prompt = trunk(x_prompt, 0, empty, empty, C0, n0, m0, p)
    keep = min(MAX_WINDOW, T)
    k_prompt = k_p[:, T - keep:]
    v_prompt = v_p[:, T - keep:]
    y_sample, k_sample, v_sample, C_sample, n_sample, m_sample = trunk(
        x_sample, PAST_LEN, cache_k, cache_v, state_C, state_n, state_m, p)
    return (y_prompt, y_sample, k_prompt, v_prompt, k_sample, v_sample,
            C_prompt, n_prompt, m_prompt, C_sample, n_sample, m_sample)
```

```python
import functools
import math

import jax
import jax.numpy as jnp
import numpy as np
from jax import lax
from jax.experimental import pallas as pl
from jax.experimental.pallas import tpu as pltpu

F32 = jnp.float32
BF16 = jnp.bfloat16

EPS = 1e-6
MLSTM_HEADS = 8
MLSTM_CHUNK = 64
ATT_HEAD_DIM = 128
DILATED_GROUPS = ((128, 1), (512, 4), (2048, 16))
MAX_WINDOW = max(w for w, _ in DILATED_GROUPS)
ROT_DIM = ATT_HEAD_DIM // 4
ROPE_THETA = 500000.0
PAST_LEN = 16384

V7X_VMEM_LIMIT_BYTES = 56 * 1024 * 1024
LANES = 128
BF16_SUBLANES = 16
NORM_ROWS = 16
EPI_ROWS = 64
PROMPT_ROW_TILE = 1024
PROJ_ROW_TILE = 512
MLSTM_PROMPT_CHUNK = 256
ATTN_Q_TILE = 256


def _params(n_axes):
    return pltpu.CompilerParams(dimension_semantics=("arbitrary",) * n_axes,
                                vmem_limit_bytes=V7X_VMEM_LIMIT_BYTES)


def _row_tile(m):
    return min(m, PROMPT_ROW_TILE)


def _rmsnorm_rows(src_ref, gain_ref, dst_ref, rows):
    step = min(rows, NORM_ROWS)

    def body(r, carry):
        rs = pl.ds(pl.multiple_of(r * step, step), step)
        ms = jnp.mean(jnp.square(src_ref[rs, :]), axis=-1, keepdims=True)
        dst_ref[rs, :] = (src_ref[rs, :] * lax.rsqrt(ms + EPS) * gain_ref[...]).astype(BF16)
        return carry

    lax.fori_loop(0, rows // step, body, 0)


def _load_rows(x_hbm, dst_ref, sem, row0, rows):
    cp = pltpu.make_async_copy(x_hbm.at[pl.ds(row0, rows)], dst_ref, sem)
    cp.start()
    cp.wait()


def _head_norm_rope(xh, gain, cosf, sinf):
    half = ROT_DIM // 2
    y = xh * lax.rsqrt(jnp.mean(xh * xh, axis=-1, keepdims=True) + EPS) * gain
    lane = lax.broadcasted_iota(jnp.int32, y.shape, 1)
    from_lower = pltpu.roll(y, half, axis=1)
    from_upper = pltpu.roll(y, LANES - half, axis=1)
    rot = jnp.where(lane < half, -from_upper, from_lower)
    return y * cosf + rot * sinf


def _ffn_body(x_hbm, gain_ref, wg_ref, wu_ref, wd_ref, o_hbm, acc_ref, xn_ref, sem, *, tm, n_chunk):
    i = pl.program_id(0)
    j = pl.program_id(1)
    row0 = pl.multiple_of(i * tm, tm)

    @pl.when(j == 0)
    def _load_and_norm():
        _load_rows(x_hbm, acc_ref, sem.at[0], row0, tm)
        _rmsnorm_rows(acc_ref, gain_ref, xn_ref, tm)

    xn = xn_ref[...]
    g = jnp.dot(xn, wg_ref[...], preferred_element_type=F32)
    u = jnp.dot(xn, wu_ref[...], preferred_element_type=F32)
    a = (0.5 * (g * jax.nn.sigmoid(g)) * u).astype(BF16)
    d = acc_ref.shape[1]
    for c in range(d // n_chunk):
        cs = slice(c * n_chunk, (c + 1) * n_chunk)
        acc_ref[:, cs] += jnp.dot(a, wd_ref[:, cs], preferred_element_type=F32)

    @pl.when(j == pl.num_programs(1) - 1)
    def _store():
        cp = pltpu.make_async_copy(acc_ref, o_hbm.at[pl.ds(row0, tm)], sem.at[1])
        cp.start()
        cp.wait()


def _ffn(x, gain, wg, wu, wd, layer, *, tf=256):
    m, d = x.shape
    f = wg.shape[2]
    tm = _row_tile(m)
    body = functools.partial(_ffn_body, tm=tm, n_chunk=min(d, 512))
    return pl.pallas_call(
        body,
        out_shape=jax.ShapeDtypeStruct((m, d), F32),
        grid=(m // tm, f // tf),
        in_specs=[
            pl.BlockSpec(memory_space=pl.ANY),
            pl.BlockSpec((None, 1, d), lambda i, j: (layer, 0, 0)),
            pl.BlockSpec((None, d, tf), lambda i, j: (layer, 0, j)),
            pl.BlockSpec((None, d, tf), lambda i, j: (layer, 0, j)),
            pl.BlockSpec((None, tf, d), lambda i, j: (layer, j, 0)),
        ],
        out_specs=pl.BlockSpec(memory_space=pl.ANY),
        scratch_shapes=[pltpu.VMEM((tm, d), F32), pltpu.VMEM((tm, d), BF16),
                        pltpu.SemaphoreType.DMA((2,))],
        compiler_params=_params(2),
        name="ffn",
    )(x, gain.reshape(gain.shape[0], 1, d), wg, wu, wd)


def _in_proj_body(x_hbm, gain_ref, w_ref, wgate_ref, proj_ref, gates_ref, xbuf, xn_ref, sem, *, tm):
    i = pl.program_id(0)
    j = pl.program_id(1)

    @pl.when(j == 0)
    def _load_and_norm():
        _load_rows(x_hbm, xbuf, sem.at[0], pl.multiple_of(i * tm, tm), tm)
        _rmsnorm_rows(xbuf, gain_ref, xn_ref, tm)
        gates_ref[...] = jnp.dot(xn_ref[...], wgate_ref[...], preferred_element_type=F32)

    proj_ref[...] = jnp.dot(xn_ref[...], w_ref[...], preferred_element_type=F32)


def _in_proj(x, gain, w_main, w_gate, layer, *, tn=512):
    m, d = x.shape
    n = w_main.shape[2]
    tm = _row_tile(m)
    return pl.pallas_call(
        functools.partial(_in_proj_body, tm=tm),
        out_shape=(jax.ShapeDtypeStruct((m, n), F32), jax.ShapeDtypeStruct((m, LANES), F32)),
        grid=(m // tm, n // tn),
        in_specs=[
            pl.BlockSpec(memory_space=pl.ANY),
            pl.BlockSpec((None, 1, d), lambda i, j: (layer, 0, 0)),
            pl.BlockSpec((None, d, tn), lambda i, j: (layer, 0, j)),
            pl.BlockSpec((None, d, LANES), lambda i, j: (layer, 0, 0)),
        ],
        out_specs=(pl.BlockSpec((tm, tn), lambda i, j: (i, j)),
                   pl.BlockSpec((tm, LANES), lambda i, j: (i, 0))),
        scratch_shapes=[pltpu.VMEM((tm, d), F32), pltpu.VMEM((tm, d), BF16),
                        pltpu.SemaphoreType.DMA((1,))],
        compiler_params=_params(2),
        name="in_proj",
    )(x, gain.reshape(gain.shape[0], 1, d), w_main, w_gate)


def _mlstm_body(bi_ref, bf_ref, q_ref, k_ref, v_ref, o_ref, g_ref, gmh_ref, c0_ref, n0_ref, m0_ref,
                y_ref, c_ref, n_ref, m_ref, *, chunk, valid):
    h = pl.program_id(1)
    c = pl.program_id(2)
    L = chunk

    @pl.when(c == 0)
    def _init():
        c_ref[...] = c0_ref[...]
        n_ref[...] = n0_ref[...]
        m_ref[...] = m0_ref[...]

    t_idx = lax.broadcasted_iota(jnp.int32, (L, L), 0)
    s_idx = lax.broadcasted_iota(jnp.int32, (L, L), 1)
    causal = s_idx <= t_idx
    diag = s_idx == t_idx

    gi_row = g_ref[0:1, :] + bi_ref[h]
    lf_row = jax.nn.log_sigmoid(g_ref[1:2, :] + bf_ref[h])
    lf_b = jnp.broadcast_to(lf_row, (L, L))
    gi_b = jnp.broadcast_to(gi_row, (L, L))
    lf_col = jnp.sum(jnp.where(diag, lf_b, 0.0), axis=1, keepdims=True)
    gi_col = jnp.sum(jnp.where(diag, gi_b, 0.0), axis=1, keepdims=True)
    b_col = jnp.sum(jnp.where(causal, lf_b, 0.0), axis=1, keepdims=True)
    b_row = jnp.sum(jnp.where(t_idx <= s_idx, jnp.broadcast_to(lf_col, (L, L)), 0.0),
                    axis=0, keepdims=True)

    log_d = jnp.where(causal, b_col + (gi_row - b_row), -jnp.inf)
    m_prev = m_ref[0:1, 0:1]
    inter = b_col + m_prev
    m_t = jnp.maximum(inter, jnp.max(log_d, axis=1, keepdims=True))
    decay = jnp.exp(log_d - m_t)

    dk = q_ref.shape[1]
    qf = q_ref[...]
    kf = k_ref[...] * (dk ** -0.5)
    qb = qf.astype(BF16)
    kb = kf.astype(BF16)
    vb = v_ref[...].astype(BF16)
    s = lax.dot_general(qb, kb, (((1,), (1,)), ((), ())), preferred_element_type=F32) * decay
    sc = jnp.exp(inter - m_t)
    num = jnp.dot(s.astype(BF16), vb, preferred_element_type=F32) \
        + sc * jnp.dot(qb, c_ref[...].astype(BF16), preferred_element_type=F32)
    den = jnp.sum(s, axis=1, keepdims=True) + sc * jnp.sum(qf * n_ref[...], axis=1, keepdims=True)
    hc = num / jnp.maximum(jnp.abs(den), jnp.exp(-m_t))
    hn = hc * lax.rsqrt(jnp.mean(hc * hc, axis=1, keepdims=True) + EPS) * gmh_ref[...]
    y_ref[...] = (jax.nn.sigmoid(o_ref[...]) * hn).astype(y_ref.dtype)

    m_new = m_t[valid - 1:valid, :]
    b_last = b_col[valid - 1:valid, :]
    dec = jnp.exp(b_last + m_prev - m_new)
    row = lax.broadcasted_iota(jnp.int32, (L, 1), 0)
    ws = jnp.where(row < valid, jnp.exp(b_last - b_col + gi_col - m_new), 0.0)
    kw = ws * kf
    c_ref[...] = dec * c_ref[...] + lax.dot_general(kw.astype(BF16), vb, (((0,), (0,)), ((), ())),
                                                    preferred_element_type=F32)
    n_ref[...] = dec * n_ref[...] + jnp.sum(kw, axis=0, keepdims=True)
    m_ref[...] = jnp.broadcast_to(m_new, m_ref.shape)


def _mlstm(proj, gates_row, b_i, b_f, g_mh, c0, n0, m0, *, chunk, valid):
    b, t, _ = proj.shape
    nh, dk, dv = c0.shape[1], c0.shape[2], c0.shape[3]
    k_blk0 = nh
    v_blk0 = 2 * nh * dk // dv
    o_blk0 = v_blk0 + nh
    smem = pl.BlockSpec(memory_space=pltpu.SMEM)
    state = lambda r, cdim: pl.BlockSpec((None, None, r, cdim), lambda bi, hi, ci: (bi, hi, 0, 0))
    return pl.pallas_call(
        functools.partial(_mlstm_body, chunk=chunk, valid=valid),
        out_shape=(jax.ShapeDtypeStruct((b, t, nh * dv), BF16),
                   jax.ShapeDtypeStruct(c0.shape, F32),
                   jax.ShapeDtypeStruct(n0.shape, F32),
                   jax.ShapeDtypeStruct(m0.shape, F32)),
        grid=(b, nh, t // chunk),
        in_specs=[
            smem, smem,
            pl.BlockSpec((None, chunk, dk), lambda bi, hi, ci: (bi, ci, hi)),
            pl.BlockSpec((None, chunk, dk), lambda bi, hi, ci: (bi, ci, k_blk0 + hi)),
            pl.BlockSpec((None, chunk, dv), lambda bi, hi, ci: (bi, ci, v_blk0 + hi)),
            pl.BlockSpec((None, chunk, dv), lambda bi, hi, ci: (bi, ci, o_blk0 + hi)),
            pl.BlockSpec((None, None, 2, chunk), lambda bi, hi, ci: (bi, hi, 0, ci)),
            pl.BlockSpec((None, 1, dv), lambda bi, hi, ci: (hi, 0, 0)),
            state(dk, dv), state(1, dk), state(1, LANES),
        ],
        out_specs=(pl.BlockSpec((None, chunk, dv), lambda bi, hi, ci: (bi, ci, hi)),
                   state(dk, dv), state(1, dk), state(1, LANES)),
        compiler_params=_params(3),
        name="mlstm",
    )(b_i, b_f, proj, proj, proj, proj, gates_row, g_mh.reshape(nh, 1, dv), c0, n0, m0)


def _matmul_resid_body(a_ref, w_ref, r_ref, o_ref):
    o_ref[...] = r_ref[...] + jnp.dot(a_ref[...], w_ref[...], preferred_element_type=F32)


def _matmul_resid(a, w, resid, layer, *, tn=512):
    m, k = a.shape
    n = w.shape[2]
    tm = _row_tile(m)
    return pl.pallas_call(
        _matmul_resid_body,
        out_shape=jax.ShapeDtypeStruct((m, n), F32),
        grid=(m // tm, n // tn),
        in_specs=[pl.BlockSpec((tm, k), lambda i, j: (i, 0)),
                  pl.BlockSpec((None, k, tn), lambda i, j: (layer, 0, j)),
                  pl.BlockSpec((tm, tn), lambda i, j: (i, j))],
        out_specs=pl.BlockSpec((tm, tn), lambda i, j: (i, j)),
        compiler_params=_params(2),
        name="matmul_resid",
    )(a, w, resid)


def _rope_table_body(invf_ref, cos_ref, sin_ref, *, seq, base):
    rows = cos_ref.shape[0]
    m = lax.broadcasted_iota(jnp.int32, (rows, LANES), 0) + pl.program_id(0) * rows
    pos = (base + (m & (seq - 1))).astype(F32)
    ang = pos * invf_ref[...]
    cos_ref[...] = jnp.cos(ang)
    sin_ref[...] = jnp.sin(ang)


def _rope_tables(m, seq, base):
    assert seq & (seq - 1) == 0
    half = ROT_DIM // 2
    inv_freq = ROPE_THETA ** (-jnp.arange(half, dtype=F32) / half)
    invf = jnp.zeros((1, LANES), F32).at[0, :ROT_DIM].set(jnp.tile(inv_freq, 2))
    rows = min(m, 512)
    spec = pl.BlockSpec((rows, LANES), lambda i: (i, 0))
    return pl.pallas_call(
        functools.partial(_rope_table_body, seq=seq, base=base),
        out_shape=(jax.ShapeDtypeStruct((m, LANES), F32),) * 2,
        grid=(m // rows,),
        in_specs=[pl.BlockSpec((1, LANES), lambda i: (0, 0))],
        out_specs=(spec, spec),
        compiler_params=_params(1),
        name="rope_tables",
    )(invf)


def _epilogue_heads(res_ref, gain_ref, cos_ref, sin_ref, out_ref, rows):
    step = min(rows, EPI_ROWS)
    n_heads = res_ref.shape[1] // ATT_HEAD_DIM

    def body(r, carry):
        rs = pl.ds(pl.multiple_of(r * step, step), step)
        cosf = cos_ref[rs, :]
        sinf = sin_ref[rs, :]
        for hh in range(n_heads):
            hs = slice(hh * ATT_HEAD_DIM, (hh + 1) * ATT_HEAD_DIM)
            out_ref[rs, hs] = _head_norm_rope(res_ref[rs, hs], gain_ref[...], cosf, sinf).astype(out_ref.dtype)
        return carry

    lax.fori_loop(0, rows // step, body, 0)


def _kv_proj_body(x_hbm, gain_ref, w_ref, kgain_ref, cos_ref, sin_ref, k_ref, v_ref,
                  xbuf, xn_ref, res_ref, sem, *, tm):
    i = pl.program_id(0)
    j = pl.program_id(1)

    @pl.when(j == 0)
    def _k_half():
        _load_rows(x_hbm, xbuf, sem.at[0], pl.multiple_of(i * tm, tm), tm)
        _rmsnorm_rows(xbuf, gain_ref, xn_ref, tm)
        res_ref[...] = jnp.dot(xn_ref[...], w_ref[...], preferred_element_type=F32)
        _epilogue_heads(res_ref, kgain_ref, cos_ref, sin_ref, k_ref, tm)

    @pl.when(j == 1)
    def _v_half():
        v_ref[...] = jnp.dot(xn_ref[...], w_ref[...], preferred_element_type=F32)


def _kv_proj(x, gain, w_kv, g_knorm, cosf, sinf):
    m, d = x.shape
    n = w_kv.shape[1] // 2
    tm = min(m, PROJ_ROW_TILE)
    out_spec = pl.BlockSpec((tm, n), lambda i, j: (i, 0))
    tab_spec = pl.BlockSpec((tm, LANES), lambda i, j: (i, 0))
    return pl.pallas_call(
        functools.partial(_kv_proj_body, tm=tm),
        out_shape=(jax.ShapeDtypeStruct((m, n), F32),) * 2,
        grid=(m // tm, 2),
        in_specs=[pl.BlockSpec(memory_space=pl.ANY),
                  pl.BlockSpec((1, d), lambda i, j: (0, 0)),
                  pl.BlockSpec((d, n), lambda i, j: (0, j)),
                  pl.BlockSpec((1, ATT_HEAD_DIM), lambda i, j: (0, 0)),
                  tab_spec, tab_spec],
        out_specs=(out_spec, out_spec),
        scratch_shapes=[pltpu.VMEM((tm, d), F32), pltpu.VMEM((tm, d), BF16), pltpu.VMEM((tm, n), F32),
                        pltpu.SemaphoreType.DMA((1,))],
        compiler_params=_params(2),
        name="kv_proj",
    )(x, gain.reshape(1, d), w_kv, g_knorm.reshape(1, ATT_HEAD_DIM), cosf, sinf)


def _q_proj_body(x_hbm, gain_ref, w_ref, qgain_ref, cos_ref, sin_ref, q_ref, xbuf, xn_ref, res_ref, sem, *, tm):
    i = pl.program_id(0)
    j = pl.program_id(1)

    @pl.when(j == 0)
    def _load_and_norm():
        _load_rows(x_hbm, xbuf, sem.at[0], pl.multiple_of(i * tm, tm), tm)
        _rmsnorm_rows(xbuf, gain_ref, xn_ref, tm)

    res_ref[...] = jnp.dot(xn_ref[...], w_ref[...], preferred_element_type=F32)
    _epilogue_heads(res_ref, qgain_ref, cos_ref, sin_ref, q_ref, tm)


def _q_proj(x, gain, w_q, g_qnorm, cosf, sinf, layer_mix, layer_b):
    m, d = x.shape
    n_groups = g_qnorm.shape[1]
    n = w_q.shape[2]
    tn = n // n_groups
    tm = min(m, PROJ_ROW_TILE)
    tab_spec = pl.BlockSpec((tm, LANES), lambda i, j: (i, 0))
    return pl.pallas_call(
        functools.partial(_q_proj_body, tm=tm),
        out_shape=jax.ShapeDtypeStruct((m, n), BF16),
        grid=(m // tm, n_groups),
        in_specs=[pl.BlockSpec(memory_space=pl.ANY),
                  pl.BlockSpec((None, 1, d), lambda i, j: (layer_mix, 0, 0)),
                  pl.BlockSpec((None, d, tn), lambda i, j: (layer_b, 0, j)),
                  pl.BlockSpec((None, None, 1, ATT_HEAD_DIM), lambda i, j: (layer_b, j, 0, 0)),
                  tab_spec, tab_spec],
        out_specs=pl.BlockSpec((tm, tn), lambda i, j: (i, j)),
        scratch_shapes=[pltpu.VMEM((tm, d), F32), pltpu.VMEM((tm, d), BF16), pltpu.VMEM((tm, tn), F32),
                        pltpu.SemaphoreType.DMA((1,))],
        compiler_params=_params(2),
        name="q_proj",
    )(x, gain.reshape(gain.shape[0], 1, d), w_q,
      g_qnorm.reshape(g_qnorm.shape[0], n_groups, 1, ATT_HEAD_DIM), cosf, sinf)


def _softmax_pieces(scores, masks):
    masked = [jnp.where(mk, sc, -jnp.inf) for sc, mk in zip(scores, masks)]
    m = functools.reduce(jnp.maximum, [jnp.max(sc, axis=1, keepdims=True) for sc in masked])
    probs = [jnp.exp(sc - m) for sc in masked]
    l = functools.reduce(jnp.add, [jnp.sum(p, axis=1, keepdims=True) for p in probs])
    return probs, l, m


def _qk(qh, kh):
    return lax.dot_general(qh, kh, (((1,), (1,)), ((), ())), preferred_element_type=F32)


def _attn_prompt_body(*refs, tq, win, has_prev, n_heads):
    if has_prev:
        q_ref, kc_ref, vc_ref, kp_ref, vp_ref, o_ref, lse_ref = refs
    else:
        q_ref, kc_ref, vc_ref, o_ref, lse_ref = refs
    qi = pl.program_id(2)
    scale = ATT_HEAD_DIM ** -0.5
    row = lax.broadcasted_iota(jnp.int32, (tq, tq), 0)
    col = lax.broadcasted_iota(jnp.int32, (tq, tq), 1)
    mask_c = (col <= row) & (row - col <= win)
    if has_prev:
        rowp = lax.broadcasted_iota(jnp.int32, (tq, win), 0)
        colp = lax.broadcasted_iota(jnp.int32, (tq, win), 1)
        mask_p = (colp >= rowp) & (qi > 0)
    for hh in range(n_heads):
        hs = slice(hh * ATT_HEAD_DIM, (hh + 1) * ATT_HEAD_DIM)
        qh = q_ref[:, hs]
        scores = [_qk(qh, kc_ref[:, hs].astype(BF16)) * scale]
        masks = [mask_c]
        values = [vc_ref[:, hs].astype(BF16)]
        if has_prev:
            scores.append(_qk(qh, kp_ref[:, hs].astype(BF16)) * scale)
            masks.append(mask_p)
            values.append(vp_ref[:, hs].astype(BF16))
        probs, l, m = _softmax_pieces(scores, masks)
        acc = functools.reduce(jnp.add, [jnp.dot(p.astype(BF16), v, preferred_element_type=F32)
                                         for p, v in zip(probs, values)])
        o_ref[:, hs] = acc / l
        lse_ref[:, hs] = jnp.broadcast_to(m + jnp.log(l), (tq, ATT_HEAD_DIM))


def _attn_prompt_group(q, k, v, group, window, dil):
    b, t, n = k.shape
    n_groups = q.shape[2] // n
    tc = t // dil
    win = window // dil
    tq = min(tc, ATTN_Q_TILE)
    has_prev = tc > tq
    assert tq % win == 0 and tc % tq == 0
    qv = q.reshape(b, tc, dil * n_groups * n)
    kv = k.reshape(b, tc, dil * n)
    vv = v.reshape(b, tc, dil * n)
    cur = pl.BlockSpec((None, tq, n), lambda bi, r, qi: (bi, qi, r))
    prev = pl.BlockSpec((None, win, n), lambda bi, r, qi: (bi, jnp.maximum(qi * (tq // win) - 1, 0), r))
    in_specs = [pl.BlockSpec((None, tq, n), lambda bi, r, qi: (bi, qi, r * n_groups + group)), cur, cur]
    args = [qv, kv, vv]
    if has_prev:
        in_specs += [prev, prev]
        args += [kv, vv]
    o, lse = pl.pallas_call(
        functools.partial(_attn_prompt_body, tq=tq, win=win, has_prev=has_prev, n_heads=n // ATT_HEAD_DIM),
        out_shape=(jax.ShapeDtypeStruct((b, tc, dil * n), F32),) * 2,
        grid=(b, dil, tc // tq),
        in_specs=in_specs,
        out_specs=(cur, cur),
        compiler_params=_params(3),
        name=f"attn_prompt_d{dil}",
    )(*args)
    return o.reshape(b * t, n), lse.reshape(b * t, n)


def _attn_out_body(*refs, n_groups):
    o_refs = refs[:n_groups]
    l_refs = refs[n_groups:2 * n_groups]
    w_ref, r_ref, out_ref, att_ref = refs[2 * n_groups:]

    @pl.when(pl.program_id(1) == 0)
    def _mix():
        lses = [l[...] for l in l_refs]
        m = functools.reduce(jnp.maximum, lses)
        es = [jnp.exp(l - m) for l in lses]
        tot = functools.reduce(jnp.add, es)
        att = functools.reduce(jnp.add, [(e / tot) * o[...] for e, o in zip(es, o_refs)])
        att_ref[...] = att.astype(BF16)

    out_ref[...] = r_ref[...] + jnp.dot(att_ref[...], w_ref[...], preferred_element_type=F32)


def _attn_out(outs, lses, w_o, resid, layer, *, tn=512, tm=512):
    m, k = outs[0].shape
    n = w_o.shape[2]
    tm = min(m, tm)
    blk = pl.BlockSpec((tm, k), lambda i, j: (i, 0))
    return pl.pallas_call(
        functools.partial(_attn_out_body, n_groups=len(outs)),
        out_shape=jax.ShapeDtypeStruct((m, n), F32),
        grid=(m // tm, n // tn),
        in_specs=[blk] * (2 * len(outs)) + [pl.BlockSpec((None, k, tn), lambda i, j: (layer, 0, j)),
                                            pl.BlockSpec((tm, tn), lambda i, j: (i, j))],
        out_specs=pl.BlockSpec((tm, tn), lambda i, j: (i, j)),
        scratch_shapes=[pltpu.VMEM((tm, k), BF16)],
        compiler_params=_params(2),
        name="attn_out",
    )(*outs, *lses, w_o, resid)


def _attn_sample_body(q_ref, kc_ref, vc_ref, kn_ref, vn_ref, o_ref, *, groups, n_heads, n_new):
    rows = q_ref.shape[0]
    cache = kc_ref.shape[0]
    n = n_heads * ATT_HEAD_DIM
    scale = ATT_HEAD_DIM ** -0.5
    dist_c = cache + lax.broadcasted_iota(jnp.int32, (rows, cache), 0) \
        - lax.broadcasted_iota(jnp.int32, (rows, cache), 1)
    dist_n = lax.broadcasted_iota(jnp.int32, (rows, rows), 0) - lax.broadcasted_iota(jnp.int32, (rows, rows), 1)
    col_n = lax.broadcasted_iota(jnp.int32, (rows, rows), 1)
    for hh in range(n_heads):
        hs = slice(hh * ATT_HEAD_DIM, (hh + 1) * ATT_HEAD_DIM)
        kc = kc_ref[:, hs].astype(BF16)
        vc = vc_ref[:, hs].astype(BF16)
        kn = kn_ref[:, hs].astype(BF16)
        vn = vn_ref[:, hs].astype(BF16)
        outs, lses = [], []
        for g, (window, dil) in enumerate(groups):
            qh = q_ref[:, g * n + hh * ATT_HEAD_DIM:g * n + (hh + 1) * ATT_HEAD_DIM]
            mask_c = (dist_c <= window) & ((dist_c & (dil - 1)) == 0)
            mask_n = (dist_n >= 0) & (col_n < n_new) & ((dist_n & (dil - 1)) == 0)
            probs, l, m = _softmax_pieces([_qk(qh, kc) * scale, _qk(qh, kn) * scale], [mask_c, mask_n])
            acc = jnp.dot(probs[0].astype(BF16), vc, preferred_element_type=F32) \
                + jnp.dot(probs[1].astype(BF16), vn, preferred_element_type=F32)
            outs.append(acc / l)
            lses.append(m + jnp.log(l))
        m_all = functools.reduce(jnp.maximum, lses)
        es = [jnp.exp(l - m_all) for l in lses]
        tot = functools.reduce(jnp.add, es)
        o_ref[:, hs] = functools.reduce(jnp.add, [(e / tot) * o for e, o in zip(es, outs)]).astype(o_ref.dtype)


def _attn_sample(q, k_new, v_new, cache_k, cache_v, n_new):
    b, rows, n = k_new.shape
    cache = cache_k.shape[1]
    assert cache == MAX_WINDOW and all(d & (d - 1) == 0 for _, d in DILATED_GROUPS)
    row_blk = lambda width: pl.BlockSpec((None, rows, width), lambda bi: (bi, 0, 0))
    cache_blk = pl.BlockSpec((None, cache, n), lambda bi: (bi, 0, 0))
    return pl.pallas_call(
        functools.partial(_attn_sample_body, groups=DILATED_GROUPS, n_heads=n // ATT_HEAD_DIM, n_new=n_new),
        out_shape=jax.ShapeDtypeStruct((b, rows, n), BF16),
        grid=(b,),
        in_specs=[row_blk(q.shape[2]), cache_blk, cache_blk, row_blk(n), row_blk(n)],
        out_specs=row_blk(n),
        compiler_params=_params(1),
        name="attn_sample",
    )(q, cache_k, cache_v, k_new, v_new)


def _trunk(x, seq, base, cache_k, cache_v, c0, n0, m0, p):
    m_rows, d = x.shape
    b = m_rows // seq
    nh = MLSTM_HEADS
    dk, dv = c0.shape[2], c0.shape[3]

    x = _ffn(x, p["norm_ffn1"], p["w_ffn1_gate"], p["w_ffn1_up"], p["w_ffn1_down"], 0)
    proj, gates = _in_proj(x, p["norm_mix"], p["w_in_main"], p["w_in_gate"], 0)
    if seq % MLSTM_PROMPT_CHUNK == 0:
        chunk, valid, t_pad = MLSTM_PROMPT_CHUNK, MLSTM_PROMPT_CHUNK, seq
    else:
        assert seq <= BF16_SUBLANES
        chunk, valid, t_pad = BF16_SUBLANES, seq, BF16_SUBLANES
    proj3 = proj.reshape(b, seq, -1)
    gates3 = gates.reshape(b, seq, LANES)[:, :, :2 * nh]
    if t_pad != seq:
        proj3 = jnp.pad(proj3, ((0, 0), (0, t_pad - seq), (0, 0)))
        gates3 = jnp.pad(gates3, ((0, 0), (0, t_pad - seq), (0, 0)))
    gates_row = gates3.reshape(b, t_pad, 2, nh).transpose(0, 3, 2, 1)
    y, c_fin, n_fin, m_fin = _mlstm(proj3, gates_row, p["b_ig"][0], p["b_fg"][0], p["g_mh"][0],
                                    c0, n0, m0, chunk=chunk, valid=valid)
    y = y[:, :seq].reshape(m_rows, nh * dv)
    x = _matmul_resid(y, p["w_out_a"], x, 0)
    x = _ffn(x, p["norm_ffn2"], p["w_ffn2_gate"], p["w_ffn2_up"], p["w_ffn2_down"], 0)

    cosf, sinf = _rope_tables(m_rows, seq, base)
    k, v = _kv_proj(x, p["norm_kv"], p["w_kv"], p["g_knorm"], cosf, sinf)

    x = _ffn(x, p["norm_ffn1"], p["w_ffn1_gate"], p["w_ffn1_up"], p["w_ffn1_down"], 1)
    q = _q_proj(x, p["norm_mix"], p["w_q_b"], p["g_qnorm"], cosf, sinf, 1, 0)
    n_kv = k.shape[1]
    if cache_k is None:
        outs, lses = [], []
        for g, (window, dil) in enumerate(DILATED_GROUPS):
            o_g, l_g = _attn_prompt_group(q.reshape(b, seq, -1), k.reshape(b, seq, n_kv),
                                          v.reshape(b, seq, n_kv), g, window, dil)
            outs.append(o_g)
            lses.append(l_g)
        x = _attn_out(outs, lses, p["w_o_b"], x, 0)
    else:
        pad = ((0, 0), (0, BF16_SUBLANES - seq), (0, 0))
        att = _attn_sample(jnp.pad(q.reshape(b, seq, -1), pad), jnp.pad(k.reshape(b, seq, n_kv), pad),
                           jnp.pad(v.reshape(b, seq, n_kv), pad), cache_k, cache_v, seq)
        x = _matmul_resid(att[:, :seq].reshape(m_rows, n_kv), p["w_o_b"], x, 0)
    x = _ffn(x, p["norm_ffn2"], p["w_ffn2_gate"], p["w_ffn2_up"], p["w_ffn2_down"], 1)
    return x, k, v, c_fin, n_fin, m_fin


def kernel(x_prompt, x_sample, cache_k, cache_v, state_C, state_n, state_m, norm_ffn1, w_ffn1_gate, w_ffn1_up, w_ffn1_down, norm_mix, norm_ffn2, w_ffn2_gate, w_ffn2_up, w_ffn2_down, w_in_a, b_ig, b_fg, g_mh, w_out_a, norm_kv, w_kv, g_knorm, w_q_b, g_qnorm, w_o_b):
    bp, seq_p, d = x_prompt.shape
    bs, seq_s, _ = x_sample.shape
    past = cache_k.shape[1]
    assert past == min(MAX_WINDOW, PAST_LEN)
    nh = MLSTM_HEADS
    dk, dv = state_C.shape[3], state_C.shape[4]
    n_main = 2 * nh * dk + nh * dv + d
    n_kv = cache_k.shape[2] * cache_k.shape[3]

    bf = lambda w: w.astype(BF16)
    w_gate_cols = jnp.pad(w_in_a[:, :, n_main:], ((0, 0), (0, 0), (0, LANES - 2 * nh)))
    p = dict(norm_ffn1=norm_ffn1, norm_mix=norm_mix, norm_ffn2=norm_ffn2, norm_kv=norm_kv,
             w_ffn1_gate=bf(w_ffn1_gate), w_ffn1_up=bf(w_ffn1_up), w_ffn1_down=bf(w_ffn1_down),
             w_ffn2_gate=bf(w_ffn2_gate), w_ffn2_up=bf(w_ffn2_up), w_ffn2_down=bf(w_ffn2_down),
             w_in_main=bf(w_in_a[:, :, :n_main]), w_in_gate=bf(w_gate_cols),
             b_ig=b_ig, b_fg=b_fg, g_mh=g_mh, w_out_a=bf(w_out_a), w_kv=bf(w_kv), g_knorm=g_knorm,
             w_q_b=bf(w_q_b), g_qnorm=g_qnorm, w_o_b=bf(w_o_b))

    zeros = lambda *s: jnp.zeros(s, F32)
    y_p, k_p, v_p, c_p, n_p, m_p = _trunk(
        x_prompt.reshape(bp * seq_p, d), seq_p, 0, None, None,
        zeros(bp, nh, dk, dv), zeros(bp, nh, 1, dk), zeros(bp, nh, 1, LANES), p)
    y_s, k_s, v_s, c_s, n_s, m_s = _trunk(
        x_sample.reshape(bs * seq_s, d), seq_s, PAST_LEN,
        cache_k.reshape(bs, past, n_kv), cache_v.reshape(bs, past, n_kv),
        state_C[:, 0], state_n[:, 0][:, :, None, :],
        jnp.broadcast_to(state_m[:, 0][:, :, None, None], (bs, nh, 1, LANES)), p)

    hkv, hd = cache_k.shape[2], cache_k.shape[3]
    keep = min(MAX_WINDOW, seq_p)
    return (y_p.reshape(bp, seq_p, d), y_s.reshape(bs, seq_s, d),
            k_p.reshape(bp, seq_p, hkv, hd)[:, seq_p - keep:], v_p.reshape(bp, seq_p, hkv, hd)[:, seq_p - keep:],
            k_s.reshape(bs, seq_s, hkv, hd), v_s.reshape(bs, seq_s, hkv, hd),
            c_p[:, None], n_p[:, None, :, 0], m_p[:, None, :, 0, 0],
            c_s[:, None], n_s[:, None, :, 0], m_s[:, None, :, 0, 0])
```

```python
import functools
import math

import jax
import jax.numpy as jnp
import numpy as np
from jax import lax
from jax.experimental import pallas as pl
from jax.experimental.pallas import tpu as pltpu

F32 = jnp.float32
BF16 = jnp.bfloat16

EPS = 1e-6
MLSTM_HEADS = 8
MLSTM_CHUNK = 64
ATT_HEAD_DIM = 128
DILATED_GROUPS = ((128, 1), (512, 4), (2048, 16))
MAX_WINDOW = max(w for w, _ in DILATED_GROUPS)
ROT_DIM = ATT_HEAD_DIM // 4
ROPE_THETA = 500000.0
PAST_LEN = 16384

V7X_VMEM_LIMIT_BYTES = 56 * 1024 * 1024
LANES = 128
BF16_SUBLANES = 16
NORM_ROWS = 16
NORM_UNROLL = 4
LOAD_CHUNKS = 8
EPI_ROWS = 64
PROMPT_ROW_TILE = 1024
PROJ_ROW_TILE = 512
MLSTM_PROMPT_CHUNK = 256
ATTN_Q_TILE = 256


def _params(n_axes):
    return pltpu.CompilerParams(dimension_semantics=("arbitrary",) * n_axes,
                                vmem_limit_bytes=V7X_VMEM_LIMIT_BYTES)


def _row_tile(m):
    return min(m, PROMPT_ROW_TILE)


def _rmsnorm_rows(src_ref, gain_ref, dst_ref, rows):
    step = min(rows, NORM_ROWS)

    def body(r, carry):
        rs = pl.ds(pl.multiple_of(r * step, step), step)
        ms = jnp.mean(jnp.square(src_ref[rs, :]), axis=-1, keepdims=True)
        dst_ref[rs, :] = (src_ref[rs, :] * lax.rsqrt(ms + EPS) * gain_ref[...]).astype(BF16)
        return carry

    trips = rows // step
    lax.fori_loop(0, trips, body, 0, unroll=min(trips, NORM_UNROLL))


def _load_norm_rows(x_hbm, buf_ref, gain_ref, xn_ref, sems, row0, rows):
    n = LOAD_CHUNKS if rows % (LOAD_CHUNKS * NORM_ROWS) == 0 else 1
    piece = rows // n
    copies = [pltpu.make_async_copy(x_hbm.at[pl.ds(row0 + c * piece, piece)],
                                    buf_ref.at[pl.ds(c * piece, piece)], sems.at[c]) for c in range(n)]
    for cp in copies:
        cp.start()
    for c, cp in enumerate(copies):
        cp.wait()
        _rmsnorm_rows(buf_ref.at[pl.ds(c * piece, piece)], gain_ref, xn_ref.at[pl.ds(c * piece, piece)], piece)


def _head_norm(xh, gain):
    return xh * lax.rsqrt(jnp.mean(xh * xh, axis=-1, keepdims=True) + EPS) * gain


def _head_rope(y, cosf, sinf):
    half = ROT_DIM // 2
    lane = lax.broadcasted_iota(jnp.int32, y.shape, 1)
    from_lower = pltpu.roll(y, half, axis=1)
    from_upper = pltpu.roll(y, LANES - half, axis=1)
    rot = jnp.where(lane < half, -from_upper, from_lower)
    return y * cosf + rot * sinf


def _ffn_body(*refs, tm, n_chunk, cast_next):
    if cast_next:
        (x_hbm, gain_ref, wg_ref, wu_ref, wd_ref, ng_ref, nu_ref, nd_ref,
         o_hbm, cg_ref, cu_ref, cd_ref, acc_ref, xn_ref, sem) = refs
        cg_ref[...] = ng_ref[...].astype(BF16)
        cu_ref[...] = nu_ref[...].astype(BF16)
        cd_ref[...] = nd_ref[...].astype(BF16)
    else:
        x_hbm, gain_ref, wg_ref, wu_ref, wd_ref, o_hbm, acc_ref, xn_ref, sem = refs
    i = pl.program_id(0)
    j = pl.program_id(1)
    row0 = pl.multiple_of(i * tm, tm)

    @pl.when(j == 0)
    def _load_and_norm():
        _load_norm_rows(x_hbm, acc_ref, gain_ref, xn_ref, sem, row0, tm)

    xn = xn_ref[...]
    g = jnp.dot(xn, wg_ref[...], preferred_element_type=F32)
    u = jnp.dot(xn, wu_ref[...], preferred_element_type=F32)
    a = (0.5 * (g * jax.nn.sigmoid(g)) * u).astype(BF16)
    d = acc_ref.shape[1]
    for c in range(d // n_chunk):
        cs = slice(c * n_chunk, (c + 1) * n_chunk)
        acc_ref[:, cs] += jnp.dot(a, wd_ref[:, cs], preferred_element_type=F32)

    @pl.when(j == pl.num_programs(1) - 1)
    def _store():
        cp = pltpu.make_async_copy(acc_ref, o_hbm.at[pl.ds(row0, tm)], sem.at[LOAD_CHUNKS])
        cp.start()
        cp.wait()


def _ffn(x, gain, layer, weights, next_weights=None, next_layer=0, *, tf=256):
    wg, wu, wd = weights
    m, d = x.shape
    f = wg.shape[2]
    tm = _row_tile(m)
    grid = (m // tm, f // tf)
    in_specs = [
        pl.BlockSpec(memory_space=pl.ANY),
        pl.BlockSpec((None, 1, d), lambda i, j: (layer, 0, 0)),
        pl.BlockSpec((None, d, tf), lambda i, j: (0, 0, j)),
        pl.BlockSpec((None, d, tf), lambda i, j: (0, 0, j)),
        pl.BlockSpec((None, tf, d), lambda i, j: (0, j, 0)),
    ]
    out_shape = [jax.ShapeDtypeStruct((m, d), F32)]
    out_specs = [pl.BlockSpec(memory_space=pl.ANY)]
    args = [x, gain.reshape(gain.shape[0], 1, d), wg, wu, wd]
    if next_weights is not None:
        tr = d // grid[0]
        assert d % grid[0] == 0 and tr % BF16_SUBLANES == 0 and tr % LANES == 0
        in_specs += [pl.BlockSpec((None, tr, tf), lambda i, j: (next_layer, i, j)),
                     pl.BlockSpec((None, tr, tf), lambda i, j: (next_layer, i, j)),
                     pl.BlockSpec((None, tf, tr), lambda i, j: (next_layer, j, i))]
        out_shape += [jax.ShapeDtypeStruct((1, d, f), BF16), jax.ShapeDtypeStruct((1, d, f), BF16),
                      jax.ShapeDtypeStruct((1, f, d), BF16)]
        out_specs += [pl.BlockSpec((None, tr, tf), lambda i, j: (0, i, j)),
                      pl.BlockSpec((None, tr, tf), lambda i, j: (0, i, j)),
                      pl.BlockSpec((None, tf, tr), lambda i, j: (0, j, i))]
        args += list(next_weights)
    outs = pl.pallas_call(
        functools.partial(_ffn_body, tm=tm, n_chunk=min(d, 512), cast_next=next_weights is not None),
        out_shape=out_shape,
        grid=grid,
        in_specs=in_specs,
        out_specs=out_specs,
        scratch_shapes=[pltpu.VMEM((tm, d), F32), pltpu.VMEM((tm, d), BF16),
                        pltpu.SemaphoreType.DMA((LOAD_CHUNKS + 1,))],
        compiler_params=_params(2),
        name="ffn",
    )(*args)
    return outs[0], tuple(outs[1:])


def _in_proj_body(x_hbm, gain_ref, w_ref, wgate_ref, proj_ref, gates_ref, xbuf, xn_ref, sem, *, tm, n_gates):
    i = pl.program_id(0)
    j = pl.program_id(1)

    @pl.when(j == 0)
    def _load_and_norm():
        _load_norm_rows(x_hbm, xbuf, gain_ref, xn_ref, sem, pl.multiple_of(i * tm, tm), tm)
        col = lax.broadcasted_iota(jnp.int32, wgate_ref.shape, 1)
        wgate = jnp.where(col < n_gates, wgate_ref[...], jnp.zeros_like(wgate_ref))
        gates_ref[...] = jnp.dot(xn_ref[...], wgate, preferred_element_type=F32)

    proj_ref[...] = jnp.dot(xn_ref[...], w_ref[...], preferred_element_type=F32)


def _in_proj(x, gain, w_in, n_main, layer, *, tn=512):
    m, d = x.shape
    n_gates = w_in.shape[2] - n_main
    assert n_main % tn == 0 and n_main % LANES == 0 and 0 < n_gates < LANES
    tm = _row_tile(m)
    return pl.pallas_call(
        functools.partial(_in_proj_body, tm=tm, n_gates=n_gates),
        out_shape=(jax.ShapeDtypeStruct((m, n_main), F32), jax.ShapeDtypeStruct((m, LANES), F32)),
        grid=(m // tm, n_main // tn),
        in_specs=[
            pl.BlockSpec(memory_space=pl.ANY),
            pl.BlockSpec((None, 1, d), lambda i, j: (layer, 0, 0)),
            pl.BlockSpec((None, d, tn), lambda i, j: (layer, 0, j)),
            pl.BlockSpec((None, d, LANES), lambda i, j: (layer, 0, n_main // LANES)),
        ],
        out_specs=(pl.BlockSpec((tm, tn), lambda i, j: (i, j)),
                   pl.BlockSpec((tm, LANES), lambda i, j: (i, 0))),
        scratch_shapes=[pltpu.VMEM((tm, d), F32), pltpu.VMEM((tm, d), BF16),
                        pltpu.SemaphoreType.DMA((LOAD_CHUNKS,))],
        compiler_params=_params(2),
        name="in_proj",
    )(x, gain.reshape(gain.shape[0], 1, d), w_in, w_in)


def _mlstm_body(bi_ref, bf_ref, q_ref, k_ref, v_ref, o_ref, g_ref, gmh_ref, c0_ref, n0_ref, m0_ref,
                y_ref, c_ref, n_ref, m_ref, *, chunk, valid):
    h = pl.program_id(1)
    c = pl.program_id(2)
    L = chunk

    @pl.when(c == 0)
    def _init():
        c_ref[...] = c0_ref[...]
        n_ref[...] = n0_ref[...]
        m_ref[...] = m0_ref[...]

    t_idx = lax.broadcasted_iota(jnp.int32, (L, L), 0)
    s_idx = lax.broadcasted_iota(jnp.int32, (L, L), 1)
    causal = s_idx <= t_idx
    diag = s_idx == t_idx

    gi_row = g_ref[0:1, :] + bi_ref[h]
    lf_row = jax.nn.log_sigmoid(g_ref[1:2, :] + bf_ref[h])
    lf_b = jnp.broadcast_to(lf_row, (L, L))
    gi_b = jnp.broadcast_to(gi_row, (L, L))
    lf_col = jnp.sum(jnp.where(diag, lf_b, 0.0), axis=1, keepdims=True)
    gi_col = jnp.sum(jnp.where(diag, gi_b, 0.0), axis=1, keepdims=True)
    b_col = jnp.sum(jnp.where(causal, lf_b, 0.0), axis=1, keepdims=True)
    b_row = jnp.sum(jnp.where(t_idx <= s_idx, jnp.broadcast_to(lf_col, (L, L)), 0.0),
                    axis=0, keepdims=True)

    log_d = jnp.where(causal, b_col + (gi_row - b_row), -jnp.inf)
    m_prev = m_ref[0:1, 0:1]
    inter = b_col + m_prev
    m_t = jnp.maximum(inter, jnp.max(log_d, axis=1, keepdims=True))
    decay = jnp.exp(log_d - m_t)

    dk = q_ref.shape[1]
    qf = q_ref[...]
    kf = k_ref[...] * (dk ** -0.5)
    qb = qf.astype(BF16)
    kb = kf.astype(BF16)
    vb = v_ref[...].astype(BF16)
    s = lax.dot_general(qb, kb, (((1,), (1,)), ((), ())), preferred_element_type=F32) * decay
    sc = jnp.exp(inter - m_t)
    num = jnp.dot(s.astype(BF16), vb, preferred_element_type=F32) \
        + sc * jnp.dot(qb, c_ref[...].astype(BF16), preferred_element_type=F32)
    den = jnp.sum(s, axis=1, keepdims=True) + sc * jnp.sum(qf * n_ref[...], axis=1, keepdims=True)
    hc = num / jnp.maximum(jnp.abs(den), jnp.exp(-m_t))
    hn = hc * lax.rsqrt(jnp.mean(hc * hc, axis=1, keepdims=True) + EPS) * gmh_ref[...]
    y_ref[...] = (jax.nn.sigmoid(o_ref[...]) * hn).astype(y_ref.dtype)

    m_new = m_t[valid - 1:valid, :]
    b_last = b_col[valid - 1:valid, :]
    dec = jnp.exp(b_last + m_prev - m_new)
    row = lax.broadcasted_iota(jnp.int32, (L, 1), 0)
    ws = jnp.where(row < valid, jnp.exp(b_last - b_col + gi_col - m_new), 0.0)
    kw = ws * kf
    c_ref[...] = dec * c_ref[...] + lax.dot_general(kw.astype(BF16), vb, (((0,), (0,)), ((), ())),
                                                    preferred_element_type=F32)
    n_ref[...] = dec * n_ref[...] + jnp.sum(kw, axis=0, keepdims=True)
    m_ref[...] = jnp.broadcast_to(m_new, m_ref.shape)


def _mlstm(proj, gates_row, b_i, b_f, g_mh, c0, n0, m0, *, chunk, valid):
    b, t, _ = proj.shape
    nh, dk, dv = c0.shape[1], c0.shape[2], c0.shape[3]
    k_blk0 = nh
    v_blk0 = 2 * nh * dk // dv
    o_blk0 = v_blk0 + nh
    smem = pl.BlockSpec(memory_space=pltpu.SMEM)
    state = lambda r, cdim: pl.BlockSpec((None, None, r, cdim), lambda bi, hi, ci: (bi, hi, 0, 0))
    return pl.pallas_call(
        functools.partial(_mlstm_body, chunk=chunk, valid=valid),
        out_shape=(jax.ShapeDtypeStruct((b, t, nh * dv), BF16),
                   jax.ShapeDtypeStruct(c0.shape, F32),
                   jax.ShapeDtypeStruct(n0.shape, F32),
                   jax.ShapeDtypeStruct(m0.shape, F32)),
        grid=(b, nh, t // chunk),
        in_specs=[
            smem, smem,
            pl.BlockSpec((None, chunk, dk), lambda bi, hi, ci: (bi, ci, hi)),
            pl.BlockSpec((None, chunk, dk), lambda bi, hi, ci: (bi, ci, k_blk0 + hi)),
            pl.BlockSpec((None, chunk, dv), lambda bi, hi, ci: (bi, ci, v_blk0 + hi)),
            pl.BlockSpec((None, chunk, dv), lambda bi, hi, ci: (bi, ci, o_blk0 + hi)),
            pl.BlockSpec((None, None, 2, chunk), lambda bi, hi, ci: (bi, hi, 0, ci)),
            pl.BlockSpec((None, 1, dv), lambda bi, hi, ci: (hi, 0, 0)),
            state(dk, dv), state(1, dk), state(1, LANES),
        ],
        out_specs=(pl.BlockSpec((None, chunk, dv), lambda bi, hi, ci: (bi, ci, hi)),
                   state(dk, dv), state(1, dk), state(1, LANES)),
        compiler_params=_params(3),
        name="mlstm",
    )(b_i, b_f, proj, proj, proj, proj, gates_row, g_mh.reshape(nh, 1, dv), c0, n0, m0)


def _matmul_resid_body(a_ref, w_ref, r_ref, o_ref):
    o_ref[...] = r_ref[...] + jnp.dot(a_ref[...], w_ref[...], preferred_element_type=F32)


def _matmul_resid(a, w, resid, layer, *, tn=512):
    m, k = a.shape
    n = w.shape[2]
    tm = _row_tile(m)
    return pl.pallas_call(
        _matmul_resid_body,
        out_shape=jax.ShapeDtypeStruct((m, n), F32),
        grid=(m // tm, n // tn),
        in_specs=[pl.BlockSpec((tm, k), lambda i, j: (i, 0)),
                  pl.BlockSpec((None, k, tn), lambda i, j: (layer, 0, j)),
                  pl.BlockSpec((tm, tn), lambda i, j: (i, j))],
        out_specs=pl.BlockSpec((tm, tn), lambda i, j: (i, j)),
        compiler_params=_params(2),
        name="matmul_resid",
    )(a, w, resid)


def _rope_table_body(invf_ref, cos_ref, sin_ref, *, seq, base):
    rows = cos_ref.shape[0]
    m = lax.broadcasted_iota(jnp.int32, (rows, LANES), 0) + pl.program_id(0) * rows
    pos = (base + (m & (seq - 1))).astype(F32)
    ang = pos * invf_ref[...]
    cos_ref[...] = jnp.cos(ang)
    sin_ref[...] = jnp.sin(ang)


def _rope_tables(m, seq, base):
    assert seq & (seq - 1) == 0
    half = ROT_DIM // 2
    inv_freq = ROPE_THETA ** (-jnp.arange(half, dtype=F32) / half)
    invf = jnp.zeros((1, LANES), F32).at[0, :ROT_DIM].set(jnp.tile(inv_freq, 2))
    rows = min(m, 512)
    spec = pl.BlockSpec((rows, LANES), lambda i: (i, 0))
    return pl.pallas_call(
        functools.partial(_rope_table_body, seq=seq, base=base),
        out_shape=(jax.ShapeDtypeStruct((m, LANES), F32),) * 2,
        grid=(m // rows,),
        in_specs=[pl.BlockSpec((1, LANES), lambda i: (0, 0))],
        out_specs=(spec, spec),
        compiler_params=_params(1),
        name="rope_tables",
    )(invf)


def _epilogue_heads(res_ref, gain_ref, cos_ref, sin_ref, out_ref, rows):
    step = min(rows, EPI_ROWS)
    heads = [slice(hh * ATT_HEAD_DIM, (hh + 1) * ATT_HEAD_DIM) for hh in range(res_ref.shape[1] // ATT_HEAD_DIM)]

    def norm_body(r, carry):
        rs = pl.ds(pl.multiple_of(r * step, step), step)
        for hs in heads:
            res_ref[rs, hs] = _head_norm(res_ref[rs, hs], gain_ref[...])
        return carry

    def rope_body(r, carry):
        rs = pl.ds(pl.multiple_of(r * step, step), step)
        cosf = cos_ref[rs, :]
        sinf = sin_ref[rs, :]
        for hs in heads:
            out_ref[rs, hs] = _head_rope(res_ref[rs, hs], cosf, sinf).astype(out_ref.dtype)
        return carry

    lax.fori_loop(0, rows // step, norm_body, 0)
    lax.fori_loop(0, rows // step, rope_body, 0)


def _kv_proj_body(x_hbm, gain_ref, w_ref, kgain_ref, cos_ref, sin_ref, k_ref, v_ref,
                  xbuf, xn_ref, res_ref, sem, *, tm):
    i = pl.program_id(0)
    j = pl.program_id(1)

    @pl.when(j == 0)
    def _k_half():
        _load_norm_rows(x_hbm, xbuf, gain_ref, xn_ref, sem, pl.multiple_of(i * tm, tm), tm)
        res_ref[...] = jnp.dot(xn_ref[...], w_ref[...], preferred_element_type=F32)
        _epilogue_heads(res_ref, kgain_ref, cos_ref, sin_ref, k_ref, tm)

    @pl.when(j == 1)
    def _v_half():
        v_ref[...] = jnp.dot(xn_ref[...], w_ref[...], preferred_element_type=F32)


def _kv_proj(x, gain, w_kv, g_knorm, cosf, sinf):
    m, d = x.shape
    n = w_kv.shape[1] // 2
    tm = min(m, PROJ_ROW_TILE)
    out_spec = pl.BlockSpec((tm, n), lambda i, j: (i, 0))
    tab_spec = pl.BlockSpec((tm, LANES), lambda i, j: (i, 0))
    return pl.pallas_call(
        functools.partial(_kv_proj_body, tm=tm),
        out_shape=(jax.ShapeDtypeStruct((m, n), F32),) * 2,
        grid=(m // tm, 2),
        in_specs=[pl.BlockSpec(memory_space=pl.ANY),
                  pl.BlockSpec((1, d), lambda i, j: (0, 0)),
                  pl.BlockSpec((d, n), lambda i, j: (0, j)),
                  pl.BlockSpec((1, ATT_HEAD_DIM), lambda i, j: (0, 0)),
                  tab_spec, tab_spec],
        out_specs=(out_spec, out_spec),
        scratch_shapes=[pltpu.VMEM((tm, d), F32), pltpu.VMEM((tm, d), BF16), pltpu.VMEM((tm, n), F32),
                        pltpu.SemaphoreType.DMA((LOAD_CHUNKS,))],
        compiler_params=_params(2),
        name="kv_proj",
    )(x, gain.reshape(1, d), w_kv, g_knorm.reshape(1, ATT_HEAD_DIM), cosf, sinf)


def _q_proj_body(x_hbm, gain_ref, w_ref, qgain_ref, cos_ref, sin_ref, q_ref, xbuf, xn_ref, res_ref, sem, *, tm):
    i = pl.program_id(0)
    j = pl.program_id(1)

    @pl.when(j == 0)
    def _load_and_norm():
        _load_norm_rows(x_hbm, xbuf, gain_ref, xn_ref, sem, pl.multiple_of(i * tm, tm), tm)

    res_ref[...] = jnp.dot(xn_ref[...], w_ref[...], preferred_element_type=F32)
    _epilogue_heads(res_ref, qgain_ref, cos_ref, sin_ref, q_ref, tm)


def _q_proj(x, gain, w_q, g_qnorm, cosf, sinf, layer_mix, layer_b):
    m, d = x.shape
    n_groups = g_qnorm.shape[1]
    n = w_q.shape[2]
    tn = n // n_groups
    tm = min(m, PROJ_ROW_TILE)
    tab_spec = pl.BlockSpec((tm, LANES), lambda i, j: (i, 0))
    return pl.pallas_call(
        functools.partial(_q_proj_body, tm=tm),
        out_shape=jax.ShapeDtypeStruct((m, n), BF16),
        grid=(m // tm, n_groups),
        in_specs=[pl.BlockSpec(memory_space=pl.ANY),
                  pl.BlockSpec((None, 1, d), lambda i, j: (layer_mix, 0, 0)),
                  pl.BlockSpec((None, d, tn), lambda i, j: (layer_b, 0, j)),
                  pl.BlockSpec((None, None, 1, ATT_HEAD_DIM), lambda i, j: (layer_b, j, 0, 0)),
                  tab_spec, tab_spec],
        out_specs=pl.BlockSpec((tm, tn), lambda i, j: (i, j)),
        scratch_shapes=[pltpu.VMEM((tm, d), F32), pltpu.VMEM((tm, d), BF16), pltpu.VMEM((tm, tn), F32),
                        pltpu.SemaphoreType.DMA((LOAD_CHUNKS,))],
        compiler_params=_params(2),
        name="q_proj",
    )(x, gain.reshape(gain.shape[0], 1, d), w_q,
      g_qnorm.reshape(g_qnorm.shape[0], n_groups, 1, ATT_HEAD_DIM), cosf, sinf)


def _softmax_pieces(scores, masks):
    masked = [jnp.where(mk, sc, -jnp.inf) for sc, mk in zip(scores, masks)]
    m = functools.reduce(jnp.maximum, [jnp.max(sc, axis=1, keepdims=True) for sc in masked])
    probs = [jnp.exp(sc - m) for sc in masked]
    l = functools.reduce(jnp.add, [jnp.sum(p, axis=1, keepdims=True) for p in probs])
    return probs, l, m


def _qk(qh, kh):
    return lax.dot_general(qh, kh, (((1,), (1,)), ((), ())), preferred_element_type=F32)


def _attn_prompt_body(*refs, tq, win, has_prev, n_heads):
    if has_prev:
        q_ref, kc_ref, vc_ref, kp_ref, vp_ref, o_ref, lse_ref = refs
    else:
        q_ref, kc_ref, vc_ref, o_ref, lse_ref = refs
    qi = pl.program_id(2)
    scale = ATT_HEAD_DIM ** -0.5
    row = lax.broadcasted_iota(jnp.int32, (tq, tq), 0)
    col = lax.broadcasted_iota(jnp.int32, (tq, tq), 1)
    mask_c = (col <= row) & (row - col <= win)
    if has_prev:
        rowp = lax.broadcasted_iota(jnp.int32, (tq, win), 0)
        colp = lax.broadcasted_iota(jnp.int32, (tq, win), 1)
        mask_p = (colp >= rowp) & (qi > 0)
    for hh in range(n_heads):
        hs = slice(hh * ATT_HEAD_DIM, (hh + 1) * ATT_HEAD_DIM)
        qh = q_ref[:, hs]
        scores = [_qk(qh, kc_ref[:, hs].astype(BF16)) * scale]
        masks = [mask_c]
        values = [vc_ref[:, hs].astype(BF16)]
        if has_prev:
            scores.append(_qk(qh, kp_ref[:, hs].astype(BF16)) * scale)
            masks.append(mask_p)
            values.append(vp_ref[:, hs].astype(BF16))
        probs, l, m = _softmax_pieces(scores, masks)
        acc = functools.reduce(jnp.add, [jnp.dot(p.astype(BF16), v, preferred_element_type=F32)
                                         for p, v in zip(probs, values)])
        o_ref[:, hs] = acc / l
        lse_ref[:, hs] = jnp.broadcast_to(m + jnp.log(l), (tq, ATT_HEAD_DIM))


def _attn_prompt_group(q, k, v, group, window, dil):
    b, t, n = k.shape
    n_groups = q.shape[2] // n
    tc = t // dil
    win = window // dil
    tq = min(tc, ATTN_Q_TILE)
    has_prev = tc > tq
    assert tq % win == 0 and tc % tq == 0
    qv = q.reshape(b, tc, dil * n_groups * n)
    kv = k.reshape(b, tc, dil * n)
    vv = v.reshape(b, tc, dil * n)
    cur = pl.BlockSpec((None, tq, n), lambda bi, r, qi: (bi, qi, r))
    prev = pl.BlockSpec((None, win, n), lambda bi, r, qi: (bi, jnp.maximum(qi * (tq // win) - 1, 0), r))
    in_specs = [pl.BlockSpec((None, tq, n), lambda bi, r, qi: (bi, qi, r * n_groups + group)), cur, cur]
    args = [qv, kv, vv]
    if has_prev:
        in_specs += [prev, prev]
        args += [kv, vv]
    o, lse = pl.pallas_call(
        functools.partial(_attn_prompt_body, tq=tq, win=win, has_prev=has_prev, n_heads=n // ATT_HEAD_DIM),
        out_shape=(jax.ShapeDtypeStruct((b, tc, dil * n), F32),) * 2,
        grid=(b, dil, tc // tq),
        in_specs=in_specs,
        out_specs=(cur, cur),
        compiler_params=_params(3),
        name=f"attn_prompt_d{dil}",
    )(*args)
    return o.reshape(b * t, n), lse.reshape(b * t, n)


def _attn_out_body(*refs, n_groups):
    o_refs = refs[:n_groups]
    l_refs = refs[n_groups:2 * n_groups]
    w_ref, r_ref, out_ref, att_ref = refs[2 * n_groups:]

    @pl.when(pl.program_id(1) == 0)
    def _mix():
        lses = [l[...] for l in l_refs]
        m = functools.reduce(jnp.maximum, lses)
        es = [jnp.exp(l - m) for l in lses]
        tot = functools.reduce(jnp.add, es)
        att = functools.reduce(jnp.add, [(e / tot) * o[...] for e, o in zip(es, o_refs)])
        att_ref[...] = att.astype(BF16)

    out_ref[...] = r_ref[...] + jnp.dot(att_ref[...], w_ref[...], preferred_element_type=F32)


def _attn_out(outs, lses, w_o, resid, layer, *, tn=512, tm=512):
    m, k = outs[0].shape
    n = w_o.shape[2]
    tm = min(m, tm)
    blk = pl.BlockSpec((tm, k), lambda i, j: (i, 0))
    return pl.pallas_call(
        functools.partial(_attn_out_body, n_groups=len(outs)),
        out_shape=jax.ShapeDtypeStruct((m, n), F32),
        grid=(m // tm, n // tn),
        in_specs=[blk] * (2 * len(outs)) + [pl.BlockSpec((None, k, tn), lambda i, j: (layer, 0, j)),
                                            pl.BlockSpec((tm, tn), lambda i, j: (i, j))],
        out_specs=pl.BlockSpec((tm, tn), lambda i, j: (i, j)),
        scratch_shapes=[pltpu.VMEM((tm, k), BF16)],
        compiler_params=_params(2),
        name="attn_out",
    )(*outs, *lses, w_o, resid)


def _attn_sample_body(q_ref, kc_ref, vc_ref, kn_ref, vn_ref, o_ref, *, groups, n_heads, n_new):
    rows = q_ref.shape[0]
    cache = kc_ref.shape[0]
    n = n_heads * ATT_HEAD_DIM
    scale = ATT_HEAD_DIM ** -0.5
    dist_c = cache + lax.broadcasted_iota(jnp.int32, (rows, cache), 0) \
        - lax.broadcasted_iota(jnp.int32, (rows, cache), 1)
    dist_n = lax.broadcasted_iota(jnp.int32, (rows, rows), 0) - lax.broadcasted_iota(jnp.int32, (rows, rows), 1)
    col_n = lax.broadcasted_iota(jnp.int32, (rows, rows), 1)
    for hh in range(n_heads):
        hs = slice(hh * ATT_HEAD_DIM, (hh + 1) * ATT_HEAD_DIM)
        kc = kc_ref[:, hs].astype(BF16)
        vc = vc_ref[:, hs].astype(BF16)
        kn = kn_ref[:, hs].astype(BF16)
        vn = vn_ref[:, hs].astype(BF16)
        outs, lses = [], []
        for g, (window, dil) in enumerate(groups):
            qh = q_ref[:, g * n + hh * ATT_HEAD_DIM:g * n + (hh + 1) * ATT_HEAD_DIM]
            mask_c = (dist_c <= window) & ((dist_c & (dil - 1)) == 0)
            mask_n = (dist_n >= 0) & (col_n < n_new) & ((dist_n & (dil - 1)) == 0)
            probs, l, m = _softmax_pieces([_qk(qh, kc) * scale, _qk(qh, kn) * scale], [mask_c, mask_n])
            acc = jnp.dot(probs[0].astype(BF16), vc, preferred_element_type=F32) \
                + jnp.dot(probs[1].astype(BF16), vn, preferred_element_type=F32)
            outs.append(acc / l)
            lses.append(m + jnp.log(l))
        m_all = functools.reduce(jnp.maximum, lses)
        es = [jnp.exp(l - m_all) for l in lses]
        tot = functools.reduce(jnp.add, es)
        o_ref[:, hs] = functools.reduce(jnp.add, [(e / tot) * o for e, o in zip(es, outs)]).astype(o_ref.dtype)


def _attn_sample(q, k_new, v_new, cache_k, cache_v, n_new):
    b, rows, n = k_new.shape
    cache = cache_k.shape[1]
    assert cache == MAX_WINDOW and all(d & (d - 1) == 0 for _, d in DILATED_GROUPS)
    row_blk = lambda width: pl.BlockSpec((None, rows, width), lambda bi: (bi, 0, 0))
    cache_blk = pl.BlockSpec((None, cache, n), lambda bi: (bi, 0, 0))
    return pl.pallas_call(
        functools.partial(_attn_sample_body, groups=DILATED_GROUPS, n_heads=n // ATT_HEAD_DIM, n_new=n_new),
        out_shape=jax.ShapeDtypeStruct((b, rows, n), BF16),
        grid=(b,),
        in_specs=[row_blk(q.shape[2]), cache_blk, cache_blk, row_blk(n), row_blk(n)],
        out_specs=row_blk(n),
        compiler_params=_params(1),
        name="attn_sample",
    )(q, cache_k, cache_v, k_new, v_new)


FFN_ORDER = (("ffn1", 0), ("ffn2", 0), ("ffn1", 1), ("ffn2", 1))


def _run_ffn(x, p, ffn_bf16, step):
    name, layer = FFN_ORDER[step]
    nxt, nxt_layer = None, 0
    if step + 1 < len(FFN_ORDER) and step + 1 not in ffn_bf16:
        nxt_name, nxt_layer = FFN_ORDER[step + 1]
        nxt = tuple(p[f"w_{nxt_name}_{part}"] for part in ("gate", "up", "down"))
    x, cast = _ffn(x, p[f"norm_{name}"], layer, ffn_bf16[step], nxt, nxt_layer)
    if nxt is not None:
        ffn_bf16[step + 1] = cast
    return x


def _trunk(x, seq, base, cache_k, cache_v, c0, n0, m0, p, ffn_bf16):
    m_rows, d = x.shape
    b = m_rows // seq
    nh = MLSTM_HEADS
    dk, dv = c0.shape[2], c0.shape[3]

    x = _run_ffn(x, p, ffn_bf16, 0)
    proj, gates = _in_proj(x, p["norm_mix"], p["w_in_a"], 2 * nh * dk + nh * dv + d, 0)
    if seq % MLSTM_PROMPT_CHUNK == 0:
        chunk, valid, t_pad = MLSTM_PROMPT_CHUNK, MLSTM_PROMPT_CHUNK, seq
    else:
        assert seq <= BF16_SUBLANES
        chunk, valid, t_pad = BF16_SUBLANES, seq, BF16_SUBLANES
    proj3 = proj.reshape(b, seq, -1)
    gates3 = gates.reshape(b, seq, LANES)[:, :, :2 * nh]
    if t_pad != seq:
        proj3 = jnp.pad(proj3, ((0, 0), (0, t_pad - seq), (0, 0)))
        gates3 = jnp.pad(gates3, ((0, 0), (0, t_pad - seq), (0, 0)))
    gates_row = gates3.reshape(b, t_pad, 2, nh).transpose(0, 3, 2, 1)
    y, c_fin, n_fin, m_fin = _mlstm(proj3, gates_row, p["b_ig"][0], p["b_fg"][0], p["g_mh"][0],
                                    c0, n0, m0, chunk=chunk, valid=valid)
    y = y[:, :seq].reshape(m_rows, nh * dv)
    x = _matmul_resid(y, p["w_out_a"], x, 0)
    x = _run_ffn(x, p, ffn_bf16, 1)

    cosf, sinf = _rope_tables(m_rows, seq, base)
    k, v = _kv_proj(x, p["norm_kv"], p["w_kv"], p["g_knorm"], cosf, sinf)

    x = _run_ffn(x, p, ffn_bf16, 2)
    q = _q_proj(x, p["norm_mix"], p["w_q_b"], p["g_qnorm"], cosf, sinf, 1, 0)
    n_kv = k.shape[1]
    if cache_k is None:
        outs, lses = [], []
        for g, (window, dil) in enumerate(DILATED_GROUPS):
            o_g, l_g = _attn_prompt_group(q.reshape(b, seq, -1), k.reshape(b, seq, n_kv),
                                          v.reshape(b, seq, n_kv), g, window, dil)
            outs.append(o_g)
            lses.append(l_g)
        x = _attn_out(outs, lses, p["w_o_b"], x, 0)
    else:
        pad = ((0, 0), (0, BF16_SUBLANES - seq), (0, 0))
        att = _attn_sample(jnp.pad(q.reshape(b, seq, -1), pad), jnp.pad(k.reshape(b, seq, n_kv), pad),
                           jnp.pad(v.reshape(b, seq, n_kv), pad), cache_k, cache_v, seq)
        x = _matmul_resid(att[:, :seq].reshape(m_rows, n_kv), p["w_o_b"], x, 0)
    x = _run_ffn(x, p, ffn_bf16, 3)
    return x, k, v, c_fin, n_fin, m_fin


def kernel(x_prompt, x_sample, cache_k, cache_v, state_C, state_n, state_m, norm_ffn1, w_ffn1_gate, w_ffn1_up, w_ffn1_down, norm_mix, norm_ffn2, w_ffn2_gate, w_ffn2_up, w_ffn2_down, w_in_a, b_ig, b_fg, g_mh, w_out_a, norm_kv, w_kv, g_knorm, w_q_b, g_qnorm, w_o_b):
    bp, seq_p, d = x_prompt.shape
    bs, seq_s, _ = x_sample.shape
    past = cache_k.shape[1]
    assert past == min(MAX_WINDOW, PAST_LEN)
    nh = MLSTM_HEADS
    dk, dv = state_C.shape[3], state_C.shape[4]
    n_kv = cache_k.shape[2] * cache_k.shape[3]

    bf = lambda w: w.astype(BF16)
    p = dict(norm_ffn1=norm_ffn1, norm_mix=norm_mix, norm_ffn2=norm_ffn2, norm_kv=norm_kv,
             w_ffn1_gate=w_ffn1_gate, w_ffn1_up=w_ffn1_up, w_ffn1_down=w_ffn1_down,
             w_ffn2_gate=w_ffn2_gate, w_ffn2_up=w_ffn2_up, w_ffn2_down=w_ffn2_down,
             w_in_a=bf(w_in_a), b_ig=b_ig, b_fg=b_fg, g_mh=g_mh, w_out_a=bf(w_out_a), w_kv=bf(w_kv),
             g_knorm=g_knorm, w_q_b=bf(w_q_b), g_qnorm=g_qnorm, w_o_b=bf(w_o_b))
    ffn_bf16 = {0: tuple(bf(w[:1]) for w in (w_ffn1_gate, w_ffn1_up, w_ffn1_down))}

    zeros = lambda *s: jnp.zeros(s, F32)
    y_p, k_p, v_p, c_p, n_p, m_p = _trunk(
        x_prompt.reshape(bp * seq_p, d), seq_p, 0, None, None,
        zeros(bp, nh, dk, dv), zeros(bp, nh, 1, dk), zeros(bp, nh, 1, LANES), p, ffn_bf16)
    y_s, k_s, v_s, c_s, n_s, m_s = _trunk(
        x_sample.reshape(bs * seq_s, d), seq_s, PAST_LEN,
        cache_k.reshape(bs, past, n_kv), cache_v.reshape(bs, past, n_kv),
        state_C[:, 0], state_n[:, 0][:, :, None, :],
        jnp.broadcast_to(state_m[:, 0][:, :, None, None], (bs, nh, 1, LANES)), p, ffn_bf16)

    hkv, hd = cache_k.shape[2], cache_k.shape[3]
    keep = min(MAX_WINDOW, seq_p)
    return (y_p.reshape(bp, seq_p, d), y_s.reshape(bs, seq_s, d),
            k_p.reshape(bp, seq_p, hkv, hd)[:, seq_p - keep:], v_p.reshape(bp, seq_p, hkv, hd)[:, seq_p - keep:],
            k_s.reshape(bs, seq_s, hkv, hd), v_s.reshape(bs, seq_s, hkv, hd),
            c_p[:, None], n_p[:, None, :, 0], m_p[:, None, :, 0, 0],
            c_s[:, None], n_s[:, None, :, 0], m_s[:, None, :, 0, 0])
```

```python
import functools

import jax
import jax.numpy as jnp
from jax import lax
from jax.experimental import pallas as pl
from jax.experimental.pallas import tpu as pltpu

F32 = jnp.float32
BF16 = jnp.bfloat16

EPS = 1e-6
MLSTM_HEADS = 8
ATT_HEAD_DIM = 128
DILATED_GROUPS = ((128, 1), (512, 4), (2048, 16))
MAX_WINDOW = max(w for w, _ in DILATED_GROUPS)
ROT_DIM = ATT_HEAD_DIM // 4
ROPE_THETA = 500000.0
PAST_LEN = 16384

V7X_VMEM_LIMIT_BYTES = 56 * 1024 * 1024
LANES = 128
BF16_SUBLANES = 16
NORM_ROWS = 16
NORM_UNROLL = 4
LOAD_CHUNKS = 8
EPI_ROWS = 64
PROMPT_ROW_TILE = 1024
PROJ_ROW_TILE = 512
MLSTM_PROMPT_CHUNK = 256
ATTN_Q_TILE = 256


def _params(n_axes):
    return pltpu.CompilerParams(dimension_semantics=("arbitrary",) * n_axes,
                                vmem_limit_bytes=V7X_VMEM_LIMIT_BYTES)


def _row_tile(m):
    return min(m, PROMPT_ROW_TILE)


def _round_up(x, mult):
    return -(-x // mult) * mult


def _rmsnorm_rows(src_ref, gain_ref, dst_ref, rows):
    step = min(rows, NORM_ROWS)

    def body(r, carry):
        rs = pl.ds(pl.multiple_of(r * step, step), step)
        ms = jnp.mean(jnp.square(src_ref[rs, :]), axis=-1, keepdims=True)
        dst_ref[rs, :] = (src_ref[rs, :] * lax.rsqrt(ms + EPS) * gain_ref[...]).astype(BF16)
        return carry

    trips = rows // step
    lax.fori_loop(0, trips, body, 0, unroll=min(trips, NORM_UNROLL))


def _load_norm_rows(x_hbm, buf_ref, gain_ref, xn_ref, sems, row0, rows):
    n = LOAD_CHUNKS if rows % (LOAD_CHUNKS * NORM_ROWS) == 0 else 1
    piece = rows // n
    copies = [pltpu.make_async_copy(x_hbm.at[pl.ds(row0 + c * piece, piece)],
                                    buf_ref.at[pl.ds(c * piece, piece)], sems.at[c]) for c in range(n)]
    for cp in copies:
        cp.start()
    for c, cp in enumerate(copies):
        cp.wait()
        _rmsnorm_rows(buf_ref.at[pl.ds(c * piece, piece)], gain_ref, xn_ref.at[pl.ds(c * piece, piece)], piece)


def _head_norm(xh, gain):
    return xh * lax.rsqrt(jnp.mean(xh * xh, axis=-1, keepdims=True) + EPS) * gain


def _head_rope(y, cosf, sinf):
    half = ROT_DIM // 2
    lane = lax.broadcasted_iota(jnp.int32, y.shape, 1)
    from_lower = pltpu.roll(y, half, axis=1)
    from_upper = pltpu.roll(y, LANES - half, axis=1)
    rot = jnp.where(lane < half, -from_upper, from_lower)
    return y * cosf + rot * sinf


def _as_bf16(x):
    return x if x.dtype == BF16 else x.astype(BF16)


def _cast_specs(src, layer, grid, rows_follow_j):
    gi, gj = grid
    _, r, c = src.shape
    if rows_follow_j:
        rb, cb = _round_up(pl.cdiv(r, gj), BF16_SUBLANES), c // gi
        assert c % gi == 0 and cb % LANES == 0
        last = pl.cdiv(r, rb) - 1
        index = lambda lay: (lambda i, j: (lay, jnp.minimum(j, last), i))
    else:
        rb, cb = r // gi, _round_up(pl.cdiv(c, gj), LANES)
        assert r % gi == 0 and rb % BF16_SUBLANES == 0
        last = pl.cdiv(c, cb) - 1
        index = lambda lay: (lambda i, j: (lay, i, jnp.minimum(j, last)))
    return (pl.BlockSpec((None, rb, cb), index(layer)), pl.BlockSpec((None, rb, cb), index(0)),
            jax.ShapeDtypeStruct((1, r, c), BF16))


def _ffn_body(*refs, tm, n_chunk, n_casts, emit):
    x_hbm, gain_ref, wg_ref, wu_ref, wd_ref = refs[:5]
    cast_in = refs[5:5 + n_casts]
    o_hbm = refs[5 + n_casts]
    n_emit = 3 if emit else 0
    emit_refs = refs[6 + n_casts:6 + n_casts + n_emit]
    cast_out = refs[6 + n_casts + n_emit:6 + 2 * n_casts + n_emit]
    acc_ref, xn_ref, sem = refs[6 + 2 * n_casts + n_emit:]

    for src, dst in zip(cast_in, cast_out):
        dst[...] = src[...].astype(BF16)

    i = pl.program_id(0)
    j = pl.program_id(1)
    row0 = pl.multiple_of(i * tm, tm)

    @pl.when(j == 0)
    def _load_and_norm():
        _load_norm_rows(x_hbm, acc_ref, gain_ref, xn_ref, sem, row0, tm)

    wg, wu, wd = wg_ref[...], wu_ref[...], wd_ref
    if emit:
        wg, wu, wd = wg.astype(BF16), wu.astype(BF16), wd_ref[...].astype(BF16)
        emit_refs[0][...] = wg
        emit_refs[1][...] = wu
        emit_refs[2][...] = wd
    xn = xn_ref[...]
    g = jnp.dot(xn, wg, preferred_element_type=F32)
    u = jnp.dot(xn, wu, preferred_element_type=F32)
    a = (0.5 * (g * jax.nn.sigmoid(g)) * u).astype(BF16)
    d = acc_ref.shape[1]
    for c in range(d // n_chunk):
        cs = slice(c * n_chunk, (c + 1) * n_chunk)
        acc_ref[:, cs] += jnp.dot(a, wd[:, cs], preferred_element_type=F32)

    @pl.when(j == pl.num_programs(1) - 1)
    def _store():
        cp = pltpu.make_async_copy(acc_ref, o_hbm.at[pl.ds(row0, tm)], sem.at[LOAD_CHUNKS])
        cp.start()
        cp.wait()


def _ffn(x, gain, layer, weights, w_layer=0, casts=(), emit=False, *, tf=256):
    wg, wu, wd = weights
    m, d = x.shape
    f = wg.shape[2]
    tm = _row_tile(m)
    grid = (m // tm, f // tf)
    assert not emit or grid[0] == 1
    in_specs = [
        pl.BlockSpec(memory_space=pl.ANY),
        pl.BlockSpec((None, 1, d), lambda i, j: (layer, 0, 0)),
        pl.BlockSpec((None, d, tf), lambda i, j: (w_layer, 0, j)),
        pl.BlockSpec((None, d, tf), lambda i, j: (w_layer, 0, j)),
        pl.BlockSpec((None, tf, d), lambda i, j: (w_layer, j, 0)),
    ]
    out_shape = [jax.ShapeDtypeStruct((m, d), F32)]
    out_specs = [pl.BlockSpec(memory_space=pl.ANY)]
    if emit:
        out_shape += [jax.ShapeDtypeStruct((1, d, f), BF16)] * 2 + [jax.ShapeDtypeStruct((1, f, d), BF16)]
        out_specs += [pl.BlockSpec((None, d, tf), lambda i, j: (0, 0, j))] * 2 \
            + [pl.BlockSpec((None, tf, d), lambda i, j: (0, j, 0))]
    for src, src_layer, rows_follow_j in casts:
        in_spec, out_spec, shape = _cast_specs(src, src_layer, grid, rows_follow_j)
        in_specs.append(in_spec)
        out_specs.append(out_spec)
        out_shape.append(shape)
    outs = pl.pallas_call(
        functools.partial(_ffn_body, tm=tm, n_chunk=min(d, 512), n_casts=len(casts), emit=emit),
        out_shape=out_shape,
        grid=grid,
        in_specs=in_specs,
        out_specs=out_specs,
        scratch_shapes=[pltpu.VMEM((tm, d), F32), pltpu.VMEM((tm, d), BF16),
                        pltpu.SemaphoreType.DMA((LOAD_CHUNKS + 1,))],
        compiler_params=_params(2),
        name="ffn",
    )(x, gain.reshape(gain.shape[0], 1, d), wg, wu, wd, *[src for src, _, _ in casts])
    n_emit = 3 if emit else 0
    return outs[0], tuple(outs[1:1 + n_emit]), tuple(outs[1 + n_emit:])


def _in_proj_body(x_hbm, gain_ref, w_ref, wgate_ref, proj_ref, gates_ref, xbuf, xn_ref, sem, *, tm, n_gates):
    i = pl.program_id(0)
    j = pl.program_id(1)

    @pl.when(j == 0)
    def _load_and_norm():
        _load_norm_rows(x_hbm, xbuf, gain_ref, xn_ref, sem, pl.multiple_of(i * tm, tm), tm)
        col = lax.broadcasted_iota(jnp.int32, wgate_ref.shape, 1)
        wgate = jnp.where(col < n_gates, wgate_ref[...], jnp.zeros_like(wgate_ref))
        gates_ref[...] = jnp.dot(xn_ref[...], wgate, preferred_element_type=F32)

    proj_ref[...] = jnp.dot(xn_ref[...], w_ref[...], preferred_element_type=F32)


def _in_proj(x, gain, w_in, n_main, layer, *, tn=512):
    m, d = x.shape
    n_gates = w_in.shape[2] - n_main
    assert n_main % tn == 0 and n_main % LANES == 0 and 0 < n_gates < LANES
    tm = _row_tile(m)
    return pl.pallas_call(
        functools.partial(_in_proj_body, tm=tm, n_gates=n_gates),
        out_shape=(jax.ShapeDtypeStruct((m, n_main), F32), jax.ShapeDtypeStruct((m, LANES), F32)),
        grid=(m // tm, n_main // tn),
        in_specs=[
            pl.BlockSpec(memory_space=pl.ANY),
            pl.BlockSpec((None, 1, d), lambda i, j: (layer, 0, 0)),
            pl.BlockSpec((None, d, tn), lambda i, j: (layer, 0, j)),
            pl.BlockSpec((None, d, LANES), lambda i, j: (layer, 0, n_main // LANES)),
        ],
        out_specs=(pl.BlockSpec((tm, tn), lambda i, j: (i, j)),
                   pl.BlockSpec((tm, LANES), lambda i, j: (i, 0))),
        scratch_shapes=[pltpu.VMEM((tm, d), F32), pltpu.VMEM((tm, d), BF16),
                        pltpu.SemaphoreType.DMA((LOAD_CHUNKS,))],
        compiler_params=_params(2),
        name="in_proj",
    )(x, gain.reshape(gain.shape[0], 1, d), w_in, w_in)


def _mlstm_body(bi_ref, bf_ref, q_ref, k_ref, v_ref, o_ref, g_ref, gmh_ref, c0_ref, n0_ref, m0_ref,
                y_ref, c_ref, n_ref, m_ref, *, chunk, valid):
    h = pl.program_id(1)
    c = pl.program_id(2)
    L = chunk

    @pl.when(c == 0)
    def _init():
        c_ref[...] = c0_ref[...]
        n_ref[...] = n0_ref[...]
        m_ref[...] = m0_ref[...]

    t_idx = lax.broadcasted_iota(jnp.int32, (L, L), 0)
    s_idx = lax.broadcasted_iota(jnp.int32, (L, L), 1)
    causal = s_idx <= t_idx
    diag = s_idx == t_idx

    gi_row = g_ref[0:1, :] + bi_ref[h]
    lf_row = jax.nn.log_sigmoid(g_ref[1:2, :] + bf_ref[h])
    lf_b = jnp.broadcast_to(lf_row, (L, L))
    gi_b = jnp.broadcast_to(gi_row, (L, L))
    lf_col = jnp.sum(jnp.where(diag, lf_b, 0.0), axis=1, keepdims=True)
    gi_col = jnp.sum(jnp.where(diag, gi_b, 0.0), axis=1, keepdims=True)
    b_col = jnp.sum(jnp.where(causal, lf_b, 0.0), axis=1, keepdims=True)
    b_row = jnp.sum(jnp.where(t_idx <= s_idx, jnp.broadcast_to(lf_col, (L, L)), 0.0),
                    axis=0, keepdims=True)

    log_d = jnp.where(causal, b_col + (gi_row - b_row), -jnp.inf)
    m_prev = m_ref[0:1, 0:1]
    inter = b_col + m_prev
    m_t = jnp.maximum(inter, jnp.max(log_d, axis=1, keepdims=True))
    decay = jnp.exp(log_d - m_t)

    dk = q_ref.shape[1]
    qf = q_ref[...]
    kf = k_ref[...] * (dk ** -0.5)
    qb = qf.astype(BF16)
    kb = kf.astype(BF16)
    vb = v_ref[...].astype(BF16)
    s = lax.dot_general(qb, kb, (((1,), (1,)), ((), ())), preferred_element_type=F32) * decay
    sc = jnp.exp(inter - m_t)
    num = jnp.dot(s.astype(BF16), vb, preferred_element_type=F32) \
        + sc * jnp.dot(qb, c_ref[...].astype(BF16), preferred_element_type=F32)
    den = jnp.sum(s, axis=1, keepdims=True) + sc * jnp.sum(qf * n_ref[...], axis=1, keepdims=True)
    hc = num / jnp.maximum(jnp.abs(den), jnp.exp(-m_t))
    hn = hc * lax.rsqrt(jnp.mean(hc * hc, axis=1, keepdims=True) + EPS) * gmh_ref[...]
    y_ref[...] = (jax.nn.sigmoid(o_ref[...]) * hn).astype(y_ref.dtype)

    m_new = m_t[valid - 1:valid, :]
    b_last = b_col[valid - 1:valid, :]
    dec = jnp.exp(b_last + m_prev - m_new)
    row = lax.broadcasted_iota(jnp.int32, (L, 1), 0)
    ws = jnp.where(row < valid, jnp.exp(b_last - b_col + gi_col - m_new), 0.0)
    kw = ws * kf
    c_ref[...] = dec * c_ref[...] + lax.dot_general(kw.astype(BF16), vb, (((0,), (0,)), ((), ())),
                                                    preferred_element_type=F32)
    n_ref[...] = dec * n_ref[...] + jnp.sum(kw, axis=0, keepdims=True)
    m_ref[...] = jnp.broadcast_to(m_new, m_ref.shape)


def _mlstm(proj, gates_row, b_i, b_f, g_mh, c0, n0, m0, *, chunk, valid):
    b, t, _ = proj.shape
    nh, dk, dv = c0.shape[1], c0.shape[2], c0.shape[3]
    k_blk0 = nh
    v_blk0 = 2 * nh * dk // dv
    o_blk0 = v_blk0 + nh
    smem = pl.BlockSpec(memory_space=pltpu.SMEM)
    state = lambda r, cdim: pl.BlockSpec((None, None, r, cdim), lambda bi, hi, ci: (bi, hi, 0, 0))
    return pl.pallas_call(
        functools.partial(_mlstm_body, chunk=chunk, valid=valid),
        out_shape=(jax.ShapeDtypeStruct((b, t, nh * dv), BF16),
                   jax.ShapeDtypeStruct(c0.shape, F32),
                   jax.ShapeDtypeStruct(n0.shape, F32),
                   jax.ShapeDtypeStruct(m0.shape, F32)),
        grid=(b, nh, t // chunk),
        in_specs=[
            smem, smem,
            pl.BlockSpec((None, chunk, dk), lambda bi, hi, ci: (bi, ci, hi)),
            pl.BlockSpec((None, chunk, dk), lambda bi, hi, ci: (bi, ci, k_blk0 + hi)),
            pl.BlockSpec((None, chunk, dv), lambda bi, hi, ci: (bi, ci, v_blk0 + hi)),
            pl.BlockSpec((None, chunk, dv), lambda bi, hi, ci: (bi, ci, o_blk0 + hi)),
            pl.BlockSpec((None, None, 2, chunk), lambda bi, hi, ci: (bi, hi, 0, ci)),
            pl.BlockSpec((None, 1, dv), lambda bi, hi, ci: (hi, 0, 0)),
            state(dk, dv), state(1, dk), state(1, LANES),
        ],
        out_specs=(pl.BlockSpec((None, chunk, dv), lambda bi, hi, ci: (bi, ci, hi)),
                   state(dk, dv), state(1, dk), state(1, LANES)),
        compiler_params=_params(3),
        name="mlstm",
    )(b_i, b_f, proj, proj, proj, proj, gates_row, g_mh.reshape(nh, 1, dv), c0, n0, m0)


def _matmul_resid_body(a_ref, w_ref, r_ref, o_ref):
    o_ref[...] = r_ref[...] + jnp.dot(a_ref[...], w_ref[...], preferred_element_type=F32)


def _matmul_resid(a, w, resid, layer, *, tn=512):
    m, k = a.shape
    n = w.shape[2]
    tm = _row_tile(m)
    return pl.pallas_call(
        _matmul_resid_body,
        out_shape=jax.ShapeDtypeStruct((m, n), F32),
        grid=(m // tm, n // tn),
        in_specs=[pl.BlockSpec((tm, k), lambda i, j: (i, 0)),
                  pl.BlockSpec((None, k, tn), lambda i, j: (layer, 0, j)),
                  pl.BlockSpec((tm, tn), lambda i, j: (i, j))],
        out_specs=pl.BlockSpec((tm, tn), lambda i, j: (i, j)),
        compiler_params=_params(2),
        name="matmul_resid",
    )(a, w, resid)


def _rope_table_body(invf_ref, cos_ref, sin_ref, *, seq, base):
    rows = cos_ref.shape[0]
    m = lax.broadcasted_iota(jnp.int32, (rows, LANES), 0) + pl.program_id(0) * rows
    pos = (base + (m & (seq - 1))).astype(F32)
    ang = pos * invf_ref[...]
    cos_ref[...] = jnp.cos(ang)
    sin_ref[...] = jnp.sin(ang)


def _rope_tables(m, seq, base):
    assert seq & (seq - 1) == 0
    half = ROT_DIM // 2
    inv_freq = ROPE_THETA ** (-jnp.arange(half, dtype=F32) / half)
    invf = jnp.zeros((1, LANES), F32).at[0, :ROT_DIM].set(jnp.tile(inv_freq, 2))
    rows = min(m, 512)
    spec = pl.BlockSpec((rows, LANES), lambda i: (i, 0))
    return pl.pallas_call(
        functools.partial(_rope_table_body, seq=seq, base=base),
        out_shape=(jax.ShapeDtypeStruct((m, LANES), F32),) * 2,
        grid=(m // rows,),
        in_specs=[pl.BlockSpec((1, LANES), lambda i: (0, 0))],
        out_specs=(spec, spec),
        compiler_params=_params(1),
        name="rope_tables",
    )(invf)


def _to_head_slabs(res_ref, x):
    for hh in range(res_ref.shape[0]):
        res_ref[hh] = x[:, hh * ATT_HEAD_DIM:(hh + 1) * ATT_HEAD_DIM]


def _norm_rope_slabs(res_ref, gain_ref, cos_ref, sin_ref):
    n_heads, rows, _ = res_ref.shape
    step = min(rows, EPI_ROWS)

    def norm_body(r, carry):
        rs = pl.ds(pl.multiple_of(r * step, step), step)
        for hh in range(n_heads):
            res_ref[hh, rs, :] = _head_norm(res_ref[hh, rs, :], gain_ref[...])
        return carry

    def rope_body(r, carry):
        rs = pl.ds(pl.multiple_of(r * step, step), step)
        cosf = cos_ref[rs, :]
        sinf = sin_ref[rs, :]
        for hh in range(n_heads):
            res_ref[hh, rs, :] = _head_rope(res_ref[hh, rs, :], cosf, sinf)
        return carry

    lax.fori_loop(0, rows // step, norm_body, 0)
    lax.fori_loop(0, rows // step, rope_body, 0)


def _emit_slabs(res_ref, out_ref, dil):
    n_heads, rows, _ = res_ref.shape
    n = n_heads * ATT_HEAD_DIM
    for r in range(dil):
        for hh in range(n_heads):
            col = r * n + hh * ATT_HEAD_DIM
            src = res_ref[hh] if dil == 1 else res_ref[hh, pl.ds(r, rows // dil, stride=dil), :]
            out_ref[:, col:col + ATT_HEAD_DIM] = src.astype(out_ref.dtype)


def _class_spec(tm, n, seq, dil):
    tiles = seq // tm
    assert seq % tm == 0 and tm % (dil * BF16_SUBLANES) == 0
    return pl.BlockSpec((None, tm // dil, dil * n), lambda i, *_: (i // tiles, i % tiles, 0))


def _kv_proj_body(*refs, tm, dilations):
    n_cls = len(dilations)
    x_hbm, gain_ref, w_ref, kgain_ref, cos_ref, sin_ref, k_ref, v_ref = refs[:8]
    kc_refs = refs[8:8 + n_cls]
    vc_refs = refs[8 + n_cls:8 + 2 * n_cls]
    xbuf, xn_ref, res_ref, sem = refs[8 + 2 * n_cls:]
    i = pl.program_id(0)
    j = pl.program_id(1)

    @pl.when(j == 0)
    def _k_half():
        _load_norm_rows(x_hbm, xbuf, gain_ref, xn_ref, sem, pl.multiple_of(i * tm, tm), tm)
        _to_head_slabs(res_ref, jnp.dot(xn_ref[...], w_ref[...], preferred_element_type=F32))
        _norm_rope_slabs(res_ref, kgain_ref, cos_ref, sin_ref)
        _emit_slabs(res_ref, k_ref, 1)
        for dil, ref in zip(dilations, kc_refs):
            _emit_slabs(res_ref, ref, dil)

    @pl.when(j == 1)
    def _v_half():
        v = jnp.dot(xn_ref[...], w_ref[...], preferred_element_type=F32)
        v_ref[...] = v
        if dilations:
            _to_head_slabs(res_ref, v)
            for dil, ref in zip(dilations, vc_refs):
                _emit_slabs(res_ref, ref, dil)


def _kv_proj(x, gain, w_kv, g_knorm, cosf, sinf, seq, dilations=()):
    m, d = x.shape
    n = w_kv.shape[2] // 2
    tm = min(m, PROJ_ROW_TILE)
    out_spec = pl.BlockSpec((tm, n), lambda i, j: (i, 0))
    tab_spec = pl.BlockSpec((tm, LANES), lambda i, j: (i, 0))
    cls_specs = [_class_spec(tm, n, seq, dil) for dil in dilations]
    cls_shapes = [jax.ShapeDtypeStruct((m // seq, seq // dil, dil * n), BF16) for dil in dilations]
    outs = pl.pallas_call(
        functools.partial(_kv_proj_body, tm=tm, dilations=tuple(dilations)),
        out_shape=[jax.ShapeDtypeStruct((m, n), F32)] * 2 + cls_shapes * 2,
        grid=(m // tm, 2),
        in_specs=[pl.BlockSpec(memory_space=pl.ANY),
                  pl.BlockSpec((1, d), lambda i, j: (0, 0)),
                  pl.BlockSpec((None, d, n), lambda i, j: (0, 0, j)),
                  pl.BlockSpec((1, ATT_HEAD_DIM), lambda i, j: (0, 0)),
                  tab_spec, tab_spec],
        out_specs=[out_spec, out_spec] + cls_specs * 2,
        scratch_shapes=[pltpu.VMEM((tm, d), F32), pltpu.VMEM((tm, d), BF16),
                        pltpu.VMEM((n // ATT_HEAD_DIM, tm, ATT_HEAD_DIM), F32),
                        pltpu.SemaphoreType.DMA((LOAD_CHUNKS,))],
        compiler_params=_params(2),
        name="kv_proj",
    )(x, gain.reshape(1, d), w_kv, g_knorm.reshape(1, ATT_HEAD_DIM), cosf, sinf)
    n_cls = len(dilations)
    return outs[0], outs[1], tuple(outs[2:2 + n_cls]), tuple(outs[2 + n_cls:])


def _q_proj_body(*refs, tm, dilations):
    x_hbm, gain_ref, w_ref, qgain_ref, cos_ref, sin_ref = refs[:6]
    n_out = max(len(dilations), 1)
    q_refs = refs[6:6 + n_out]
    xbuf, xn_ref, res_ref, sem = refs[6 + n_out:]
    i = pl.program_id(0)
    j = pl.program_id(1)

    @pl.when(j == 0)
    def _load_and_norm():
        _load_norm_rows(x_hbm, xbuf, gain_ref, xn_ref, sem, pl.multiple_of(i * tm, tm), tm)

    _to_head_slabs(res_ref, jnp.dot(xn_ref[...], w_ref[...], preferred_element_type=F32))
    _norm_rope_slabs(res_ref, qgain_ref, cos_ref, sin_ref)
    if not dilations:
        _emit_slabs(res_ref, q_refs[0], 1)
    for g, dil in enumerate(dilations):
        @pl.when(j == g)
        def _emit_group(g=g, dil=dil):
            _emit_slabs(res_ref, q_refs[g], dil)


def _q_proj(x, gain, w_q, g_qnorm, cosf, sinf, layer_mix, layer_b, seq, dilations=()):
    m, d = x.shape
    n_groups = g_qnorm.shape[1]
    tn = w_q.shape[2] // n_groups
    tm = min(m, PROJ_ROW_TILE)
    tab_spec = pl.BlockSpec((tm, LANES), lambda i, j: (i, 0))
    if dilations:
        assert len(dilations) == n_groups
        out_shape = [jax.ShapeDtypeStruct((m // seq, seq // dil, dil * tn), BF16) for dil in dilations]
        out_specs = [_class_spec(tm, tn, seq, dil) for dil in dilations]
    else:
        out_shape = [jax.ShapeDtypeStruct((m, n_groups * tn), BF16)]
        out_specs = [pl.BlockSpec((tm, tn), lambda i, j: (i, j))]
    return pl.pallas_call(
        functools.partial(_q_proj_body, tm=tm, dilations=tuple(dilations)),
        out_shape=out_shape,
        grid=(m // tm, n_groups),
        in_specs=[pl.BlockSpec(memory_space=pl.ANY),
                  pl.BlockSpec((None, 1, d), lambda i, j: (layer_mix, 0, 0)),
                  pl.BlockSpec((None, d, tn), lambda i, j: (layer_b, 0, j)),
                  pl.BlockSpec((None, None, 1, ATT_HEAD_DIM), lambda i, j: (layer_b, j, 0, 0)),
                  tab_spec, tab_spec],
        out_specs=out_specs,
        scratch_shapes=[pltpu.VMEM((tm, d), F32), pltpu.VMEM((tm, d), BF16),
                        pltpu.VMEM((tn // ATT_HEAD_DIM, tm, ATT_HEAD_DIM), F32),
                        pltpu.SemaphoreType.DMA((LOAD_CHUNKS,))],
        compiler_params=_params(2),
        name="q_proj",
    )(x, gain.reshape(gain.shape[0], 1, d), w_q,
      g_qnorm.reshape(g_qnorm.shape[0], n_groups, 1, ATT_HEAD_DIM), cosf, sinf)


def _softmax_pieces(scores, masks):
    masked = [jnp.where(mk, sc, -jnp.inf) for sc, mk in zip(scores, masks)]
    m = functools.reduce(jnp.maximum, [jnp.max(sc, axis=1, keepdims=True) for sc in masked])
    probs = [jnp.exp(sc - m) for sc in masked]
    l = functools.reduce(jnp.add, [jnp.sum(p, axis=1, keepdims=True) for p in probs])
    return probs, l, m


def _qk(qh, kh):
    return lax.dot_general(qh, kh, (((1,), (1,)), ((), ())), preferred_element_type=F32)


def _attn_prompt_body(*refs, tq, win, dil, has_prev, n_heads):
    if has_prev:
        q_ref, kc_ref, vc_ref, kp_ref, vp_ref, o_ref, lse_ref = refs
    else:
        q_ref, kc_ref, vc_ref, o_ref, lse_ref = refs
    qi = pl.program_id(1)
    res = pl.program_id(2)
    scale = ATT_HEAD_DIM ** -0.5
    row = lax.broadcasted_iota(jnp.int32, (tq, tq), 0)
    col = lax.broadcasted_iota(jnp.int32, (tq, tq), 1)
    mask_c = (col <= row) & (row - col <= win)
    if has_prev:
        rowp = lax.broadcasted_iota(jnp.int32, (tq, win), 0)
        colp = lax.broadcasted_iota(jnp.int32, (tq, win), 1)
        mask_p = (colp >= rowp) & (qi > 0)
    rows = slice(None) if dil == 1 else pl.ds(res, tq, stride=dil)
    for hh in range(n_heads):
        hs = slice(hh * ATT_HEAD_DIM, (hh + 1) * ATT_HEAD_DIM)
        qh = q_ref[:, hs]
        scores = [_qk(qh, _as_bf16(kc_ref[:, hs])) * scale]
        masks = [mask_c]
        values = [_as_bf16(vc_ref[:, hs])]
        if has_prev:
            scores.append(_qk(qh, _as_bf16(kp_ref[:, hs])) * scale)
            masks.append(mask_p)
            values.append(_as_bf16(vp_ref[:, hs]))
        probs, l, m = _softmax_pieces(scores, masks)
        acc = functools.reduce(jnp.add, [jnp.dot(p.astype(BF16), v, preferred_element_type=F32)
                                         for p, v in zip(probs, values)])
        o_ref[hh, rows, :] = acc / l
        lse_ref[hh, rows, :] = jnp.broadcast_to(m + jnp.log(l), (tq, ATT_HEAD_DIM))


def _attn_prompt_group(q, k, v, seq, window, dil):
    b = q.shape[0]
    n = q.shape[2] // dil
    n_heads = n // ATT_HEAD_DIM
    tc = seq // dil
    win = window // dil
    tq = min(tc, ATTN_Q_TILE)
    has_prev = tc > tq
    assert tq % win == 0 and tc % tq == 0
    cur = pl.BlockSpec((None, tq, n), lambda bi, qi, r: (bi, qi, r))
    prev = pl.BlockSpec((None, win, n), lambda bi, qi, r: (bi, jnp.maximum(qi * (tq // win) - 1, 0), r))
    out = pl.BlockSpec((None, n_heads, tq * dil, ATT_HEAD_DIM), lambda bi, qi, r: (bi, 0, qi, 0))
    return pl.pallas_call(
        functools.partial(_attn_prompt_body, tq=tq, win=win, dil=dil, has_prev=has_prev, n_heads=n_heads),
        out_shape=(jax.ShapeDtypeStruct((b, n_heads, seq, ATT_HEAD_DIM), F32),) * 2,
        grid=(b, tc // tq, dil),
        in_specs=[cur, cur, cur] + ([prev, prev] if has_prev else []),
        out_specs=(out, out),
        compiler_params=_params(3),
        name=f"attn_prompt_d{dil}",
    )(q, k, v, *([k, v] if has_prev else []))


def _attn_out_body(*refs, n_groups):
    o_refs = refs[:n_groups]
    l_refs = refs[n_groups:2 * n_groups]
    w_ref, r_ref, out_ref, att_ref = refs[2 * n_groups:]

    @pl.when(pl.program_id(1) == 0)
    def _mix():
        for hh in range(o_refs[0].shape[0]):
            lses = [l[hh] for l in l_refs]
            m = functools.reduce(jnp.maximum, lses)
            es = [jnp.exp(l - m) for l in lses]
            tot = functools.reduce(jnp.add, es)
            att = functools.reduce(jnp.add, [(e / tot) * o[hh] for e, o in zip(es, o_refs)])
            att_ref[:, hh * ATT_HEAD_DIM:(hh + 1) * ATT_HEAD_DIM] = att.astype(BF16)

    out_ref[...] = r_ref[...] + jnp.dot(att_ref[...], w_ref[...], preferred_element_type=F32)


def _attn_out(outs, lses, w_o, resid, layer, *, tn=512, tm=512):
    b, n_heads, seq, hd = outs[0].shape
    m, n = resid.shape
    k = n_heads * hd
    tm = min(seq, tm)
    tiles = seq // tm
    blk = pl.BlockSpec((None, n_heads, tm, hd), lambda i, j: (i // tiles, 0, i % tiles, 0))
    return pl.pallas_call(
        functools.partial(_attn_out_body, n_groups=len(outs)),
        out_shape=jax.ShapeDtypeStruct((m, n), F32),
        grid=(m // tm, n // tn),
        in_specs=[blk] * (2 * len(outs)) + [pl.BlockSpec((None, k, tn), lambda i, j: (layer, 0, j)),
                                            pl.BlockSpec((tm, tn), lambda i, j: (i, j))],
        out_specs=pl.BlockSpec((tm, tn), lambda i, j: (i, j)),
        scratch_shapes=[pltpu.VMEM((tm, k), BF16)],
        compiler_params=_params(2),
        name="attn_out",
    )(*outs, *lses, w_o, resid)


def _attn_sample_body(q_ref, kc_ref, vc_ref, kn_ref, vn_ref, o_ref, *, groups, n_heads, n_new):
    rows = q_ref.shape[0]
    cache = kc_ref.shape[0]
    n = n_heads * ATT_HEAD_DIM
    scale = ATT_HEAD_DIM ** -0.5
    dist_c = cache + lax.broadcasted_iota(jnp.int32, (rows, cache), 0) \
        - lax.broadcasted_iota(jnp.int32, (rows, cache), 1)
    dist_n = lax.broadcasted_iota(jnp.int32, (rows, rows), 0) - lax.broadcasted_iota(jnp.int32, (rows, rows), 1)
    col_n = lax.broadcasted_iota(jnp.int32, (rows, rows), 1)
    for hh in range(n_heads):
        hs = slice(hh * ATT_HEAD_DIM, (hh + 1) * ATT_HEAD_DIM)
        kc = kc_ref[:, hs].astype(BF16)
        vc = vc_ref[:, hs].astype(BF16)
        kn = kn_ref[:, hs].astype(BF16)
        vn = vn_ref[:, hs].astype(BF16)
        outs, lses = [], []
        for g, (window, dil) in enumerate(groups):
            qh = q_ref[:, g * n + hh * ATT_HEAD_DIM:g * n + (hh + 1) * ATT_HEAD_DIM]
            mask_c = (dist_c <= window) & ((dist_c & (dil - 1)) == 0)
            mask_n = (dist_n >= 0) & (col_n < n_new) & ((dist_n & (dil - 1)) == 0)
            probs, l, m = _softmax_pieces([_qk(qh, kc) * scale, _qk(qh, kn) * scale], [mask_c, mask_n])
            acc = jnp.dot(probs[0].astype(BF16), vc, preferred_element_type=F32) \
                + jnp.dot(probs[1].astype(BF16), vn, preferred_element_type=F32)
            outs.append(acc / l)
            lses.append(m + jnp.log(l))
        m_all = functools.reduce(jnp.maximum, lses)
        es = [jnp.exp(l - m_all) for l in lses]
        tot = functools.reduce(jnp.add, es)
        o_ref[:, hs] = functools.reduce(jnp.add, [(e / tot) * o for e, o in zip(es, outs)]).astype(o_ref.dtype)


def _attn_sample(q, k_new, v_new, cache_k, cache_v, n_new):
    b, rows, n = k_new.shape
    cache = cache_k.shape[1]
    assert cache == MAX_WINDOW and all(d & (d - 1) == 0 for _, d in DILATED_GROUPS)
    row_blk = lambda width: pl.BlockSpec((None, rows, width), lambda bi: (bi, 0, 0))
    cache_blk = pl.BlockSpec((None, cache, n), lambda bi: (bi, 0, 0))
    return pl.pallas_call(
        functools.partial(_attn_sample_body, groups=DILATED_GROUPS, n_heads=n // ATT_HEAD_DIM, n_new=n_new),
        out_shape=jax.ShapeDtypeStruct((b, rows, n), BF16),
        grid=(b,),
        in_specs=[row_blk(q.shape[2]), cache_blk, cache_blk, row_blk(n), row_blk(n)],
        out_specs=row_blk(n),
        compiler_params=_params(1),
        name="attn_sample",
    )(q, cache_k, cache_v, k_new, v_new)


FFN_ORDER = (("ffn1", 0), ("ffn2", 0), ("ffn1", 1), ("ffn2", 1))
SMALL_WEIGHTS = ("w_in_a", "w_out_a", "w_kv", "w_q_b", "w_o_b")


def _ffn_stacks(p, step):
    name, _ = FFN_ORDER[step]
    return tuple(p[f"w_{name}_{part}"] for part in ("gate", "up", "down"))


def _run_ffn(x, p, bf16, step):
    name, layer = FFN_ORDER[step]
    if ("ffn", step) not in bf16:
        x, bf16[("ffn", step)], _ = _ffn(x, p[f"norm_{name}"], layer, _ffn_stacks(p, step), layer, emit=True)
        return x
    casts, keys = [], []
    if step + 1 < len(FFN_ORDER) and ("ffn", step + 1) not in bf16:
        gate, up, down = _ffn_stacks(p, step + 1)
        nxt_layer = FFN_ORDER[step + 1][1]
        casts += [(gate, nxt_layer, False), (up, nxt_layer, False), (down, nxt_layer, True)]
        keys += [("ffn", step + 1)] * 3
    for key in SMALL_WEIGHTS:
        if key not in bf16:
            assert p[key].shape[0] == 1
            casts.append((p[key], 0, False))
            keys.append(key)
    x, _, cast = _ffn(x, p[f"norm_{name}"], layer, bf16[("ffn", step)], casts=tuple(casts))
    for key, arr in zip(keys, cast):
        if key[0] == "ffn":
            bf16[key] = bf16.get(key, ()) + (arr,)
        else:
            bf16[key] = arr
    return x


def _trunk(x, seq, base, cache_k, cache_v, c0, n0, m0, p, bf16, first_ffn_done=False):
    m_rows, d = x.shape
    b = m_rows // seq
    nh = MLSTM_HEADS
    dk, dv = c0.shape[2], c0.shape[3]
    fresh = cache_k is None
    dilations = tuple(dil for _, dil in DILATED_GROUPS)

    if not first_ffn_done:
        x = _run_ffn(x, p, bf16, 0)
    proj, gates = _in_proj(x, p["norm_mix"], bf16["w_in_a"], 2 * nh * dk + nh * dv + d, 0)
    if seq % MLSTM_PROMPT_CHUNK == 0:
        chunk, valid, t_pad = MLSTM_PROMPT_CHUNK, MLSTM_PROMPT_CHUNK, seq
    else:
        assert seq <= BF16_SUBLANES
        chunk, valid, t_pad = BF16_SUBLANES, seq, BF16_SUBLANES
    proj3 = proj.reshape(b, seq, -1)
    gates3 = gates.reshape(b, seq, LANES)[:, :, :2 * nh]
    if t_pad != seq:
        proj3 = jnp.pad(proj3, ((0, 0), (0, t_pad - seq), (0, 0)))
        gates3 = jnp.pad(gates3, ((0, 0), (0, t_pad - seq), (0, 0)))
    gates_row = gates3.reshape(b, t_pad, 2, nh).transpose(0, 3, 2, 1)
    y, c_fin, n_fin, m_fin = _mlstm(proj3, gates_row, p["b_ig"][0], p["b_fg"][0], p["g_mh"][0],
                                    c0, n0, m0, chunk=chunk, valid=valid)
    y = y[:, :seq].reshape(m_rows, nh * dv)
    x = _matmul_resid(y, bf16["w_out_a"], x, 0)
    x = _run_ffn(x, p, bf16, 1)

    cosf, sinf = _rope_tables(m_rows, seq, base)
    cls_dils = tuple(dil for dil in dilations if dil > 1) if fresh else ()
    k, v, k_cls, v_cls = _kv_proj(x, p["norm_kv"], bf16["w_kv"], p["g_knorm"], cosf, sinf, seq, cls_dils)

    x = _run_ffn(x, p, bf16, 2)
    q = _q_proj(x, p["norm_mix"], bf16["w_q_b"], p["g_qnorm"], cosf, sinf, 1, 0, seq, dilations if fresh else ())
    n_kv = k.shape[1]
    if fresh:
        k_by_dil = {1: k.reshape(b, seq, n_kv), **dict(zip(cls_dils, k_cls))}
        v_by_dil = {1: v.reshape(b, seq, n_kv), **dict(zip(cls_dils, v_cls))}
        outs, lses = [], []
        for q_g, (window, dil) in zip(q, DILATED_GROUPS):
            o_g, l_g = _attn_prompt_group(q_g, k_by_dil[dil], v_by_dil[dil], seq, window, dil)
            outs.append(o_g)
            lses.append(l_g)
        x = _attn_out(outs, lses, bf16["w_o_b"], x, 0)
    else:
        pad = ((0, 0), (0, BF16_SUBLANES - seq), (0, 0))
        att = _attn_sample(jnp.pad(q[0].reshape(b, seq, -1), pad), jnp.pad(k.reshape(b, seq, n_kv), pad),
                           jnp.pad(v.reshape(b, seq, n_kv), pad), cache_k, cache_v, seq)
        x = _matmul_resid(att[:, :seq].reshape(m_rows, n_kv), bf16["w_o_b"], x, 0)
    x = _run_ffn(x, p, bf16, 3)
    return x, k, v, c_fin, n_fin, m_fin


def kernel(x_prompt, x_sample, cache_k, cache_v, state_C, state_n, state_m, norm_ffn1, w_ffn1_gate, w_ffn1_up, w_ffn1_down, norm_mix, norm_ffn2, w_ffn2_gate, w_ffn2_up, w_ffn2_down, w_in_a, b_ig, b_fg, g_mh, w_out_a, norm_kv, w_kv, g_knorm, w_q_b, g_qnorm, w_o_b):
    bp, seq_p, d = x_prompt.shape
    bs, seq_s, _ = x_sample.shape
    past = cache_k.shape[1]
    assert past == min(MAX_WINDOW, PAST_LEN)
    nh = MLSTM_HEADS
    dk, dv = state_C.shape[3], state_C.shape[4]
    n_kv = cache_k.shape[2] * cache_k.shape[3]

    p = dict(norm_ffn1=norm_ffn1, norm_mix=norm_mix, norm_ffn2=norm_ffn2, norm_kv=norm_kv,
             w_ffn1_gate=w_ffn1_gate, w_ffn1_up=w_ffn1_up, w_ffn1_down=w_ffn1_down,
             w_ffn2_gate=w_ffn2_gate, w_ffn2_up=w_ffn2_up, w_ffn2_down=w_ffn2_down,
             w_in_a=w_in_a, b_ig=b_ig, b_fg=b_fg, g_mh=g_mh, w_out_a=w_out_a, w_kv=w_kv[None],
             g_knorm=g_knorm, w_q_b=w_q_b, g_qnorm=g_qnorm, w_o_b=w_o_b)
    bf16 = {}
    xs = _run_ffn(x_sample.reshape(bs * seq_s, d), p, bf16, 0)

    zeros = lambda *s: jnp.zeros(s, F32)
    y_p, k_p, v_p, c_p, n_p, m_p = _trunk(
        x_prompt.reshape(bp * seq_p, d), seq_p, 0, None, None,
        zeros(bp, nh, dk, dv), zeros(bp, nh, 1, dk), zeros(bp, nh, 1, LANES), p, bf16)
    y_s, k_s, v_s, c_s, n_s, m_s = _trunk(
        xs, seq_s, PAST_LEN, cache_k.reshape(bs, past, n_kv), cache_v.reshape(bs, past, n_kv),
        state_C[:, 0], state_n[:, 0][:, :, None, :],
        jnp.broadcast_to(state_m[:, 0][:, :, None, None], (bs, nh, 1, LANES)), p, bf16, first_ffn_done=True)

    hkv, hd = cache_k.shape[2], cache_k.shape[3]
    keep = min(MAX_WINDOW, seq_p)
    return (y_p.reshape(bp, seq_p, d), y_s.reshape(bs, seq_s, d),
            k_p.reshape(bp, seq_p, hkv, hd)[:, seq_p - keep:], v_p.reshape(bp, seq_p, hkv, hd)[:, seq_p - keep:],
            k_s.reshape(bs, seq_s, hkv, hd), v_s.reshape(bs, seq_s, hkv, hd),
            c_p[:, None], n_p[:, None, :, 0], m_p[:, None, :, 0, 0],
            c_s[:, None], n_s[:, None, :, 0], m_s[:, None, :, 0, 0])
```

```python
import functools

import jax
import jax.numpy as jnp
from jax import lax
from jax.experimental import pallas as pl
from jax.experimental.pallas import tpu as pltpu

F32 = jnp.float32
BF16 = jnp.bfloat16

EPS = 1e-6
MLSTM_HEADS = 8
ATT_HEAD_DIM = 128
DILATED_GROUPS = ((128, 1), (512, 4), (2048, 16))
MAX_WINDOW = max(w for w, _ in DILATED_GROUPS)
ROT_DIM = ATT_HEAD_DIM // 4
ROPE_THETA = 500000.0
PAST_LEN = 16384

V7X_VMEM_LIMIT_BYTES = 56 * 1024 * 1024
LANES = 128
BF16_SUBLANES = 16
NORM_ROWS = 16
NORM_UNROLL = 4
LOAD_CHUNKS = 8
EPI_ROWS = 64
PROMPT_ROW_TILE = 1024
PROJ_ROW_TILE = 512
MLSTM_PROMPT_CHUNK = 256
MLSTM_HEADS_PER_STEP = 4
ATTN_Q_TILE = 256


def _params(n_axes):
    return pltpu.CompilerParams(dimension_semantics=("arbitrary",) * n_axes,
                                vmem_limit_bytes=V7X_VMEM_LIMIT_BYTES)


def _row_tile(m):
    return min(m, PROMPT_ROW_TILE)


def _round_up(x, mult):
    return -(-x // mult) * mult


def _rmsnorm_rows(src_ref, gain_ref, dst_ref, rows):
    step = min(rows, NORM_ROWS)

    def body(r, carry):
        rs = pl.ds(pl.multiple_of(r * step, step), step)
        ms = jnp.mean(jnp.square(src_ref[rs, :]), axis=-1, keepdims=True)
        dst_ref[rs, :] = (src_ref[rs, :] * lax.rsqrt(ms + EPS) * gain_ref[...]).astype(BF16)
        return carry

    trips = rows // step
    lax.fori_loop(0, trips, body, 0, unroll=min(trips, NORM_UNROLL))


def _load_norm_rows(x_hbm, buf_ref, gain_ref, xn_ref, sems, row0, rows):
    n = LOAD_CHUNKS if rows % (LOAD_CHUNKS * NORM_ROWS) == 0 else 1
    piece = rows // n
    copies = [pltpu.make_async_copy(x_hbm.at[pl.ds(row0 + c * piece, piece)],
                                    buf_ref.at[pl.ds(c * piece, piece)], sems.at[c]) for c in range(n)]
    for cp in copies:
        cp.start()
    for c, cp in enumerate(copies):
        cp.wait()
        _rmsnorm_rows(buf_ref.at[pl.ds(c * piece, piece)], gain_ref, xn_ref.at[pl.ds(c * piece, piece)], piece)


def _head_norm(xh, gain):
    return xh * lax.rsqrt(jnp.mean(xh * xh, axis=-1, keepdims=True) + EPS) * gain


def _head_rope(y, cosf, sinf):
    half = ROT_DIM // 2
    lane = lax.broadcasted_iota(jnp.int32, y.shape, 1)
    from_lower = pltpu.roll(y, half, axis=1)
    from_upper = pltpu.roll(y, LANES - half, axis=1)
    rot = jnp.where(lane < half, -from_upper, from_lower)
    return y * cosf + rot * sinf


def _as_bf16(x):
    return x if x.dtype == BF16 else x.astype(BF16)


def _dot_nt(a, b):
    return lax.dot_general(a, b, (((1,), (1,)), ((), ())), preferred_element_type=F32)


def _cast_specs(src, layer, grid, rows_follow_j):
    gi, gj = grid
    _, r, c = src.shape
    if rows_follow_j:
        rb, cb = _round_up(pl.cdiv(r, gj), BF16_SUBLANES), c // gi
        assert c % gi == 0 and cb % LANES == 0
        last = pl.cdiv(r, rb) - 1
        index = lambda lay: (lambda i, j: (lay, jnp.minimum(j, last), i))
    else:
        rb, cb = r // gi, _round_up(pl.cdiv(c, gj), LANES)
        assert r % gi == 0 and rb % BF16_SUBLANES == 0
        last = pl.cdiv(c, cb) - 1
        index = lambda lay: (lambda i, j: (lay, i, jnp.minimum(j, last)))
    return (pl.BlockSpec((None, rb, cb), index(layer)), pl.BlockSpec((None, rb, cb), index(0)),
            jax.ShapeDtypeStruct((1, r, c), BF16))


def _ffn_body(*refs, tm, n_chunk, n_casts, emit):
    x_hbm, gain_ref, wg_ref, wu_ref, wd_ref = refs[:5]
    cast_in = refs[5:5 + n_casts]
    o_hbm = refs[5 + n_casts]
    n_emit = 3 if emit else 0
    emit_refs = refs[6 + n_casts:6 + n_casts + n_emit]
    cast_out = refs[6 + n_casts + n_emit:6 + 2 * n_casts + n_emit]
    acc_ref, xn_ref, sem = refs[6 + 2 * n_casts + n_emit:]

    for src, dst in zip(cast_in, cast_out):
        dst[...] = src[...].astype(BF16)

    i = pl.program_id(0)
    j = pl.program_id(1)
    row0 = pl.multiple_of(i * tm, tm)

    @pl.when(j == 0)
    def _load_and_norm():
        _load_norm_rows(x_hbm, acc_ref, gain_ref, xn_ref, sem, row0, tm)

    wg, wu, wd = wg_ref[...], wu_ref[...], wd_ref
    if emit:
        wg, wu, wd = wg.astype(BF16), wu.astype(BF16), wd_ref[...].astype(BF16)
        emit_refs[0][...] = wg
        emit_refs[1][...] = wu
        emit_refs[2][...] = wd
    xn = xn_ref[...]
    g = jnp.dot(xn, wg, preferred_element_type=F32)
    u = jnp.dot(xn, wu, preferred_element_type=F32)
    a = (0.5 * (g * jax.nn.sigmoid(g)) * u).astype(BF16)
    d = acc_ref.shape[1]
    for c in range(d // n_chunk):
        cs = slice(c * n_chunk, (c + 1) * n_chunk)
        acc_ref[:, cs] += jnp.dot(a, wd[:, cs], preferred_element_type=F32)

    @pl.when(j == pl.num_programs(1) - 1)
    def _store():
        cp = pltpu.make_async_copy(acc_ref, o_hbm.at[pl.ds(row0, tm)], sem.at[LOAD_CHUNKS])
        cp.start()
        cp.wait()


def _ffn(x, gain, layer, weights, w_layer=0, casts=(), emit=False, *, tf=256):
    wg, wu, wd = weights
    m, d = x.shape
    f = wg.shape[2]
    tm = _row_tile(m)
    grid = (m // tm, f // tf)
    assert not emit or grid[0] == 1
    in_specs = [
        pl.BlockSpec(memory_space=pl.ANY),
        pl.BlockSpec((None, 1, d), lambda i, j: (layer, 0, 0)),
        pl.BlockSpec((None, d, tf), lambda i, j: (w_layer, 0, j)),
        pl.BlockSpec((None, d, tf), lambda i, j: (w_layer, 0, j)),
        pl.BlockSpec((None, tf, d), lambda i, j: (w_layer, j, 0)),
    ]
    out_shape = [jax.ShapeDtypeStruct((m, d), F32)]
    out_specs = [pl.BlockSpec(memory_space=pl.ANY)]
    if emit:
        out_shape += [jax.ShapeDtypeStruct((1, d, f), BF16)] * 2 + [jax.ShapeDtypeStruct((1, f, d), BF16)]
        out_specs += [pl.BlockSpec((None, d, tf), lambda i, j: (0, 0, j))] * 2 \
            + [pl.BlockSpec((None, tf, d), lambda i, j: (0, j, 0))]
    for src, src_layer, rows_follow_j in casts:
        in_spec, out_spec, shape = _cast_specs(src, src_layer, grid, rows_follow_j)
        in_specs.append(in_spec)
        out_specs.append(out_spec)
        out_shape.append(shape)
    outs = pl.pallas_call(
        functools.partial(_ffn_body, tm=tm, n_chunk=min(d, 512), n_casts=len(casts), emit=emit),
        out_shape=out_shape,
        grid=grid,
        in_specs=in_specs,
        out_specs=out_specs,
        scratch_shapes=[pltpu.VMEM((tm, d), F32), pltpu.VMEM((tm, d), BF16),
                        pltpu.SemaphoreType.DMA((LOAD_CHUNKS + 1,))],
        compiler_params=_params(2),
        name="ffn",
    )(x, gain.reshape(gain.shape[0], 1, d), wg, wu, wd, *[src for src, _, _ in casts])
    n_emit = 3 if emit else 0
    return outs[0], tuple(outs[1:1 + n_emit]), tuple(outs[1 + n_emit:])


def _in_proj_body(x_hbm, gain_ref, w_ref, wgate_ref, proj_ref, gates_ref, xbuf, xn_ref, sem, *, tm, n_gates):
    i = pl.program_id(0)
    j = pl.program_id(1)

    @pl.when(j == 0)
    def _load_and_norm():
        _load_norm_rows(x_hbm, xbuf, gain_ref, xn_ref, sem, pl.multiple_of(i * tm, tm), tm)
        row = lax.broadcasted_iota(jnp.int32, wgate_ref.shape, 0)
        wgate = jnp.where(row < n_gates, wgate_ref[...], jnp.zeros_like(wgate_ref))
        gates_ref[...] = _dot_nt(xn_ref[...], wgate)

    proj_ref[...] = _dot_nt(xn_ref[...], w_ref[...])


def _in_proj(x, gain, w_in_t, n_main, layer, *, tn=512):
    m, d = x.shape
    n_gates = w_in_t.shape[1] - n_main
    assert n_main % tn == 0 and n_main % LANES == 0 and 0 < n_gates < LANES
    tm = _row_tile(m)
    return pl.pallas_call(
        functools.partial(_in_proj_body, tm=tm, n_gates=n_gates),
        out_shape=(jax.ShapeDtypeStruct((m, n_main), F32), jax.ShapeDtypeStruct((m, LANES), F32)),
        grid=(m // tm, n_main // tn),
        in_specs=[
            pl.BlockSpec(memory_space=pl.ANY),
            pl.BlockSpec((None, 1, d), lambda i, j: (layer, 0, 0)),
            pl.BlockSpec((None, tn, d), lambda i, j: (layer, j, 0)),
            pl.BlockSpec((None, LANES, d), lambda i, j: (layer, n_main // LANES, 0)),
        ],
        out_specs=(pl.BlockSpec((tm, tn), lambda i, j: (i, j)),
                   pl.BlockSpec((tm, LANES), lambda i, j: (i, 0))),
        scratch_shapes=[pltpu.VMEM((tm, d), F32), pltpu.VMEM((tm, d), BF16),
                        pltpu.SemaphoreType.DMA((LOAD_CHUNKS,))],
        compiler_params=_params(2),
        name="in_proj",
    )(x, gain.reshape(gain.shape[0], 1, d), w_in_t, w_in_t)


def _mlstm_body(bi_ref, bf_ref, q_ref, k_ref, v_ref, o_ref, g_ref, gmh_ref, c0_ref, n0_ref, m0_ref,
                y_ref, c_ref, n_ref, m_ref, *, chunk, valid, heads):
    hg = pl.program_id(1)
    c = pl.program_id(2)
    L = chunk
    dk = q_ref.shape[1] // heads
    dv = v_ref.shape[1] // heads

    @pl.when(c == 0)
    def _init():
        c_ref[...] = c0_ref[...]
        n_ref[...] = n0_ref[...]
        m_ref[...] = m0_ref[...]

    t_idx = lax.broadcasted_iota(jnp.int32, (L, L), 0)
    s_idx = lax.broadcasted_iota(jnp.int32, (L, L), 1)
    causal = s_idx <= t_idx
    diag = s_idx == t_idx
    row = lax.broadcasted_iota(jnp.int32, (L, 1), 0)

    for e in range(heads):
        h = hg * heads + e
        qs = slice(e * dk, (e + 1) * dk)
        vs = slice(e * dv, (e + 1) * dv)
        gi_row = g_ref[e, 0:1, :] + bi_ref[h]
        lf_row = jax.nn.log_sigmoid(g_ref[e, 1:2, :] + bf_ref[h])
        lf_b = jnp.broadcast_to(lf_row, (L, L))
        gi_b = jnp.broadcast_to(gi_row, (L, L))
        lf_col = jnp.sum(jnp.where(diag, lf_b, 0.0), axis=1, keepdims=True)
        gi_col = jnp.sum(jnp.where(diag, gi_b, 0.0), axis=1, keepdims=True)
        b_col = jnp.sum(jnp.where(causal, lf_b, 0.0), axis=1, keepdims=True)
        b_row = jnp.sum(jnp.where(t_idx <= s_idx, jnp.broadcast_to(lf_col, (L, L)), 0.0),
                        axis=0, keepdims=True)

        log_d = jnp.where(causal, b_col + (gi_row - b_row), -jnp.inf)
        m_prev = m_ref[e, 0:1, 0:1]
        inter = b_col + m_prev
        m_t = jnp.maximum(inter, jnp.max(log_d, axis=1, keepdims=True))
        decay = jnp.exp(log_d - m_t)

        qf = q_ref[:, qs]
        kf = k_ref[:, qs] * (dk ** -0.5)
        qb = qf.astype(BF16)
        kb = kf.astype(BF16)
        vb = v_ref[:, vs].astype(BF16)
        s = _dot_nt(qb, kb) * decay
        sc = jnp.exp(inter - m_t)
        num = jnp.dot(s.astype(BF16), vb, preferred_element_type=F32) \
            + sc * jnp.dot(qb, c_ref[e].astype(BF16), preferred_element_type=F32)
        den = jnp.sum(s, axis=1, keepdims=True) + sc * jnp.sum(qf * n_ref[e], axis=1, keepdims=True)
        hc = num / jnp.maximum(jnp.abs(den), jnp.exp(-m_t))
        hn = hc * lax.rsqrt(jnp.mean(hc * hc, axis=1, keepdims=True) + EPS) * gmh_ref[e]
        y_ref[:, vs] = (jax.nn.sigmoid(o_ref[:, vs]) * hn).astype(y_ref.dtype)

        m_new = m_t[valid - 1:valid, :]
        b_last = b_col[valid - 1:valid, :]
        dec = jnp.exp(b_last + m_prev - m_new)
        ws = jnp.where(row < valid, jnp.exp(b_last - b_col + gi_col - m_new), 0.0)
        kw = ws * kf
        c_ref[e] = dec * c_ref[e] + lax.dot_general(kw.astype(BF16), vb, (((0,), (0,)), ((), ())),
                                                    preferred_element_type=F32)
        n_ref[e] = dec * n_ref[e] + jnp.sum(kw, axis=0, keepdims=True)
        m_ref[e] = jnp.broadcast_to(m_new, m_ref.shape[1:])


def _mlstm(proj, gates_row, b_i, b_f, g_mh, c0, n0, m0, *, chunk, valid):
    b, t, _ = proj.shape
    nh, dk, dv = c0.shape[1], c0.shape[2], c0.shape[3]
    hps = MLSTM_HEADS_PER_STEP
    assert nh % hps == 0 and (2 * nh * dk) % (hps * dv) == 0
    k_blk0 = nh // hps
    v_blk0 = 2 * nh * dk // (hps * dv)
    o_blk0 = v_blk0 + nh // hps
    smem = pl.BlockSpec(memory_space=pltpu.SMEM)
    state = lambda r, cdim: pl.BlockSpec((None, hps, r, cdim), lambda bi, hi, ci: (bi, hi, 0, 0))
    return pl.pallas_call(
        functools.partial(_mlstm_body, chunk=chunk, valid=valid, heads=hps),
        out_shape=(jax.ShapeDtypeStruct((b, t, nh * dv), BF16),
                   jax.ShapeDtypeStruct(c0.shape, F32),
                   jax.ShapeDtypeStruct(n0.shape, F32),
                   jax.ShapeDtypeStruct(m0.shape, F32)),
        grid=(b, nh // hps, t // chunk),
        in_specs=[
            smem, smem,
            pl.BlockSpec((None, chunk, hps * dk), lambda bi, hi, ci: (bi, ci, hi)),
            pl.BlockSpec((None, chunk, hps * dk), lambda bi, hi, ci: (bi, ci, k_blk0 + hi)),
            pl.BlockSpec((None, chunk, hps * dv), lambda bi, hi, ci: (bi, ci, v_blk0 + hi)),
            pl.BlockSpec((None, chunk, hps * dv), lambda bi, hi, ci: (bi, ci, o_blk0 + hi)),
            pl.BlockSpec((None, hps, 2, chunk), lambda bi, hi, ci: (bi, hi, 0, ci)),
            pl.BlockSpec((hps, 1, dv), lambda bi, hi, ci: (hi, 0, 0)),
            state(dk, dv), state(1, dk), state(1, LANES),
        ],
        out_specs=(pl.BlockSpec((None, chunk, hps * dv), lambda bi, hi, ci: (bi, ci, hi)),
                   state(dk, dv), state(1, dk), state(1, LANES)),
        compiler_params=_params(3),
        name="mlstm",
    )(b_i, b_f, proj, proj, proj, proj, gates_row, g_mh.reshape(nh, 1, dv), c0, n0, m0)


def _matmul_resid_body(a_ref, w_ref, r_ref, o_ref):
    o_ref[...] = r_ref[...] + jnp.dot(a_ref[...], w_ref[...], preferred_element_type=F32)


def _matmul_resid(a, w, resid, layer, *, tn=512):
    m, k = a.shape
    n = w.shape[2]
    tm = _row_tile(m)
    return pl.pallas_call(
        _matmul_resid_body,
        out_shape=jax.ShapeDtypeStruct((m, n), F32),
        grid=(m // tm, n // tn),
        in_specs=[pl.BlockSpec((tm, k), lambda i, j: (i, 0)),
                  pl.BlockSpec((None, k, tn), lambda i, j: (layer, 0, j)),
                  pl.BlockSpec((tm, tn), lambda i, j: (i, j))],
        out_specs=pl.BlockSpec((tm, tn), lambda i, j: (i, j)),
        compiler_params=_params(2),
        name="matmul_resid",
    )(a, w, resid)


def _rope_table_body(invf_ref, cos_ref, sin_ref, *, seq, base):
    rows = cos_ref.shape[0]
    m = lax.broadcasted_iota(jnp.int32, (rows, LANES), 0) + pl.program_id(0) * rows
    pos = (base + (m & (seq - 1))).astype(F32)
    ang = pos * invf_ref[...]
    cos_ref[...] = jnp.cos(ang)
    sin_ref[...] = jnp.sin(ang)


def _rope_tables(m, seq, base):
    assert seq & (seq - 1) == 0
    half = ROT_DIM // 2
    inv_freq = ROPE_THETA ** (-jnp.arange(half, dtype=F32) / half)
    invf = jnp.zeros((1, LANES), F32).at[0, :ROT_DIM].set(jnp.tile(inv_freq, 2))
    rows = min(m, 512)
    spec = pl.BlockSpec((rows, LANES), lambda i: (i, 0))
    return pl.pallas_call(
        functools.partial(_rope_table_body, seq=seq, base=base),
        out_shape=(jax.ShapeDtypeStruct((m, LANES), F32),) * 2,
        grid=(m // rows,),
        in_specs=[pl.BlockSpec((1, LANES), lambda i: (0, 0))],
        out_specs=(spec, spec),
        compiler_params=_params(1),
        name="rope_tables",
    )(invf)


def _to_head_slabs(res_ref, x):
    for hh in range(res_ref.shape[0]):
        res_ref[hh] = x[:, hh * ATT_HEAD_DIM:(hh + 1) * ATT_HEAD_DIM]


def _norm_rope_slabs(res_ref, gain_ref, cos_ref, sin_ref):
    n_heads, rows, _ = res_ref.shape
    step = min(rows, EPI_ROWS)

    def norm_body(r, carry):
        rs = pl.ds(pl.multiple_of(r * step, step), step)
        for hh in range(n_heads):
            res_ref[hh, rs, :] = _head_norm(res_ref[hh, rs, :], gain_ref[...])
        return carry

    def rope_body(r, carry):
        rs = pl.ds(pl.multiple_of(r * step, step), step)
        cosf = cos_ref[rs, :]
        sinf = sin_ref[rs, :]
        for hh in range(n_heads):
            res_ref[hh, rs, :] = _head_rope(res_ref[hh, rs, :], cosf, sinf)
        return carry

    lax.fori_loop(0, rows // step, norm_body, 0)
    lax.fori_loop(0, rows // step, rope_body, 0)


def _emit_slabs(res_ref, out_ref, dil):
    n_heads, rows, _ = res_ref.shape
    n = n_heads * ATT_HEAD_DIM
    for r in range(dil):
        for hh in range(n_heads):
            col = r * n + hh * ATT_HEAD_DIM
            src = res_ref[hh] if dil == 1 else res_ref[hh, pl.ds(r, rows // dil, stride=dil), :]
            out_ref[:, col:col + ATT_HEAD_DIM] = src.astype(out_ref.dtype)


def _emit_heads(res_ref, out_ref):
    for hh in range(res_ref.shape[0]):
        out_ref[:, hh, :] = res_ref[hh]


def _class_spec(tm, n, seq, dil):
    tiles = seq // tm
    assert seq % tm == 0 and tm % (dil * BF16_SUBLANES) == 0
    return pl.BlockSpec((None, tm // dil, dil * n), lambda i, *_: (i // tiles, i % tiles, 0))


def _kv_proj_body(*refs, tm, dilations):
    n_cls = len(dilations)
    x_hbm, gain_ref, w_ref, kgain_ref, cos_ref, sin_ref, k_ref, v_ref = refs[:8]
    kc_refs = refs[8:8 + n_cls]
    vc_refs = refs[8 + n_cls:8 + 2 * n_cls]
    xbuf, xn_ref, res_ref, sem = refs[8 + 2 * n_cls:]
    i = pl.program_id(0)
    j = pl.program_id(1)

    @pl.when(j == 0)
    def _k_half():
        _load_norm_rows(x_hbm, xbuf, gain_ref, xn_ref, sem, pl.multiple_of(i * tm, tm), tm)
        _to_head_slabs(res_ref, jnp.dot(xn_ref[...], w_ref[...], preferred_element_type=F32))
        _norm_rope_slabs(res_ref, kgain_ref, cos_ref, sin_ref)
        _emit_heads(res_ref, k_ref)
        for dil, ref in zip(dilations, kc_refs):
            _emit_slabs(res_ref, ref, dil)

    @pl.when(j == 1)
    def _v_half():
        _to_head_slabs(res_ref, jnp.dot(xn_ref[...], w_ref[...], preferred_element_type=F32))
        _emit_heads(res_ref, v_ref)
        for dil, ref in zip(dilations, vc_refs):
            _emit_slabs(res_ref, ref, dil)


def _kv_proj(x, gain, w_kv, g_knorm, cosf, sinf, seq, dilations=()):
    m, d = x.shape
    n = w_kv.shape[2] // 2
    n_heads = n // ATT_HEAD_DIM
    tm = min(m, PROJ_ROW_TILE)
    out_spec = pl.BlockSpec((tm, n_heads, ATT_HEAD_DIM), lambda i, j: (i, 0, 0))
    tab_spec = pl.BlockSpec((tm, LANES), lambda i, j: (i, 0))
    cls_specs = [_class_spec(tm, n, seq, dil) for dil in dilations]
    cls_shapes = [jax.ShapeDtypeStruct((m // seq, seq // dil, dil * n), BF16) for dil in dilations]
    outs = pl.pallas_call(
        functools.partial(_kv_proj_body, tm=tm, dilations=tuple(dilations)),
        out_shape=[jax.ShapeDtypeStruct((m, n_heads, ATT_HEAD_DIM), F32)] * 2 + cls_shapes * 2,
        grid=(m // tm, 2),
        in_specs=[pl.BlockSpec(memory_space=pl.ANY),
                  pl.BlockSpec((1, d), lambda i, j: (0, 0)),
                  pl.BlockSpec((None, d, n), lambda i, j: (0, 0, j)),
                  pl.BlockSpec((1, ATT_HEAD_DIM), lambda i, j: (0, 0)),
                  tab_spec, tab_spec],
        out_specs=[out_spec, out_spec] + cls_specs * 2,
        scratch_shapes=[pltpu.VMEM((tm, d), F32), pltpu.VMEM((tm, d), BF16),
                        pltpu.VMEM((n_heads, tm, ATT_HEAD_DIM), F32),
                        pltpu.SemaphoreType.DMA((LOAD_CHUNKS,))],
        compiler_params=_params(2),
        name="kv_proj",
    )(x, gain.reshape(1, d), w_kv, g_knorm.reshape(1, ATT_HEAD_DIM), cosf, sinf)
    n_cls = len(dilations)
    return outs[0], outs[1], tuple(outs[2:2 + n_cls]), tuple(outs[2 + n_cls:])


def _q_proj_body(*refs, tm, dilations):
    x_hbm, gain_ref, w_ref, qgain_ref, cos_ref, sin_ref = refs[:6]
    n_out = max(len(dilations), 1)
    q_refs = refs[6:6 + n_out]
    xbuf, xn_ref, res_ref, sem = refs[6 + n_out:]
    i = pl.program_id(0)
    j = pl.program_id(1)

    @pl.when(j == 0)
    def _load_and_norm():
        _load_norm_rows(x_hbm, xbuf, gain_ref, xn_ref, sem, pl.multiple_of(i * tm, tm), tm)

    _to_head_slabs(res_ref, jnp.dot(xn_ref[...], w_ref[...], preferred_element_type=F32))
    _norm_rope_slabs(res_ref, qgain_ref, cos_ref, sin_ref)
    if not dilations:
        _emit_slabs(res_ref, q_refs[0], 1)
    for g, dil in enumerate(dilations):
        @pl.when(j == g)
        def _emit_group(g=g, dil=dil):
            _emit_slabs(res_ref, q_refs[g], dil)


def _q_proj(x, gain, w_q, g_qnorm, cosf, sinf, layer_mix, layer_b, seq, dilations=()):
    m, d = x.shape
    n_groups = g_qnorm.shape[1]
    tn = w_q.shape[2] // n_groups
    tm = min(m, PROJ_ROW_TILE)
    tab_spec = pl.BlockSpec((tm, LANES), lambda i, j: (i, 0))
    if dilations:
        assert len(dilations) == n_groups
        out_shape = [jax.ShapeDtypeStruct((m // seq, seq // dil, dil * tn), BF16) for dil in dilations]
        out_specs = [_class_spec(tm, tn, seq, dil) for dil in dilations]
    else:
        out_shape = [jax.ShapeDtypeStruct((m, n_groups * tn), BF16)]
        out_specs = [pl.BlockSpec((tm, tn), lambda i, j: (i, j))]
    return pl.pallas_call(
        functools.partial(_q_proj_body, tm=tm, dilations=tuple(dilations)),
        out_shape=out_shape,
        grid=(m // tm, n_groups),
        in_specs=[pl.BlockSpec(memory_space=pl.ANY),
                  pl.BlockSpec((None, 1, d), lambda i, j: (layer_mix, 0, 0)),
                  pl.BlockSpec((None, d, tn), lambda i, j: (layer_b, 0, j)),
                  pl.BlockSpec((None, None, 1, ATT_HEAD_DIM), lambda i, j: (layer_b, j, 0, 0)),
                  tab_spec, tab_spec],
        out_specs=out_specs,
        scratch_shapes=[pltpu.VMEM((tm, d), F32), pltpu.VMEM((tm, d), BF16),
                        pltpu.VMEM((tn // ATT_HEAD_DIM, tm, ATT_HEAD_DIM), F32),
                        pltpu.SemaphoreType.DMA((LOAD_CHUNKS,))],
        compiler_params=_params(2),
        name="q_proj",
    )(x, gain.reshape(gain.shape[0], 1, d), w_q,
      g_qnorm.reshape(g_qnorm.shape[0], n_groups, 1, ATT_HEAD_DIM), cosf, sinf)


def _softmax_pieces(scores, masks):
    masked = [jnp.where(mk, sc, -jnp.inf) for sc, mk in zip(scores, masks)]
    m = functools.reduce(jnp.maximum, [jnp.max(sc, axis=1, keepdims=True) for sc in masked])
    probs = [jnp.exp(sc - m) for sc in masked]
    l = functools.reduce(jnp.add, [jnp.sum(p, axis=1, keepdims=True) for p in probs])
    return probs, l, m


def _attn_prompt_body(*refs, tq, win, dil, has_prev, n_heads):
    if has_prev:
        q_ref, kc_ref, vc_ref, kp_ref, vp_ref, o_ref, lse_ref = refs
    else:
        q_ref, kc_ref, vc_ref, o_ref, lse_ref = refs
    qi = pl.program_id(1)
    res = pl.program_id(2)
    scale = ATT_HEAD_DIM ** -0.5
    row = lax.broadcasted_iota(jnp.int32, (tq, tq), 0)
    col = lax.broadcasted_iota(jnp.int32, (tq, tq), 1)
    mask_c = (col <= row) & (row - col <= win)
    if has_prev:
        rowp = lax.broadcasted_iota(jnp.int32, (tq, win), 0)
        colp = lax.broadcasted_iota(jnp.int32, (tq, win), 1)
        mask_p = (colp >= rowp) & (qi > 0)
    rows = slice(None) if dil == 1 else pl.ds(res, tq, stride=dil)
    for hh in range(n_heads):
        hs = slice(hh * ATT_HEAD_DIM, (hh + 1) * ATT_HEAD_DIM)
        qh = q_ref[:, hs]
        scores = [_dot_nt(qh, _as_bf16(kc_ref[:, hs])) * scale]
        masks = [mask_c]
        values = [_as_bf16(vc_ref[:, hs])]
        if has_prev:
            scores.append(_dot_nt(qh, _as_bf16(kp_ref[:, hs])) * scale)
            masks.append(mask_p)
            values.append(_as_bf16(vp_ref[:, hs]))
        probs, l, m = _softmax_pieces(scores, masks)
        acc = functools.reduce(jnp.add, [jnp.dot(p.astype(BF16), v, preferred_element_type=F32)
                                         for p, v in zip(probs, values)])
        o_ref[hh, rows, :] = acc / l
        lse_ref[hh, rows, :] = jnp.broadcast_to(m + jnp.log(l), (tq, ATT_HEAD_DIM))


def _attn_prompt_group(q, k, v, seq, window, dil):
    b = q.shape[0]
    n = q.shape[2] // dil
    n_heads = n // ATT_HEAD_DIM
    tc = seq // dil
    win = window // dil
    tq = min(tc, ATTN_Q_TILE)
    has_prev = tc > tq
    assert tq % win == 0 and tc % tq == 0
    cur = pl.BlockSpec((None, tq, n), lambda bi, qi, r: (bi, qi, r))
    prev = pl.BlockSpec((None, win, n), lambda bi, qi, r: (bi, jnp.maximum(qi * (tq // win) - 1, 0), r))
    out = pl.BlockSpec((None, n_heads, tq * dil, ATT_HEAD_DIM), lambda bi, qi, r: (bi, 0, qi, 0))
    return pl.pallas_call(
        functools.partial(_attn_prompt_body, tq=tq, win=win, dil=dil, has_prev=has_prev, n_heads=n_heads),
        out_shape=(jax.ShapeDtypeStruct((b, n_heads, seq, ATT_HEAD_DIM), F32),) * 2,
        grid=(b, tc // tq, dil),
        in_specs=[cur, cur, cur] + ([prev, prev] if has_prev else []),
        out_specs=(out, out),
        compiler_params=_params(3),
        name=f"attn_prompt_d{dil}",
    )(q, k, v, *([k, v] if has_prev else []))


def _attn_out_body(*refs, n_groups):
    o_refs = refs[:n_groups]
    l_refs = refs[n_groups:2 * n_groups]
    w_ref, r_ref, out_ref, att_ref = refs[2 * n_groups:]

    @pl.when(pl.program_id(1) == 0)
    def _mix():
        for hh in range(o_refs[0].shape[0]):
            lses = [l[hh] for l in l_refs]
            m = functools.reduce(jnp.maximum, lses)
            es = [jnp.exp(l - m) for l in lses]
            tot = functools.reduce(jnp.add, es)
            att = functools.reduce(jnp.add, [(e / tot) * o[hh] for e, o in zip(es, o_refs)])
            att_ref[:, hh * ATT_HEAD_DIM:(hh + 1) * ATT_HEAD_DIM] = att.astype(BF16)

    out_ref[...] = r_ref[...] + jnp.dot(att_ref[...], w_ref[...], preferred_element_type=F32)


def _attn_out(outs, lses, w_o, resid, layer, *, tm=256):
    b, n_heads, seq, hd = outs[0].shape
    m, n = resid.shape
    k = n_heads * hd
    tn = n
    tm = min(seq, tm)
    tiles = seq // tm
    blk = pl.BlockSpec((None, n_heads, tm, hd), lambda i, j: (i // tiles, 0, i % tiles, 0))
    return pl.pallas_call(
        functools.partial(_attn_out_body, n_groups=len(outs)),
        out_shape=jax.ShapeDtypeStruct((m, n), F32),
        grid=(m // tm, n // tn),
        in_specs=[blk] * (2 * len(outs)) + [pl.BlockSpec((None, k, tn), lambda i, j: (layer, 0, j)),
                                            pl.BlockSpec((tm, tn), lambda i, j: (i, j))],
        out_specs=pl.BlockSpec((tm, tn), lambda i, j: (i, j)),
        scratch_shapes=[pltpu.VMEM((tm, k), BF16)],
        compiler_params=_params(2),
        name="attn_out",
    )(*outs, *lses, w_o, resid)


def _attn_sample_body(q_ref, kc_ref, vc_ref, kn_ref, vn_ref, o_ref, *, groups, n_heads, n_new):
    rows = q_ref.shape[0]
    cache = kc_ref.shape[0]
    n = n_heads * ATT_HEAD_DIM
    scale = ATT_HEAD_DIM ** -0.5
    dist_c = cache + lax.broadcasted_iota(jnp.int32, (rows, cache), 0) \
        - lax.broadcasted_iota(jnp.int32, (rows, cache), 1)
    dist_n = lax.broadcasted_iota(jnp.int32, (rows, rows), 0) - lax.broadcasted_iota(jnp.int32, (rows, rows), 1)
    col_n = lax.broadcasted_iota(jnp.int32, (rows, rows), 1)
    for hh in range(n_heads):
        hs = slice(hh * ATT_HEAD_DIM, (hh + 1) * ATT_HEAD_DIM)
        kc = kc_ref[:, hs].astype(BF16)
        vc = vc_ref[:, hs].astype(BF16)
        kn = kn_ref[:, hs].astype(BF16)
        vn = vn_ref[:, hs].astype(BF16)
        outs, lses = [], []
        for g, (window, dil) in enumerate(groups):
            qh = q_ref[:, g * n + hh * ATT_HEAD_DIM:g * n + (hh + 1) * ATT_HEAD_DIM]
            mask_c = (dist_c <= window) & ((dist_c & (dil - 1)) == 0)
            mask_n = (dist_n >= 0) & (col_n < n_new) & ((dist_n & (dil - 1)) == 0)
            probs, l, m = _softmax_pieces([_dot_nt(qh, kc) * scale, _dot_nt(qh, kn) * scale], [mask_c, mask_n])
            acc = jnp.dot(probs[0].astype(BF16), vc, preferred_element_type=F32) \
                + jnp.dot(probs[1].astype(BF16), vn, preferred_element_type=F32)
            outs.append(acc / l)
            lses.append(m + jnp.log(l))
        m_all = functools.reduce(jnp.maximum, lses)
        es = [jnp.exp(l - m_all) for l in lses]
        tot = functools.reduce(jnp.add, es)
        o_ref[:, hs] = functools.reduce(jnp.add, [(e / tot) * o for e, o in zip(es, outs)]).astype(o_ref.dtype)


def _attn_sample(q, k_new, v_new, cache_k, cache_v, n_new):
    b, rows, n = k_new.shape
    cache = cache_k.shape[1]
    assert cache == MAX_WINDOW and all(d & (d - 1) == 0 for _, d in DILATED_GROUPS)
    row_blk = lambda width: pl.BlockSpec((None, rows, width), lambda bi: (bi, 0, 0))
    cache_blk = pl.BlockSpec((None, cache, n), lambda bi: (bi, 0, 0))
    return pl.pallas_call(
        functools.partial(_attn_sample_body, groups=DILATED_GROUPS, n_heads=n // ATT_HEAD_DIM, n_new=n_new),
        out_shape=jax.ShapeDtypeStruct((b, rows, n), BF16),
        grid=(b,),
        in_specs=[row_blk(q.shape[2]), cache_blk, cache_blk, row_blk(n), row_blk(n)],
        out_specs=row_blk(n),
        compiler_params=_params(1),
        name="attn_sample",
    )(q, cache_k, cache_v, k_new, v_new)


FFN_ORDER = (("ffn1", 0), ("ffn2", 0), ("ffn1", 1), ("ffn2", 1))
SMALL_WEIGHTS = (("w_in_a_t", True), ("w_out_a", False), ("w_kv", False), ("w_q_b", False), ("w_o_b", False))


def _ffn_stacks(p, step):
    name, _ = FFN_ORDER[step]
    return tuple(p[f"w_{name}_{part}"] for part in ("gate", "up", "down"))


def _run_ffn(x, p, bf16, step):
    name, layer = FFN_ORDER[step]
    if ("ffn", step) not in bf16:
        x, bf16[("ffn", step)], _ = _ffn(x, p[f"norm_{name}"], layer, _ffn_stacks(p, step), layer, emit=True)
        return x
    casts, keys = [], []
    if step + 1 < len(FFN_ORDER) and ("ffn", step + 1) not in bf16:
        gate, up, down = _ffn_stacks(p, step + 1)
        nxt_layer = FFN_ORDER[step + 1][1]
        casts += [(gate, nxt_layer, False), (up, nxt_layer, False), (down, nxt_layer, True)]
        keys += [("ffn", step + 1)] * 3
    for key, rows_follow_j in SMALL_WEIGHTS:
        if key not in bf16:
            assert p[key].shape[0] == 1
            casts.append((p[key], 0, rows_follow_j))
            keys.append(key)
    x, _, cast = _ffn(x, p[f"norm_{name}"], layer, bf16[("ffn", step)], casts=tuple(casts))
    for key, arr in zip(keys, cast):
        if key[0] == "ffn":
            bf16[key] = bf16.get(key, ()) + (arr,)
        else:
            bf16[key] = arr
    return x


def _trunk(x, seq, base, cache_k, cache_v, c0, n0, m0, p, bf16, first_ffn_done=False):
    m_rows, d = x.shape
    b = m_rows // seq
    nh = MLSTM_HEADS
    dk, dv = c0.shape[2], c0.shape[3]
    fresh = cache_k is None
    dilations = tuple(dil for _, dil in DILATED_GROUPS)

    if not first_ffn_done:
        x = _run_ffn(x, p, bf16, 0)
    proj, gates = _in_proj(x, p["norm_mix"], bf16["w_in_a_t"], 2 * nh * dk + nh * dv + d, 0)
    if seq % MLSTM_PROMPT_CHUNK == 0:
        chunk, valid, t_pad = MLSTM_PROMPT_CHUNK, MLSTM_PROMPT_CHUNK, seq
    else:
        assert seq <= BF16_SUBLANES
        chunk, valid, t_pad = BF16_SUBLANES, seq, BF16_SUBLANES
    proj3 = proj.reshape(b, seq, -1)
    gates3 = gates.reshape(b, seq, LANES)[:, :, :2 * nh]
    if t_pad != seq:
        proj3 = jnp.pad(proj3, ((0, 0), (0, t_pad - seq), (0, 0)))
        gates3 = jnp.pad(gates3, ((0, 0), (0, t_pad - seq), (0, 0)))
    gates_row = gates3.reshape(b, t_pad, 2, nh).transpose(0, 3, 2, 1)
    y, c_fin, n_fin, m_fin = _mlstm(proj3, gates_row, p["b_ig"][0], p["b_fg"][0], p["g_mh"][0],
                                    c0, n0, m0, chunk=chunk, valid=valid)
    y = y[:, :seq].reshape(m_rows, nh * dv)
    x = _matmul_resid(y, bf16["w_out_a"], x, 0)
    x = _run_ffn(x, p, bf16, 1)

    cosf, sinf = _rope_tables(m_rows, seq, base)
    cls_dils = dilations if fresh else ()
    k, v, k_cls, v_cls = _kv_proj(x, p["norm_kv"], bf16["w_kv"], p["g_knorm"], cosf, sinf, seq, cls_dils)

    x = _run_ffn(x, p, bf16, 2)
    q = _q_proj(x, p["norm_mix"], bf16["w_q_b"], p["g_qnorm"], cosf, sinf, 1, 0, seq, dilations if fresh else ())
    n_kv = k.shape[1] * k.shape[2]
    if fresh:
        outs, lses = [], []
        for q_g, k_g, v_g, (window, dil) in zip(q, k_cls, v_cls, DILATED_GROUPS):
            o_g, l_g = _attn_prompt_group(q_g, k_g, v_g, seq, window, dil)
            outs.append(o_g)
            lses.append(l_g)
        x = _attn_out(outs, lses, bf16["w_o_b"], x, 0)
    else:
        pad = ((0, 0), (0, BF16_SUBLANES - seq), (0, 0))
        att = _attn_sample(jnp.pad(q[0].reshape(b, seq, -1), pad), jnp.pad(k.reshape(b, seq, n_kv), pad),
                           jnp.pad(v.reshape(b, seq, n_kv), pad), cache_k, cache_v, seq)
        x = _matmul_resid(att[:, :seq].reshape(m_rows, n_kv), bf16["w_o_b"], x, 0)
    x = _run_ffn(x, p, bf16, 3)
    return x, k, v, c_fin, n_fin, m_fin


def kernel(x_prompt, x_sample, cache_k, cache_v, state_C, state_n, state_m, norm_ffn1, w_ffn1_gate, w_ffn1_up, w_ffn1_down, norm_mix, norm_ffn2, w_ffn2_gate, w_ffn2_up, w_ffn2_down, w_in_a, b_ig, b_fg, g_mh, w_out_a, norm_kv, w_kv, g_knorm, w_q_b, g_qnorm, w_o_b):
    bp, seq_p, d = x_prompt.shape
    bs, seq_s, _ = x_sample.shape
    past = cache_k.shape[1]
    assert past == min(MAX_WINDOW, PAST_LEN)
    nh = MLSTM_HEADS
    dk, dv = state_C.shape[3], state_C.shape[4]
    n_kv = cache_k.shape[2] * cache_k.shape[3]

    p = dict(norm_ffn1=norm_ffn1, norm_mix=norm_mix, norm_ffn2=norm_ffn2, norm_kv=norm_kv,
             w_ffn1_gate=w_ffn1_gate, w_ffn1_up=w_ffn1_up, w_ffn1_down=w_ffn1_down,
             w_ffn2_gate=w_ffn2_gate, w_ffn2_up=w_ffn2_up, w_ffn2_down=w_ffn2_down,
             w_in_a_t=jnp.swapaxes(w_in_a, 1, 2), b_ig=b_ig, b_fg=b_fg, g_mh=g_mh, w_out_a=w_out_a, w_kv=w_kv[None],
             g_knorm=g_knorm, w_q_b=w_q_b, g_qnorm=g_qnorm, w_o_b=w_o_b)
    bf16 = {}
    xs = _run_ffn(x_sample.reshape(bs * seq_s, d), p, bf16, 0)

    zeros = lambda *s: jnp.zeros(s, F32)
    y_p, k_p, v_p, c_p, n_p, m_p = _trunk(
        x_prompt.reshape(bp * seq_p, d), seq_p, 0, None, None,
        zeros(bp, nh, dk, dv), zeros(bp, nh, 1, dk), zeros(bp, nh, 1, LANES), p, bf16)
    y_s, k_s, v_s, c_s, n_s, m_s = _trunk(
        xs, seq_s, PAST_LEN, cache_k.reshape(bs, past, n_kv), cache_v.reshape(bs, past, n_kv),
        state_C[:, 0], state_n[:, 0][:, :, None, :],
        jnp.broadcast_to(state_m[:, 0][:, :, None, None], (bs, nh, 1, LANES)), p, bf16, first_ffn_done=True)

    hkv, hd = cache_k.shape[2], cache_k.shape[3]
    keep = min(MAX_WINDOW, seq_p)
    return (y_p.reshape(bp, seq_p, d), y_s.reshape(bs, seq_s, d),
            k_p.reshape(bp, seq_p, hkv, hd)[:, seq_p - keep:], v_p.reshape(bp, seq_p, hkv, hd)[:, seq_p - keep:],
            k_s.reshape(bs, seq_s, hkv, hd), v_s.reshape(bs, seq_s, hkv, hd),
            c_p[:, None], n_p[:, None, :, 0], m_p[:, None, :, 0, 0],
            c_s[:, None], n_s[:, None, :, 0], m_s[:, None, :, 0, 0])
```

```python
import functools

import jax
import jax.numpy as jnp
from jax import lax
from jax.experimental import pallas as pl
from jax.experimental.pallas import tpu as pltpu

F32 = jnp.float32
BF16 = jnp.bfloat16

EPS = 1e-6
MLSTM_HEADS = 8
ATT_HEAD_DIM = 128
DILATED_GROUPS = ((128, 1), (512, 4), (2048, 16))
MAX_WINDOW = max(w for w, _ in DILATED_GROUPS)
ROT_DIM = ATT_HEAD_DIM // 4
ROPE_THETA = 500000.0
PAST_LEN = 16384

V7X_VMEM_LIMIT_BYTES = 56 * 1024 * 1024
LANES = 128
BF16_SUBLANES = 16
NORM_ROWS = 16
NORM_UNROLL = 4
LOAD_CHUNKS = 8
EPI_ROWS = 64
PROMPT_ROW_TILE = 1024
PROJ_ROW_TILE = 512
MLSTM_PROMPT_CHUNK = 256
MLSTM_HEADS_PER_STEP = 4
ATTN_Q_TILE = 256


def _params(n_axes):
    return pltpu.CompilerParams(dimension_semantics=("arbitrary",) * n_axes,
                                vmem_limit_bytes=V7X_VMEM_LIMIT_BYTES)


def _row_tile(m):
    return min(m, PROMPT_ROW_TILE)


def _round_up(x, mult):
    return -(-x // mult) * mult


def _rmsnorm_rows(src_ref, gain_ref, dst_ref, rows):
    step = min(rows, NORM_ROWS)

    def body(r, carry):
        rs = pl.ds(pl.multiple_of(r * step, step), step)
        ms = jnp.mean(jnp.square(src_ref[rs, :]), axis=-1, keepdims=True)
        dst_ref[rs, :] = (src_ref[rs, :] * lax.rsqrt(ms + EPS) * gain_ref[...]).astype(BF16)
        return carry

    trips = rows // step
    lax.fori_loop(0, trips, body, 0, unroll=min(trips, NORM_UNROLL))


def _load_norm_rows(x_hbm, buf_ref, gain_ref, xn_ref, sems, row0, rows):
    n = LOAD_CHUNKS if rows % (LOAD_CHUNKS * NORM_ROWS) == 0 else 1
    piece = rows // n
    copies = [pltpu.make_async_copy(x_hbm.at[pl.ds(row0 + c * piece, piece)],
                                    buf_ref.at[pl.ds(c * piece, piece)], sems.at[c]) for c in range(n)]
    for cp in copies:
        cp.start()
    for c, cp in enumerate(copies):
        cp.wait()
        _rmsnorm_rows(buf_ref.at[pl.ds(c * piece, piece)], gain_ref, xn_ref.at[pl.ds(c * piece, piece)], piece)


def _head_norm(xh, gain):
    return xh * lax.rsqrt(jnp.mean(xh * xh, axis=-1, keepdims=True) + EPS) * gain


def _head_rope(y, cosf, sinf):
    half = ROT_DIM // 2
    lane = lax.broadcasted_iota(jnp.int32, y.shape, 1)
    from_lower = pltpu.roll(y, half, axis=1)
    from_upper = pltpu.roll(y, LANES - half, axis=1)
    rot = jnp.where(lane < half, -from_upper, from_lower)
    return y * cosf + rot * sinf


def _as_bf16(x):
    return x if x.dtype == BF16 else x.astype(BF16)


def _dot_nt(a, b):
    return lax.dot_general(a, b, (((1,), (1,)), ((), ())), preferred_element_type=F32)


def _cast_specs(src, layer, grid, rows_follow_j):
    gi, gj = grid
    _, r, c = src.shape
    if rows_follow_j:
        rb, cb = _round_up(pl.cdiv(r, gj), BF16_SUBLANES), c // gi
        assert c % gi == 0 and cb % LANES == 0
        last = pl.cdiv(r, rb) - 1
        index = lambda lay: (lambda i, j: (lay, jnp.minimum(j, last), i))
    else:
        rb, cb = r // gi, _round_up(pl.cdiv(c, gj), LANES)
        assert r % gi == 0 and rb % BF16_SUBLANES == 0
        last = pl.cdiv(c, cb) - 1
        index = lambda lay: (lambda i, j: (lay, i, jnp.minimum(j, last)))
    return (pl.BlockSpec((None, rb, cb), index(layer)), pl.BlockSpec((None, rb, cb), index(0)),
            jax.ShapeDtypeStruct((1, r, c), BF16))


def _ffn_body(*refs, tm, n_chunk, n_casts, emit):
    x_hbm, gain_ref, wg_ref, wu_ref, wd_ref = refs[:5]
    cast_in = refs[5:5 + n_casts]
    o_hbm = refs[5 + n_casts]
    n_emit = 3 if emit else 0
    emit_refs = refs[6 + n_casts:6 + n_casts + n_emit]
    cast_out = refs[6 + n_casts + n_emit:6 + 2 * n_casts + n_emit]
    acc_ref, xn_ref, sem = refs[6 + 2 * n_casts + n_emit:]

    for src, dst in zip(cast_in, cast_out):
        dst[...] = src[...].astype(BF16)

    i = pl.program_id(0)
    j = pl.program_id(1)
    row0 = pl.multiple_of(i * tm, tm)

    @pl.when(j == 0)
    def _load_and_norm():
        _load_norm_rows(x_hbm, acc_ref, gain_ref, xn_ref, sem, row0, tm)

    wg, wu, wd = wg_ref[...], wu_ref[...], wd_ref
    if emit:
        wg, wu, wd = wg.astype(BF16), wu.astype(BF16), wd_ref[...].astype(BF16)
        emit_refs[0][...] = wg
        emit_refs[1][...] = wu
        emit_refs[2][...] = wd
    xn = xn_ref[...]
    g = jnp.dot(xn, wg, preferred_element_type=F32)
    u = jnp.dot(xn, wu, preferred_element_type=F32)
    a = (0.5 * (g * jax.nn.sigmoid(g)) * u).astype(BF16)
    d = acc_ref.shape[1]
    for c in range(d // n_chunk):
        cs = slice(c * n_chunk, (c + 1) * n_chunk)
        acc_ref[:, cs] += jnp.dot(a, wd[:, cs], preferred_element_type=F32)

    @pl.when(j == pl.num_programs(1) - 1)
    def _store():
        cp = pltpu.make_async_copy(acc_ref, o_hbm.at[pl.ds(row0, tm)], sem.at[LOAD_CHUNKS])
        cp.start()
        cp.wait()


def _ffn(x, gain, layer, weights, w_layer=0, casts=(), emit=False, *, tf=256):
    wg, wu, wd = weights
    m, d = x.shape
    f = wg.shape[2]
    tm = _row_tile(m)
    grid = (m // tm, f // tf)
    assert not emit or grid[0] == 1
    in_specs = [
        pl.BlockSpec(memory_space=pl.ANY),
        pl.BlockSpec((None, 1, d), lambda i, j: (layer, 0, 0)),
        pl.BlockSpec((None, d, tf), lambda i, j: (w_layer, 0, j)),
        pl.BlockSpec((None, d, tf), lambda i, j: (w_layer, 0, j)),
        pl.BlockSpec((None, tf, d), lambda i, j: (w_layer, j, 0)),
    ]
    out_shape = [jax.ShapeDtypeStruct((m, d), F32)]
    out_specs = [pl.BlockSpec(memory_space=pl.ANY)]
    if emit:
        out_shape += [jax.ShapeDtypeStruct((1, d, f), BF16)] * 2 + [jax.ShapeDtypeStruct((1, f, d), BF16)]
        out_specs += [pl.BlockSpec((None, d, tf), lambda i, j: (0, 0, j))] * 2 \
            + [pl.BlockSpec((None, tf, d), lambda i, j: (0, j, 0))]
    for src, src_layer, rows_follow_j in casts:
        in_spec, out_spec, shape = _cast_specs(src, src_layer, grid, rows_follow_j)
        in_specs.append(in_spec)
        out_specs.append(out_spec)
        out_shape.append(shape)
    outs = pl.pallas_call(
        functools.partial(_ffn_body, tm=tm, n_chunk=min(d, 512), n_casts=len(casts), emit=emit),
        out_shape=out_shape,
        grid=grid,
        in_specs=in_specs,
        out_specs=out_specs,
        scratch_shapes=[pltpu.VMEM((tm, d), F32), pltpu.VMEM((tm, d), BF16),
                        pltpu.SemaphoreType.DMA((LOAD_CHUNKS + 1,))],
        compiler_params=_params(2),
        name="ffn",
    )(x, gain.reshape(gain.shape[0], 1, d), wg, wu, wd, *[src for src, _, _ in casts])
    n_emit = 3 if emit else 0
    return outs[0], tuple(outs[1:1 + n_emit]), tuple(outs[1 + n_emit:])


def _in_proj_body(x_hbm, gain_ref, w_ref, wgate_ref, proj_ref, gates_ref, xbuf, xn_ref, sem, *, tm, n_gates):
    i = pl.program_id(0)
    j = pl.program_id(1)

    @pl.when(j == 0)
    def _load_and_norm():
        _load_norm_rows(x_hbm, xbuf, gain_ref, xn_ref, sem, pl.multiple_of(i * tm, tm), tm)
        row = lax.broadcasted_iota(jnp.int32, wgate_ref.shape, 0)
        wgate = jnp.where(row < n_gates, wgate_ref[...], jnp.zeros_like(wgate_ref))
        gates_ref[...] = _dot_nt(xn_ref[...], wgate)

    proj_ref[...] = _dot_nt(xn_ref[...], w_ref[...])


def _in_proj(x, gain, w_in_t, n_main, layer, *, tn=512):
    m, d = x.shape
    n_gates = w_in_t.shape[1] - n_main
    assert n_main % tn == 0 and n_main % LANES == 0 and 0 < n_gates < LANES
    tm = _row_tile(m)
    return pl.pallas_call(
        functools.partial(_in_proj_body, tm=tm, n_gates=n_gates),
        out_shape=(jax.ShapeDtypeStruct((m, n_main), F32), jax.ShapeDtypeStruct((m, LANES), F32)),
        grid=(m // tm, n_main // tn),
        in_specs=[
            pl.BlockSpec(memory_space=pl.ANY),
            pl.BlockSpec((None, 1, d), lambda i, j: (layer, 0, 0)),
            pl.BlockSpec((None, tn, d), lambda i, j: (layer, j, 0)),
            pl.BlockSpec((None, LANES, d), lambda i, j: (layer, n_main // LANES, 0)),
        ],
        out_specs=(pl.BlockSpec((tm, tn), lambda i, j: (i, j)),
                   pl.BlockSpec((tm, LANES), lambda i, j: (i, 0))),
        scratch_shapes=[pltpu.VMEM((tm, d), F32), pltpu.VMEM((tm, d), BF16),
                        pltpu.SemaphoreType.DMA((LOAD_CHUNKS,))],
        compiler_params=_params(2),
        name="in_proj",
    )(x, gain.reshape(gain.shape[0], 1, d), w_in_t, w_in_t)


def _mlstm_body(bi_ref, bf_ref, q_ref, k_ref, v_ref, o_ref, g_ref, gmh_ref, c0_ref, n0_ref, m0_ref,
                y_ref, c_ref, n_ref, m_ref, *, chunk, valid, heads):
    hg = pl.program_id(1)
    c = pl.program_id(2)
    L = chunk
    dk = q_ref.shape[1] // heads
    dv = v_ref.shape[1] // heads

    @pl.when(c == 0)
    def _init():
        c_ref[...] = c0_ref[...]
        n_ref[...] = n0_ref[...]
        m_ref[...] = m0_ref[...]

    t_idx = lax.broadcasted_iota(jnp.int32, (L, L), 0)
    s_idx = lax.broadcasted_iota(jnp.int32, (L, L), 1)
    causal = s_idx <= t_idx
    diag = s_idx == t_idx
    row = lax.broadcasted_iota(jnp.int32, (L, 1), 0)

    for e in range(heads):
        h = hg * heads + e
        qs = slice(e * dk, (e + 1) * dk)
        vs = slice(e * dv, (e + 1) * dv)
        gi_row = g_ref[e, 0:1, :] + bi_ref[h]
        lf_row = jax.nn.log_sigmoid(g_ref[e, 1:2, :] + bf_ref[h])
        lf_b = jnp.broadcast_to(lf_row, (L, L))
        gi_b = jnp.broadcast_to(gi_row, (L, L))
        lf_col = jnp.sum(jnp.where(diag, lf_b, 0.0), axis=1, keepdims=True)
        gi_col = jnp.sum(jnp.where(diag, gi_b, 0.0), axis=1, keepdims=True)
        b_col = jnp.sum(jnp.where(causal, lf_b, 0.0), axis=1, keepdims=True)
        b_row = jnp.sum(jnp.where(t_idx <= s_idx, jnp.broadcast_to(lf_col, (L, L)), 0.0),
                        axis=0, keepdims=True)

        log_d = jnp.where(causal, b_col + (gi_row - b_row), -jnp.inf)
        m_prev = m_ref[e, 0:1, 0:1]
        inter = b_col + m_prev
        m_t = jnp.maximum(inter, jnp.max(log_d, axis=1, keepdims=True))
        decay = jnp.exp(log_d - m_t)

        qf = q_ref[:, qs]
        kf = k_ref[:, qs] * (dk ** -0.5)
        qb = qf.astype(BF16)
        kb = kf.astype(BF16)
        vb = v_ref[:, vs].astype(BF16)
        s = _dot_nt(qb, kb) * decay
        sc = jnp.exp(inter - m_t)
        num = jnp.dot(s.astype(BF16), vb, preferred_element_type=F32) \
            + sc * jnp.dot(qb, c_ref[e].astype(BF16), preferred_element_type=F32)
        den = jnp.sum(s, axis=1, keepdims=True) + sc * jnp.sum(qf * n_ref[e], axis=1, keepdims=True)
        hc = num / jnp.maximum(jnp.abs(den), jnp.exp(-m_t))
        hn = hc * lax.rsqrt(jnp.mean(hc * hc, axis=1, keepdims=True) + EPS) * gmh_ref[e]
        y_ref[:, vs] = (jax.nn.sigmoid(o_ref[:, vs]) * hn).astype(y_ref.dtype)

        m_new = m_t[valid - 1:valid, :]
        b_last = b_col[valid - 1:valid, :]
        dec = jnp.exp(b_last + m_prev - m_new)
        ws = jnp.where(row < valid, jnp.exp(b_last - b_col + gi_col - m_new), 0.0)
        kw = ws * kf
        c_ref[e] = dec * c_ref[e] + lax.dot_general(kw.astype(BF16), vb, (((0,), (0,)), ((), ())),
                                                    preferred_element_type=F32)
        n_ref[e] = dec * n_ref[e] + jnp.sum(kw, axis=0, keepdims=True)
        m_ref[e] = jnp.broadcast_to(m_new, m_ref.shape[1:])


def _mlstm(proj, gates_row, b_i, b_f, g_mh, c0, n0, m0, *, chunk, valid):
    b, t, _ = proj.shape
    nh, dk, dv = c0.shape[1], c0.shape[2], c0.shape[3]
    hps = MLSTM_HEADS_PER_STEP
    assert nh % hps == 0 and (2 * nh * dk) % (hps * dv) == 0
    k_blk0 = nh // hps
    v_blk0 = 2 * nh * dk // (hps * dv)
    o_blk0 = v_blk0 + nh // hps
    smem = pl.BlockSpec(memory_space=pltpu.SMEM)
    state = lambda r, cdim: pl.BlockSpec((None, hps, r, cdim), lambda bi, hi, ci: (bi, hi, 0, 0))
    return pl.pallas_call(
        functools.partial(_mlstm_body, chunk=chunk, valid=valid, heads=hps),
        out_shape=(jax.ShapeDtypeStruct((b, t, nh * dv), BF16),
                   jax.ShapeDtypeStruct(c0.shape, F32),
                   jax.ShapeDtypeStruct(n0.shape, F32),
                   jax.ShapeDtypeStruct(m0.shape, F32)),
        grid=(b, nh // hps, t // chunk),
        in_specs=[
            smem, smem,
            pl.BlockSpec((None, chunk, hps * dk), lambda bi, hi, ci: (bi, ci, hi)),
            pl.BlockSpec((None, chunk, hps * dk), lambda bi, hi, ci: (bi, ci, k_blk0 + hi)),
            pl.BlockSpec((None, chunk, hps * dv), lambda bi, hi, ci: (bi, ci, v_blk0 + hi)),
            pl.BlockSpec((None, chunk, hps * dv), lambda bi, hi, ci: (bi, ci, o_blk0 + hi)),
            pl.BlockSpec((None, hps, 2, chunk), lambda bi, hi, ci: (bi, hi, 0, ci)),
            pl.BlockSpec((hps, 1, dv), lambda bi, hi, ci: (hi, 0, 0)),
            state(dk, dv), state(1, dk), state(1, LANES),
        ],
        out_specs=(pl.BlockSpec((None, chunk, hps * dv), lambda bi, hi, ci: (bi, ci, hi)),
                   state(dk, dv), state(1, dk), state(1, LANES)),
        compiler_params=_params(3),
        name="mlstm",
    )(b_i, b_f, proj, proj, proj, proj, gates_row, g_mh.reshape(nh, 1, dv), c0, n0, m0)


def _matmul_resid_body(a_ref, w_ref, r_ref, o_ref):
    o_ref[...] = r_ref[...] + jnp.dot(a_ref[...], w_ref[...], preferred_element_type=F32)


def _matmul_resid(a, w, resid, layer, *, tn=512):
    m, k = a.shape
    n = w.shape[2]
    tm = _row_tile(m)
    return pl.pallas_call(
        _matmul_resid_body,
        out_shape=jax.ShapeDtypeStruct((m, n), F32),
        grid=(m // tm, n // tn),
        in_specs=[pl.BlockSpec((tm, k), lambda i, j: (i, 0)),
                  pl.BlockSpec((None, k, tn), lambda i, j: (layer, 0, j)),
                  pl.BlockSpec((tm, tn), lambda i, j: (i, j))],
        out_specs=pl.BlockSpec((tm, tn), lambda i, j: (i, j)),
        compiler_params=_params(2),
        name="matmul_resid",
    )(a, w, resid)


def _rope_table_body(invf_ref, cos_ref, sin_ref, *, seq, base):
    rows = cos_ref.shape[0]
    m = lax.broadcasted_iota(jnp.int32, (rows, LANES), 0) + pl.program_id(0) * rows
    pos = (base + (m & (seq - 1))).astype(F32)
    ang = pos * invf_ref[...]
    cos_ref[...] = jnp.cos(ang)
    sin_ref[...] = jnp.sin(ang)


def _rope_tables(m, seq, base):
    assert seq & (seq - 1) == 0
    half = ROT_DIM // 2
    inv_freq = ROPE_THETA ** (-jnp.arange(half, dtype=F32) / half)
    invf = jnp.zeros((1, LANES), F32).at[0, :ROT_DIM].set(jnp.tile(inv_freq, 2))
    rows = min(m, 512)
    spec = pl.BlockSpec((rows, LANES), lambda i: (i, 0))
    return pl.pallas_call(
        functools.partial(_rope_table_body, seq=seq, base=base),
        out_shape=(jax.ShapeDtypeStruct((m, LANES), F32),) * 2,
        grid=(m // rows,),
        in_specs=[pl.BlockSpec((1, LANES), lambda i: (0, 0))],
        out_specs=(spec, spec),
        compiler_params=_params(1),
        name="rope_tables",
    )(invf)


def _to_head_slabs(res_ref, x):
    for hh in range(res_ref.shape[0]):
        res_ref[hh] = x[:, hh * ATT_HEAD_DIM:(hh + 1) * ATT_HEAD_DIM]


def _norm_rope_to_slabs(nat_ref, col0, res_ref, gain_ref, cos_ref, sin_ref):
    n_heads, rows, _ = res_ref.shape
    step = min(rows, EPI_ROWS)
    for r in range(rows // step):
        rs = slice(r * step, (r + 1) * step)
        for hh in range(n_heads):
            hs = slice(col0 + hh * ATT_HEAD_DIM, col0 + (hh + 1) * ATT_HEAD_DIM)
            res_ref[hh, rs, :] = _head_norm(nat_ref[rs, hs], gain_ref[...])
    for r in range(rows // step):
        rs = slice(r * step, (r + 1) * step)
        cosf = cos_ref[rs, :]
        sinf = sin_ref[rs, :]
        for hh in range(n_heads):
            res_ref[hh, rs, :] = _head_rope(res_ref[hh, rs, :], cosf, sinf)


def _emit_slabs(res_ref, out_ref, dil):
    n_heads, rows, _ = res_ref.shape
    n = n_heads * ATT_HEAD_DIM
    for r in range(dil):
        for hh in range(n_heads):
            col = r * n + hh * ATT_HEAD_DIM
            src = res_ref[hh] if dil == 1 else res_ref[hh, pl.ds(r, rows // dil, stride=dil), :]
            out_ref[:, col:col + ATT_HEAD_DIM] = src.astype(out_ref.dtype)


def _emit_heads(res_ref, out_ref):
    n_heads, rows, _ = res_ref.shape
    for hh in range(n_heads):
        out_ref[pl.ds(hh, rows, stride=n_heads), :] = res_ref[hh]


def _class_spec(tm, n, seq, dil):
    tiles = seq // tm
    assert seq % tm == 0 and tm % (dil * BF16_SUBLANES) == 0
    return pl.BlockSpec((None, tm // dil, dil * n), lambda i, *_: (i // tiles, i % tiles, 0))


def _kv_proj_body(*refs, tm, dilations):
    n_cls = len(dilations)
    x_hbm, gain_ref, w_ref, kgain_ref, cos_ref, sin_ref, k_ref, v_ref = refs[:8]
    kc_refs = refs[8:8 + n_cls]
    vc_refs = refs[8 + n_cls:8 + 2 * n_cls]
    xbuf, xn_ref, res_ref, sem = refs[8 + 2 * n_cls:]
    i = pl.program_id(0)
    j = pl.program_id(1)

    n = w_ref.shape[1]

    @pl.when(j == 0)
    def _k_half():
        _load_norm_rows(x_hbm, xbuf, gain_ref, xn_ref, sem, pl.multiple_of(i * tm, tm), tm)
        xbuf[:, 0:n] = jnp.dot(xn_ref[...], w_ref[...], preferred_element_type=F32)

    @pl.when(j == 1)
    def _v_half():
        v = jnp.dot(xn_ref[...], w_ref[...], preferred_element_type=F32)
        _norm_rope_to_slabs(xbuf, 0, res_ref, kgain_ref, cos_ref, sin_ref)
        _emit_heads(res_ref, k_ref)
        for dil, ref in zip(dilations, kc_refs):
            _emit_slabs(res_ref, ref, dil)
        _to_head_slabs(res_ref, v)
        _emit_heads(res_ref, v_ref)
        for dil, ref in zip(dilations, vc_refs):
            _emit_slabs(res_ref, ref, dil)


def _kv_proj(x, gain, w_kv, g_knorm, cosf, sinf, seq, dilations=()):
    m, d = x.shape
    n = w_kv.shape[2] // 2
    n_heads = n // ATT_HEAD_DIM
    tm = min(m, PROJ_ROW_TILE)
    out_spec = pl.BlockSpec((tm * n_heads, ATT_HEAD_DIM), lambda i, j: (i, 0))
    tab_spec = pl.BlockSpec((tm, LANES), lambda i, j: (i, 0))
    cls_specs = [_class_spec(tm, n, seq, dil) for dil in dilations]
    cls_shapes = [jax.ShapeDtypeStruct((m // seq, seq // dil, dil * n), BF16) for dil in dilations]
    outs = pl.pallas_call(
        functools.partial(_kv_proj_body, tm=tm, dilations=tuple(dilations)),
        out_shape=[jax.ShapeDtypeStruct((m * n_heads, ATT_HEAD_DIM), F32)] * 2 + cls_shapes * 2,
        grid=(m // tm, 2),
        in_specs=[pl.BlockSpec(memory_space=pl.ANY),
                  pl.BlockSpec((1, d), lambda i, j: (0, 0)),
                  pl.BlockSpec((None, d, n), lambda i, j: (0, 0, j)),
                  pl.BlockSpec((1, ATT_HEAD_DIM), lambda i, j: (0, 0)),
                  tab_spec, tab_spec],
        out_specs=[out_spec, out_spec] + cls_specs * 2,
        scratch_shapes=[pltpu.VMEM((tm, d), F32), pltpu.VMEM((tm, d), BF16),
                        pltpu.VMEM((n_heads, tm, ATT_HEAD_DIM), F32),
                        pltpu.SemaphoreType.DMA((LOAD_CHUNKS,))],
        compiler_params=_params(2),
        name="kv_proj",
    )(x, gain.reshape(1, d), w_kv, g_knorm.reshape(1, ATT_HEAD_DIM), cosf, sinf)
    n_cls = len(dilations)
    return outs[0], outs[1], tuple(outs[2:2 + n_cls]), tuple(outs[2 + n_cls:])


def _q_proj_body(*refs, tm, n_groups, dilations):
    x_hbm, gain_ref, w_ref, qgain_ref, cos_ref, sin_ref = refs[:6]
    n_out = max(len(dilations), 1)
    q_refs = refs[6:6 + n_out]
    xbuf, xn_ref, res_ref, sem = refs[6 + n_out:]
    i = pl.program_id(0)
    j = pl.program_id(1)

    tn = w_ref.shape[1]
    for g in range(n_groups + 1):
        @pl.when(j == g)
        def _step(g=g):
            if g == 0:
                _load_norm_rows(x_hbm, xbuf, gain_ref, xn_ref, sem, pl.multiple_of(i * tm, tm), tm)
            if g < n_groups:
                proj = jnp.dot(xn_ref[...], w_ref[...], preferred_element_type=F32)
            if g > 0:
                _norm_rope_to_slabs(xbuf, ((g - 1) % 2) * tn, res_ref, qgain_ref, cos_ref, sin_ref)
                if dilations:
                    _emit_slabs(res_ref, q_refs[g - 1], dilations[g - 1])
                else:
                    _emit_slabs(res_ref, q_refs[0], 1)
            if g < n_groups:
                xbuf[:, (g % 2) * tn:(g % 2 + 1) * tn] = proj


def _q_proj(x, gain, w_q, g_qnorm, cosf, sinf, layer_mix, layer_b, seq, dilations=()):
    m, d = x.shape
    n_groups = g_qnorm.shape[1]
    tn = w_q.shape[2] // n_groups
    tm = min(m, PROJ_ROW_TILE)
    tab_spec = pl.BlockSpec((tm, LANES), lambda i, j: (i, 0))
    if dilations:
        assert len(dilations) == n_groups
        out_shape = [jax.ShapeDtypeStruct((m // seq, seq // dil, dil * tn), BF16) for dil in dilations]
        out_specs = [_class_spec(tm, tn, seq, dil) for dil in dilations]
    else:
        out_shape = [jax.ShapeDtypeStruct((m, n_groups * tn), BF16)]
        out_specs = [pl.BlockSpec((tm, tn), lambda i, j: (i, jnp.maximum(j - 1, 0)))]
    assert d >= 2 * tn
    return pl.pallas_call(
        functools.partial(_q_proj_body, tm=tm, n_groups=n_groups, dilations=tuple(dilations)),
        out_shape=out_shape,
        grid=(m // tm, n_groups + 1),
        in_specs=[pl.BlockSpec(memory_space=pl.ANY),
                  pl.BlockSpec((None, 1, d), lambda i, j: (layer_mix, 0, 0)),
                  pl.BlockSpec((None, d, tn), lambda i, j: (layer_b, 0, jnp.minimum(j, n_groups - 1))),
                  pl.BlockSpec((None, None, 1, ATT_HEAD_DIM), lambda i, j: (layer_b, jnp.maximum(j - 1, 0), 0, 0)),
                  tab_spec, tab_spec],
        out_specs=out_specs,
        scratch_shapes=[pltpu.VMEM((tm, d), F32), pltpu.VMEM((tm, d), BF16),
                        pltpu.VMEM((tn // ATT_HEAD_DIM, tm, ATT_HEAD_DIM), F32),
                        pltpu.SemaphoreType.DMA((LOAD_CHUNKS,))],
        compiler_params=_params(2),
        name="q_proj",
    )(x, gain.reshape(gain.shape[0], 1, d), w_q,
      g_qnorm.reshape(g_qnorm.shape[0], n_groups, 1, ATT_HEAD_DIM), cosf, sinf)


def _softmax_pieces(scores, masks):
    masked = [jnp.where(mk, sc, -jnp.inf) for sc, mk in zip(scores, masks)]
    m = functools.reduce(jnp.maximum, [jnp.max(sc, axis=1, keepdims=True) for sc in masked])
    probs = [jnp.exp(sc - m) for sc in masked]
    l = functools.reduce(jnp.add, [jnp.sum(p, axis=1, keepdims=True) for p in probs])
    return probs, l, m


def _attn_prompt_body(*refs, tq, win, dil, has_prev, n_heads):
    if has_prev:
        q_ref, kc_ref, vc_ref, kp_ref, vp_ref, o_ref, lse_ref = refs
    else:
        q_ref, kc_ref, vc_ref, o_ref, lse_ref = refs
    qi = pl.program_id(1)
    res = pl.program_id(2)
    scale = ATT_HEAD_DIM ** -0.5
    row = lax.broadcasted_iota(jnp.int32, (tq, tq), 0)
    col = lax.broadcasted_iota(jnp.int32, (tq, tq), 1)
    mask_c = (col <= row) & (row - col <= win)
    if has_prev:
        rowp = lax.broadcasted_iota(jnp.int32, (tq, win), 0)
        colp = lax.broadcasted_iota(jnp.int32, (tq, win), 1)
        mask_p = (colp >= rowp) & (qi > 0)
    rows = slice(None) if dil == 1 else pl.ds(res, tq, stride=dil)
    for hh in range(n_heads):
        hs = slice(hh * ATT_HEAD_DIM, (hh + 1) * ATT_HEAD_DIM)
        qh = q_ref[:, hs]
        scores = [_dot_nt(qh, _as_bf16(kc_ref[:, hs])) * scale]
        masks = [mask_c]
        values = [_as_bf16(vc_ref[:, hs])]
        if has_prev:
            scores.append(_dot_nt(qh, _as_bf16(kp_ref[:, hs])) * scale)
            masks.append(mask_p)
            values.append(_as_bf16(vp_ref[:, hs]))
        probs, l, m = _softmax_pieces(scores, masks)
        acc = functools.reduce(jnp.add, [jnp.dot(p.astype(BF16), v, preferred_element_type=F32)
                                         for p, v in zip(probs, values)])
        o_ref[hh, rows, :] = acc / l
        lse_ref[hh, rows, :] = jnp.broadcast_to(m + jnp.log(l), (tq, ATT_HEAD_DIM))


def _attn_prompt_group(q, k, v, seq, window, dil):
    b = q.shape[0]
    n = q.shape[2] // dil
    n_heads = n // ATT_HEAD_DIM
    tc = seq // dil
    win = window // dil
    tq = min(tc, ATTN_Q_TILE)
    has_prev = tc > tq
    assert tq % win == 0 and tc % tq == 0
    cur = pl.BlockSpec((None, tq, n), lambda bi, qi, r: (bi, qi, r))
    prev = pl.BlockSpec((None, win, n), lambda bi, qi, r: (bi, jnp.maximum(qi * (tq // win) - 1, 0), r))
    out = pl.BlockSpec((None, n_heads, tq * dil, ATT_HEAD_DIM), lambda bi, qi, r: (bi, 0, qi, 0))
    return pl.pallas_call(
        functools.partial(_attn_prompt_body, tq=tq, win=win, dil=dil, has_prev=has_prev, n_heads=n_heads),
        out_shape=(jax.ShapeDtypeStruct((b, n_heads, seq, ATT_HEAD_DIM), F32),) * 2,
        grid=(b, tc // tq, dil),
        in_specs=[cur, cur, cur] + ([prev, prev] if has_prev else []),
        out_specs=(out, out),
        compiler_params=_params(3),
        name=f"attn_prompt_d{dil}",
    )(q, k, v, *([k, v] if has_prev else []))


def _attn_out_body(*refs, n_groups):
    o_refs = refs[:n_groups]
    l_refs = refs[n_groups:2 * n_groups]
    w_ref, r_ref, out_ref, att_ref = refs[2 * n_groups:]

    @pl.when(pl.program_id(1) == 0)
    def _mix():
        for hh in range(o_refs[0].shape[0]):
            lses = [l[hh] for l in l_refs]
            m = functools.reduce(jnp.maximum, lses)
            es = [jnp.exp(l - m) for l in lses]
            tot = functools.reduce(jnp.add, es)
            att = functools.reduce(jnp.add, [(e / tot) * o[hh] for e, o in zip(es, o_refs)])
            att_ref[:, hh * ATT_HEAD_DIM:(hh + 1) * ATT_HEAD_DIM] = att.astype(BF16)

    out_ref[...] = r_ref[...] + jnp.dot(att_ref[...], w_ref[...], preferred_element_type=F32)


def _attn_out(outs, lses, w_o, resid, layer, *, tm=256):
    b, n_heads, seq, hd = outs[0].shape
    m, n = resid.shape
    k = n_heads * hd
    tn = n
    tm = min(seq, tm)
    tiles = seq // tm
    blk = pl.BlockSpec((None, n_heads, tm, hd), lambda i, j: (i // tiles, 0, i % tiles, 0))
    return pl.pallas_call(
        functools.partial(_attn_out_body, n_groups=len(outs)),
        out_shape=jax.ShapeDtypeStruct((m, n), F32),
        grid=(m // tm, n // tn),
        in_specs=[blk] * (2 * len(outs)) + [pl.BlockSpec((None, k, tn), lambda i, j: (layer, 0, j)),
                                            pl.BlockSpec((tm, tn), lambda i, j: (i, j))],
        out_specs=pl.BlockSpec((tm, tn), lambda i, j: (i, j)),
        scratch_shapes=[pltpu.VMEM((tm, k), BF16)],
        compiler_params=_params(2),
        name="attn_out",
    )(*outs, *lses, w_o, resid)


def _attn_sample_body(q_ref, kc_ref, vc_ref, kn_ref, vn_ref, o_ref, *, groups, n_heads, n_new):
    rows = q_ref.shape[0]
    cache = kc_ref.shape[0] // n_heads
    n = n_heads * ATT_HEAD_DIM
    scale = ATT_HEAD_DIM ** -0.5
    dist_c = cache + lax.broadcasted_iota(jnp.int32, (rows, cache), 0) \
        - lax.broadcasted_iota(jnp.int32, (rows, cache), 1)
    dist_n = lax.broadcasted_iota(jnp.int32, (rows, rows), 0) - lax.broadcasted_iota(jnp.int32, (rows, rows), 1)
    col_n = lax.broadcasted_iota(jnp.int32, (rows, rows), 1)
    for hh in range(n_heads):
        hs = slice(hh * ATT_HEAD_DIM, (hh + 1) * ATT_HEAD_DIM)
        kc = kc_ref[pl.ds(hh, cache, stride=n_heads), :].astype(BF16)
        vc = vc_ref[pl.ds(hh, cache, stride=n_heads), :].astype(BF16)
        kn = kn_ref[:, hs].astype(BF16)
        vn = vn_ref[:, hs].astype(BF16)
        outs, lses = [], []
        for g, (window, dil) in enumerate(groups):
            qh = q_ref[:, g * n + hh * ATT_HEAD_DIM:g * n + (hh + 1) * ATT_HEAD_DIM]
            mask_c = (dist_c <= window) & ((dist_c & (dil - 1)) == 0)
            mask_n = (dist_n >= 0) & (col_n < n_new) & ((dist_n & (dil - 1)) == 0)
            probs, l, m = _softmax_pieces([_dot_nt(qh, kc) * scale, _dot_nt(qh, kn) * scale], [mask_c, mask_n])
            acc = jnp.dot(probs[0].astype(BF16), vc, preferred_element_type=F32) \
                + jnp.dot(probs[1].astype(BF16), vn, preferred_element_type=F32)
            outs.append(acc / l)
            lses.append(m + jnp.log(l))
        m_all = functools.reduce(jnp.maximum, lses)
        es = [jnp.exp(l - m_all) for l in lses]
        tot = functools.reduce(jnp.add, es)
        o_ref[:, hs] = functools.reduce(jnp.add, [(e / tot) * o for e, o in zip(es, outs)]).astype(o_ref.dtype)


def _attn_sample(q, k_new, v_new, cache_k, cache_v, n_new):
    b, rows, n = k_new.shape
    cache, n_heads = cache_k.shape[1], cache_k.shape[2]
    assert cache == MAX_WINDOW and all(d & (d - 1) == 0 for _, d in DILATED_GROUPS)
    row_blk = lambda width: pl.BlockSpec((None, rows, width), lambda bi: (bi, 0, 0))
    cache_blk = pl.BlockSpec((None, cache * n_heads, ATT_HEAD_DIM), lambda bi: (bi, 0, 0))
    cache_k = cache_k.reshape(b, cache * n_heads, ATT_HEAD_DIM)
    cache_v = cache_v.reshape(b, cache * n_heads, ATT_HEAD_DIM)
    return pl.pallas_call(
        functools.partial(_attn_sample_body, groups=DILATED_GROUPS, n_heads=n // ATT_HEAD_DIM, n_new=n_new),
        out_shape=jax.ShapeDtypeStruct((b, rows, n), BF16),
        grid=(b,),
        in_specs=[row_blk(q.shape[2]), cache_blk, cache_blk, row_blk(n), row_blk(n)],
        out_specs=row_blk(n),
        compiler_params=_params(1),
        name="attn_sample",
    )(q, cache_k, cache_v, k_new, v_new)


FFN_ORDER = (("ffn1", 0), ("ffn2", 0), ("ffn1", 1), ("ffn2", 1))
SMALL_WEIGHTS = (("w_in_a_t", True), ("w_out_a", False), ("w_kv", False), ("w_q_b", False), ("w_o_b", False))


def _ffn_stacks(p, step):
    name, _ = FFN_ORDER[step]
    return tuple(p[f"w_{name}_{part}"] for part in ("gate", "up", "down"))


def _run_ffn(x, p, bf16, step):
    name, layer = FFN_ORDER[step]
    if ("ffn", step) not in bf16:
        x, bf16[("ffn", step)], _ = _ffn(x, p[f"norm_{name}"], layer, _ffn_stacks(p, step), layer, emit=True)
        return x
    casts, keys = [], []
    if step + 1 < len(FFN_ORDER) and ("ffn", step + 1) not in bf16:
        gate, up, down = _ffn_stacks(p, step + 1)
        nxt_layer = FFN_ORDER[step + 1][1]
        casts += [(gate, nxt_layer, False), (up, nxt_layer, False), (down, nxt_layer, True)]
        keys += [("ffn", step + 1)] * 3
    for key, rows_follow_j in SMALL_WEIGHTS:
        if key not in bf16:
            assert p[key].shape[0] == 1
            casts.append((p[key], 0, rows_follow_j))
            keys.append(key)
    x, _, cast = _ffn(x, p[f"norm_{name}"], layer, bf16[("ffn", step)], casts=tuple(casts))
    for key, arr in zip(keys, cast):
        if key[0] == "ffn":
            bf16[key] = bf16.get(key, ()) + (arr,)
        else:
            bf16[key] = arr
    return x


def _trunk(x, seq, base, cache_k, cache_v, c0, n0, m0, p, bf16, first_ffn_done=False):
    m_rows, d = x.shape
    b = m_rows // seq
    nh = MLSTM_HEADS
    dk, dv = c0.shape[2], c0.shape[3]
    fresh = cache_k is None
    dilations = tuple(dil for _, dil in DILATED_GROUPS)

    if not first_ffn_done:
        x = _run_ffn(x, p, bf16, 0)
    proj, gates = _in_proj(x, p["norm_mix"], bf16["w_in_a_t"], 2 * nh * dk + nh * dv + d, 0)
    if seq % MLSTM_PROMPT_CHUNK == 0:
        chunk, valid, t_pad = MLSTM_PROMPT_CHUNK, MLSTM_PROMPT_CHUNK, seq
    else:
        assert seq <= BF16_SUBLANES
        chunk, valid, t_pad = BF16_SUBLANES, seq, BF16_SUBLANES
    proj3 = proj.reshape(b, seq, -1)
    gates3 = gates.reshape(b, seq, LANES)[:, :, :2 * nh]
    if t_pad != seq:
        proj3 = jnp.pad(proj3, ((0, 0), (0, t_pad - seq), (0, 0)))
        gates3 = jnp.pad(gates3, ((0, 0), (0, t_pad - seq), (0, 0)))
    gates_row = gates3.reshape(b, t_pad, 2, nh).transpose(0, 3, 2, 1)
    y, c_fin, n_fin, m_fin = _mlstm(proj3, gates_row, p["b_ig"][0], p["b_fg"][0], p["g_mh"][0],
                                    c0, n0, m0, chunk=chunk, valid=valid)
    y = y[:, :seq].reshape(m_rows, nh * dv)
    x = _matmul_resid(y, bf16["w_out_a"], x, 0)
    x = _run_ffn(x, p, bf16, 1)

    cosf, sinf = _rope_tables(m_rows, seq, base)
    cls_dils = dilations if fresh else ()
    k, v, k_cls, v_cls = _kv_proj(x, p["norm_kv"], bf16["w_kv"], p["g_knorm"], cosf, sinf, seq, cls_dils)

    x = _run_ffn(x, p, bf16, 2)
    q = _q_proj(x, p["norm_mix"], bf16["w_q_b"], p["g_qnorm"], cosf, sinf, 1, 0, seq, dilations if fresh else ())
    n_kv = k.shape[0] // m_rows * k.shape[1]
    if fresh:
        outs, lses = [], []
        for q_g, k_g, v_g, (window, dil) in zip(q, k_cls, v_cls, DILATED_GROUPS):
            o_g, l_g = _attn_prompt_group(q_g, k_g, v_g, seq, window, dil)
            outs.append(o_g)
            lses.append(l_g)
        x = _attn_out(outs, lses, bf16["w_o_b"], x, 0)
    else:
        pad = ((0, 0), (0, BF16_SUBLANES - seq), (0, 0))
        att = _attn_sample(jnp.pad(q[0].reshape(b, seq, -1), pad), jnp.pad(k.reshape(b, seq, n_kv), pad),
                           jnp.pad(v.reshape(b, seq, n_kv), pad), cache_k, cache_v, seq)
        x = _matmul_resid(att[:, :seq].reshape(m_rows, n_kv), bf16["w_o_b"], x, 0)
    x = _run_ffn(x, p, bf16, 3)
    return x, k, v, c_fin, n_fin, m_fin


def kernel(x_prompt, x_sample, cache_k, cache_v, state_C, state_n, state_m, norm_ffn1, w_ffn1_gate, w_ffn1_up, w_ffn1_down, norm_mix, norm_ffn2, w_ffn2_gate, w_ffn2_up, w_ffn2_down, w_in_a, b_ig, b_fg, g_mh, w_out_a, norm_kv, w_kv, g_knorm, w_q_b, g_qnorm, w_o_b):
    bp, seq_p, d = x_prompt.shape
    bs, seq_s, _ = x_sample.shape
    past = cache_k.shape[1]
    assert past == min(MAX_WINDOW, PAST_LEN)
    nh = MLSTM_HEADS
    dk, dv = state_C.shape[3], state_C.shape[4]

    p = dict(norm_ffn1=norm_ffn1, norm_mix=norm_mix, norm_ffn2=norm_ffn2, norm_kv=norm_kv,
             w_ffn1_gate=w_ffn1_gate, w_ffn1_up=w_ffn1_up, w_ffn1_down=w_ffn1_down,
             w_ffn2_gate=w_ffn2_gate, w_ffn2_up=w_ffn2_up, w_ffn2_down=w_ffn2_down,
             w_in_a_t=jnp.swapaxes(w_in_a, 1, 2), b_ig=b_ig, b_fg=b_fg, g_mh=g_mh, w_out_a=w_out_a, w_kv=w_kv[None],
             g_knorm=g_knorm, w_q_b=w_q_b, g_qnorm=g_qnorm, w_o_b=w_o_b)
    bf16 = {}
    xs = _run_ffn(x_sample.reshape(bs * seq_s, d), p, bf16, 0)

    zeros = lambda *s: jnp.zeros(s, F32)
    y_p, k_p, v_p, c_p, n_p, m_p = _trunk(
        x_prompt.reshape(bp * seq_p, d), seq_p, 0, None, None,
        zeros(bp, nh, dk, dv), zeros(bp, nh, 1, dk), zeros(bp, nh, 1, LANES), p, bf16)
    y_s, k_s, v_s, c_s, n_s, m_s = _trunk(
        xs, seq_s, PAST_LEN, cache_k, cache_v,
        state_C[:, 0], state_n[:, 0][:, :, None, :],
        jnp.broadcast_to(state_m[:, 0][:, :, None, None], (bs, nh, 1, LANES)), p, bf16, first_ffn_done=True)

    hkv, hd = cache_k.shape[2], cache_k.shape[3]
    keep = min(MAX_WINDOW, seq_p)
    return (y_p.reshape(bp, seq_p, d), y_s.reshape(bs, seq_s, d),
            k_p.reshape(bp, seq_p, hkv, hd)[:, seq_p - keep:], v_p.reshape(bp, seq_p, hkv, hd)[:, seq_p - keep:],
            k_s.reshape(bs, seq_s, hkv, hd), v_s.reshape(bs, seq_s, hkv, hd),
            c_p[:, None], n_p[:, None, :, 0], m_p[:, None, :, 0, 0],
            c_s[:, None], n_s[:, None, :, 0], m_s[:, None, :, 0, 0])
```

```python
import functools

import jax
import jax.numpy as jnp
from jax import lax
from jax.experimental import pallas as pl
from jax.experimental.pallas import tpu as pltpu

F32 = jnp.float32
BF16 = jnp.bfloat16

EPS = 1e-6
MLSTM_HEADS = 8
ATT_HEAD_DIM = 128
DILATED_GROUPS = ((128, 1), (512, 4), (2048, 16))
MAX_WINDOW = max(w for w, _ in DILATED_GROUPS)
ROT_DIM = ATT_HEAD_DIM // 4
ROPE_THETA = 500000.0
PAST_LEN = 16384

V7X_VMEM_LIMIT_BYTES = 56 * 1024 * 1024
LANES = 128
BF16_SUBLANES = 16
NORM_ROWS = 16
NORM_UNROLL = 4
LOAD_CHUNKS = 8
EPI_ROWS = 64
PROMPT_ROW_TILE = 1024
PROJ_ROW_TILE = 512
MLSTM_PROMPT_CHUNK = 256
MLSTM_HEADS_PER_STEP = 4
ATTN_Q_TILE = 256


def _params(n_axes):
    return pltpu.CompilerParams(dimension_semantics=("arbitrary",) * n_axes,
                                vmem_limit_bytes=V7X_VMEM_LIMIT_BYTES)


def _row_tile(m):
    return min(m, PROMPT_ROW_TILE)


def _round_up(x, mult):
    return -(-x // mult) * mult


def _rmsnorm_rows(src_ref, gain_ref, dst_ref, rows):
    step = min(rows, NORM_ROWS)

    def body(r, carry):
        rs = pl.ds(pl.multiple_of(r * step, step), step)
        ms = jnp.mean(jnp.square(src_ref[rs, :]), axis=-1, keepdims=True)
        dst_ref[rs, :] = (src_ref[rs, :] * lax.rsqrt(ms + EPS) * gain_ref[...]).astype(BF16)
        return carry

    trips = rows // step
    lax.fori_loop(0, trips, body, 0, unroll=min(trips, NORM_UNROLL))


def _load_norm_rows(x_hbm, buf_ref, gain_ref, xn_ref, sems, row0, rows):
    n = LOAD_CHUNKS if rows % (LOAD_CHUNKS * NORM_ROWS) == 0 else 1
    piece = rows // n
    copies = [pltpu.make_async_copy(x_hbm.at[pl.ds(row0 + c * piece, piece)],
                                    buf_ref.at[pl.ds(c * piece, piece)], sems.at[c]) for c in range(n)]
    for cp in copies:
        cp.start()
    for c, cp in enumerate(copies):
        cp.wait()
        _rmsnorm_rows(buf_ref.at[pl.ds(c * piece, piece)], gain_ref, xn_ref.at[pl.ds(c * piece, piece)], piece)


def _head_norm(xh, gain):
    return xh * lax.rsqrt(jnp.mean(xh * xh, axis=-1, keepdims=True) + EPS) * gain


def _head_rope(y, cosf, sinf):
    half = ROT_DIM // 2
    lane = lax.broadcasted_iota(jnp.int32, y.shape, 1)
    from_lower = pltpu.roll(y, half, axis=1)
    from_upper = pltpu.roll(y, LANES - half, axis=1)
    rot = jnp.where(lane < half, -from_upper, from_lower)
    return y * cosf + rot * sinf


def _as_bf16(x):
    return x if x.dtype == BF16 else x.astype(BF16)


def _dot_nt(a, b):
    return lax.dot_general(a, b, (((1,), (1,)), ((), ())), preferred_element_type=F32)


def _cast_specs(src, layer, grid, rows_follow_j):
    gi, gj = grid
    _, r, c = src.shape
    if rows_follow_j:
        rb, cb = _round_up(pl.cdiv(r, gj), BF16_SUBLANES), c // gi
        assert c % gi == 0 and cb % LANES == 0
        last = pl.cdiv(r, rb) - 1
        index = lambda lay: (lambda i, j: (lay, jnp.minimum(j, last), i))
    else:
        rb, cb = r // gi, _round_up(pl.cdiv(c, gj), LANES)
        assert r % gi == 0 and rb % BF16_SUBLANES == 0
        last = pl.cdiv(c, cb) - 1
        index = lambda lay: (lambda i, j: (lay, i, jnp.minimum(j, last)))
    return (pl.BlockSpec((None, rb, cb), index(layer)), pl.BlockSpec((None, rb, cb), index(0)),
            jax.ShapeDtypeStruct((1, r, c), BF16))


def _ffn_body(*refs, tm, n_chunk, n_casts, emit):
    x_hbm, gain_ref, wg_ref, wu_ref, wd_ref = refs[:5]
    cast_in = refs[5:5 + n_casts]
    o_hbm = refs[5 + n_casts]
    n_emit = 3 if emit else 0
    emit_refs = refs[6 + n_casts:6 + n_casts + n_emit]
    cast_out = refs[6 + n_casts + n_emit:6 + 2 * n_casts + n_emit]
    acc_ref, xn_ref, sem = refs[6 + 2 * n_casts + n_emit:]

    for src, dst in zip(cast_in, cast_out):
        dst[...] = src[...].astype(BF16)

    i = pl.program_id(0)
    j = pl.program_id(1)
    row0 = pl.multiple_of(i * tm, tm)

    @pl.when(j == 0)
    def _load_and_norm():
        _load_norm_rows(x_hbm, acc_ref, gain_ref, xn_ref, sem, row0, tm)

    wg, wu, wd = wg_ref[...], wu_ref[...], wd_ref
    if emit:
        wg, wu, wd = wg.astype(BF16), wu.astype(BF16), wd_ref[...].astype(BF16)
        emit_refs[0][...] = wg
        emit_refs[1][...] = wu
        emit_refs[2][...] = wd
    xn = xn_ref[...]
    g = jnp.dot(xn, wg, preferred_element_type=F32)
    u = jnp.dot(xn, wu, preferred_element_type=F32)
    a = (0.5 * (g * jax.nn.sigmoid(g)) * u).astype(BF16)
    d = acc_ref.shape[1]
    for c in range(d // n_chunk):
        cs = slice(c * n_chunk, (c + 1) * n_chunk)
        acc_ref[:, cs] += jnp.dot(a, wd[:, cs], preferred_element_type=F32)

    @pl.when(j == pl.num_programs(1) - 1)
    def _store():
        cp = pltpu.make_async_copy(acc_ref, o_hbm.at[pl.ds(row0, tm)], sem.at[LOAD_CHUNKS])
        cp.start()
        cp.wait()


def _ffn(x, gain, layer, weights, w_layer=0, casts=(), emit=False, *, tf=256):
    wg, wu, wd = weights
    m, d = x.shape
    f = wg.shape[2]
    tm = _row_tile(m)
    grid = (m // tm, f // tf)
    assert not emit or grid[0] == 1
    in_specs = [
        pl.BlockSpec(memory_space=pl.ANY),
        pl.BlockSpec((None, 1, d), lambda i, j: (layer, 0, 0)),
        pl.BlockSpec((None, d, tf), lambda i, j: (w_layer, 0, j)),
        pl.BlockSpec((None, d, tf), lambda i, j: (w_layer, 0, j)),
        pl.BlockSpec((None, tf, d), lambda i, j: (w_layer, j, 0)),
    ]
    out_shape = [jax.ShapeDtypeStruct((m, d), F32)]
    out_specs = [pl.BlockSpec(memory_space=pl.ANY)]
    if emit:
        out_shape += [jax.ShapeDtypeStruct((1, d, f), BF16)] * 2 + [jax.ShapeDtypeStruct((1, f, d), BF16)]
        out_specs += [pl.BlockSpec((None, d, tf), lambda i, j: (0, 0, j))] * 2 \
            + [pl.BlockSpec((None, tf, d), lambda i, j: (0, j, 0))]
    for src, src_layer, rows_follow_j in casts:
        in_spec, out_spec, shape = _cast_specs(src, src_layer, grid, rows_follow_j)
        in_specs.append(in_spec)
        out_specs.append(out_spec)
        out_shape.append(shape)
    outs = pl.pallas_call(
        functools.partial(_ffn_body, tm=tm, n_chunk=min(d, 512), n_casts=len(casts), emit=emit),
        out_shape=out_shape,
        grid=grid,
        in_specs=in_specs,
        out_specs=out_specs,
        scratch_shapes=[pltpu.VMEM((tm, d), F32), pltpu.VMEM((tm, d), BF16),
                        pltpu.SemaphoreType.DMA((LOAD_CHUNKS + 1,))],
        compiler_params=_params(2),
        name="ffn",
    )(x, gain.reshape(gain.shape[0], 1, d), wg, wu, wd, *[src for src, _, _ in casts])
    n_emit = 3 if emit else 0
    return outs[0], tuple(outs[1:1 + n_emit]), tuple(outs[1 + n_emit:])


def _in_proj_body(x_hbm, gain_ref, w_ref, wgate_ref, proj_ref, gates_ref, xbuf, xn_ref, sem, *, tm, n_gates):
    i = pl.program_id(0)
    j = pl.program_id(1)

    @pl.when(j == 0)
    def _load_and_norm():
        _load_norm_rows(x_hbm, xbuf, gain_ref, xn_ref, sem, pl.multiple_of(i * tm, tm), tm)
        row = lax.broadcasted_iota(jnp.int32, wgate_ref.shape, 0)
        wgate = jnp.where(row < n_gates, wgate_ref[...], jnp.zeros_like(wgate_ref))
        gates_ref[...] = _dot_nt(xn_ref[...], wgate)

    proj_ref[...] = _dot_nt(xn_ref[...], w_ref[...])


def _in_proj(x, gain, w_in_t, n_main, layer, *, tn=512):
    m, d = x.shape
    n_gates = w_in_t.shape[1] - n_main
    assert n_main % tn == 0 and n_main % LANES == 0 and 0 < n_gates < LANES
    tm = _row_tile(m)
    return pl.pallas_call(
        functools.partial(_in_proj_body, tm=tm, n_gates=n_gates),
        out_shape=(jax.ShapeDtypeStruct((m, n_main), F32), jax.ShapeDtypeStruct((m, LANES), F32)),
        grid=(m // tm, n_main // tn),
        in_specs=[
            pl.BlockSpec(memory_space=pl.ANY),
            pl.BlockSpec((None, 1, d), lambda i, j: (layer, 0, 0)),
            pl.BlockSpec((None, tn, d), lambda i, j: (layer, j, 0)),
            pl.BlockSpec((None, LANES, d), lambda i, j: (layer, n_main // LANES, 0)),
        ],
        out_specs=(pl.BlockSpec((tm, tn), lambda i, j: (i, j)),
                   pl.BlockSpec((tm, LANES), lambda i, j: (i, 0))),
        scratch_shapes=[pltpu.VMEM((tm, d), F32), pltpu.VMEM((tm, d), BF16),
                        pltpu.SemaphoreType.DMA((LOAD_CHUNKS,))],
        compiler_params=_params(2),
        name="in_proj",
    )(x, gain.reshape(gain.shape[0], 1, d), w_in_t, w_in_t)


def _mlstm_body(bi_ref, bf_ref, q_ref, k_ref, v_ref, o_ref, g_ref, gmh_ref, c0_ref, n0_ref, m0_ref,
                y_ref, c_ref, n_ref, m_ref, *, chunk, valid, heads):
    hg = pl.program_id(1)
    c = pl.program_id(2)
    L = chunk
    dk = q_ref.shape[1] // heads
    dv = v_ref.shape[1] // heads

    @pl.when(c == 0)
    def _init():
        c_ref[...] = c0_ref[...]
        n_ref[...] = n0_ref[...]
        m_ref[...] = m0_ref[...]

    t_idx = lax.broadcasted_iota(jnp.int32, (L, L), 0)
    s_idx = lax.broadcasted_iota(jnp.int32, (L, L), 1)
    causal = s_idx <= t_idx
    diag = s_idx == t_idx
    row = lax.broadcasted_iota(jnp.int32, (L, 1), 0)

    for e in range(heads):
        h = hg * heads + e
        qs = slice(e * dk, (e + 1) * dk)
        vs = slice(e * dv, (e + 1) * dv)
        gi_row = g_ref[e, 0:1, :] + bi_ref[h]
        lf_row = jax.nn.log_sigmoid(g_ref[e, 1:2, :] + bf_ref[h])
        lf_b = jnp.broadcast_to(lf_row, (L, L))
        gi_b = jnp.broadcast_to(gi_row, (L, L))
        lf_col = jnp.sum(jnp.where(diag, lf_b, 0.0), axis=1, keepdims=True)
        gi_col = jnp.sum(jnp.where(diag, gi_b, 0.0), axis=1, keepdims=True)
        b_col = jnp.sum(jnp.where(causal, lf_b, 0.0), axis=1, keepdims=True)
        b_row = jnp.sum(jnp.where(t_idx <= s_idx, jnp.broadcast_to(lf_col, (L, L)), 0.0),
                        axis=0, keepdims=True)

        log_d = jnp.where(causal, b_col + (gi_row - b_row), -jnp.inf)
        m_prev = m_ref[e, 0:1, 0:1]
        inter = b_col + m_prev
        m_t = jnp.maximum(inter, jnp.max(log_d, axis=1, keepdims=True))
        decay = jnp.exp(log_d - m_t)

        qf = q_ref[:, qs]
        kf = k_ref[:, qs] * (dk ** -0.5)
        qb = qf.astype(BF16)
        kb = kf.astype(BF16)
        vb = v_ref[:, vs].astype(BF16)
        s = _dot_nt(qb, kb) * decay
        sc = jnp.exp(inter - m_t)
        num = jnp.dot(s.astype(BF16), vb, preferred_element_type=F32) \
            + sc * jnp.dot(qb, c_ref[e].astype(BF16), preferred_element_type=F32)
        den = jnp.sum(s, axis=1, keepdims=True) + sc * jnp.sum(qf * n_ref[e], axis=1, keepdims=True)
        hc = num / jnp.maximum(jnp.abs(den), jnp.exp(-m_t))
        hn = hc * lax.rsqrt(jnp.mean(hc * hc, axis=1, keepdims=True) + EPS) * gmh_ref[e]
        y_ref[:, vs] = (jax.nn.sigmoid(o_ref[:, vs]) * hn).astype(y_ref.dtype)

        m_new = m_t[valid - 1:valid, :]
        b_last = b_col[valid - 1:valid, :]
        dec = jnp.exp(b_last + m_prev - m_new)
        ws = jnp.where(row < valid, jnp.exp(b_last - b_col + gi_col - m_new), 0.0)
        kw = ws * kf
        c_ref[e] = dec * c_ref[e] + lax.dot_general(kw.astype(BF16), vb, (((0,), (0,)), ((), ())),
                                                    preferred_element_type=F32)
        n_ref[e] = dec * n_ref[e] + jnp.sum(kw, axis=0, keepdims=True)
        m_ref[e] = jnp.broadcast_to(m_new, m_ref.shape[1:])


def _mlstm(proj, gates_row, b_i, b_f, g_mh, c0, n0, m0, *, chunk, valid):
    b, t, _ = proj.shape
    nh, dk, dv = c0.shape[1], c0.shape[2], c0.shape[3]
    hps = MLSTM_HEADS_PER_STEP
    assert nh % hps == 0 and (2 * nh * dk) % (hps * dv) == 0
    k_blk0 = nh // hps
    v_blk0 = 2 * nh * dk // (hps * dv)
    o_blk0 = v_blk0 + nh // hps
    smem = pl.BlockSpec(memory_space=pltpu.SMEM)
    state = lambda r, cdim: pl.BlockSpec((None, hps, r, cdim), lambda bi, hi, ci: (bi, hi, 0, 0))
    return pl.pallas_call(
        functools.partial(_mlstm_body, chunk=chunk, valid=valid, heads=hps),
        out_shape=(jax.ShapeDtypeStruct((b, t, nh * dv), BF16),
                   jax.ShapeDtypeStruct(c0.shape, F32),
                   jax.ShapeDtypeStruct(n0.shape, F32),
                   jax.ShapeDtypeStruct(m0.shape, F32)),
        grid=(b, nh // hps, t // chunk),
        in_specs=[
            smem, smem,
            pl.BlockSpec((None, chunk, hps * dk), lambda bi, hi, ci: (bi, ci, hi)),
            pl.BlockSpec((None, chunk, hps * dk), lambda bi, hi, ci: (bi, ci, k_blk0 + hi)),
            pl.BlockSpec((None, chunk, hps * dv), lambda bi, hi, ci: (bi, ci, v_blk0 + hi)),
            pl.BlockSpec((None, chunk, hps * dv), lambda bi, hi, ci: (bi, ci, o_blk0 + hi)),
            pl.BlockSpec((None, hps, 2, chunk), lambda bi, hi, ci: (bi, hi, 0, ci)),
            pl.BlockSpec((hps, 1, dv), lambda bi, hi, ci: (hi, 0, 0)),
            state(dk, dv), state(1, dk), state(1, LANES),
        ],
        out_specs=(pl.BlockSpec((None, chunk, hps * dv), lambda bi, hi, ci: (bi, ci, hi)),
                   state(dk, dv), state(1, dk), state(1, LANES)),
        compiler_params=_params(3),
        name="mlstm",
    )(b_i, b_f, proj, proj, proj, proj, gates_row, g_mh.reshape(nh, 1, dv), c0, n0, m0)


def _matmul_resid_body(a_ref, w_ref, r_ref, o_ref):
    o_ref[...] = r_ref[...] + jnp.dot(a_ref[...], w_ref[...], preferred_element_type=F32)


def _matmul_resid(a, w, resid, layer, *, tn=512):
    m, k = a.shape
    n = w.shape[2]
    tm = _row_tile(m)
    return pl.pallas_call(
        _matmul_resid_body,
        out_shape=jax.ShapeDtypeStruct((m, n), F32),
        grid=(m // tm, n // tn),
        in_specs=[pl.BlockSpec((tm, k), lambda i, j: (i, 0)),
                  pl.BlockSpec((None, k, tn), lambda i, j: (layer, 0, j)),
                  pl.BlockSpec((tm, tn), lambda i, j: (i, j))],
        out_specs=pl.BlockSpec((tm, tn), lambda i, j: (i, j)),
        compiler_params=_params(2),
        name="matmul_resid",
    )(a, w, resid)


def _rope_table_body(invf_ref, cos_ref, sin_ref, *, seq, base):
    rows = cos_ref.shape[0]
    m = lax.broadcasted_iota(jnp.int32, (rows, LANES), 0) + pl.program_id(0) * rows
    pos = (base + (m & (seq - 1))).astype(F32)
    ang = pos * invf_ref[...]
    cos_ref[...] = jnp.cos(ang)
    sin_ref[...] = jnp.sin(ang)


def _rope_tables(m, seq, base):
    assert seq & (seq - 1) == 0
    half = ROT_DIM // 2
    inv_freq = ROPE_THETA ** (-jnp.arange(half, dtype=F32) / half)
    invf = jnp.zeros((1, LANES), F32).at[0, :ROT_DIM].set(jnp.tile(inv_freq, 2))
    rows = min(m, 512)
    spec = pl.BlockSpec((rows, LANES), lambda i: (i, 0))
    return pl.pallas_call(
        functools.partial(_rope_table_body, seq=seq, base=base),
        out_shape=(jax.ShapeDtypeStruct((m, LANES), F32),) * 2,
        grid=(m // rows,),
        in_specs=[pl.BlockSpec((1, LANES), lambda i: (0, 0))],
        out_specs=(spec, spec),
        compiler_params=_params(1),
        name="rope_tables",
    )(invf)


def _to_head_slabs(res_ref, x):
    for hh in range(res_ref.shape[0]):
        res_ref[hh] = x[:, hh * ATT_HEAD_DIM:(hh + 1) * ATT_HEAD_DIM]


def _norm_rope_to_slabs(nat_ref, col0, res_ref, gain_ref, cos_ref, sin_ref):
    n_heads, rows, _ = res_ref.shape
    step = min(rows, EPI_ROWS)
    for r in range(rows // step):
        rs = slice(r * step, (r + 1) * step)
        for hh in range(n_heads):
            hs = slice(col0 + hh * ATT_HEAD_DIM, col0 + (hh + 1) * ATT_HEAD_DIM)
            res_ref[hh, rs, :] = _head_norm(nat_ref[rs, hs], gain_ref[...])
    for r in range(rows // step):
        rs = slice(r * step, (r + 1) * step)
        cosf = cos_ref[rs, :]
        sinf = sin_ref[rs, :]
        for hh in range(n_heads):
            res_ref[hh, rs, :] = _head_rope(res_ref[hh, rs, :], cosf, sinf)


def _emit_slabs(res_ref, out_ref, dil):
    n_heads, rows, _ = res_ref.shape
    n = n_heads * ATT_HEAD_DIM
    for r in range(dil):
        for hh in range(n_heads):
            col = r * n + hh * ATT_HEAD_DIM
            src = res_ref[hh] if dil == 1 else res_ref[hh, pl.ds(r, rows // dil, stride=dil), :]
            out_ref[:, col:col + ATT_HEAD_DIM] = src.astype(out_ref.dtype)


def _emit_heads(res_ref, out_ref):
    n_heads, rows, _ = res_ref.shape
    for hh in range(n_heads):
        out_ref[pl.ds(hh, rows, stride=n_heads), :] = res_ref[hh]


def _class_spec(tm, n, seq, dil):
    tiles = seq // tm
    assert seq % tm == 0 and tm % (dil * BF16_SUBLANES) == 0
    return pl.BlockSpec((None, tm // dil, dil * n), lambda i, *_: (i // tiles, i % tiles, 0))


def _kv_proj_body(*refs, tm, dilations):
    n_cls = len(dilations)
    x_hbm, gain_ref, w_ref, kgain_ref, cos_ref, sin_ref, k_ref, v_ref = refs[:8]
    kc_refs = refs[8:8 + n_cls]
    vc_refs = refs[8 + n_cls:8 + 2 * n_cls]
    xbuf, xn_ref, res_ref, sem = refs[8 + 2 * n_cls:]
    i = pl.program_id(0)
    j = pl.program_id(1)

    n = w_ref.shape[1] // 2

    @pl.when(j == 0)
    def _k_half():
        _load_norm_rows(x_hbm, xbuf, gain_ref, xn_ref, sem, pl.multiple_of(i * tm, tm), tm)
        xbuf[:, 0:n] = jnp.dot(xn_ref[...], w_ref[:, 0:n], preferred_element_type=F32)

    @pl.when(j == 1)
    def _v_half():
        v = jnp.dot(xn_ref[...], w_ref[:, n:2 * n], preferred_element_type=F32)
        _norm_rope_to_slabs(xbuf, 0, res_ref, kgain_ref, cos_ref, sin_ref)
        _emit_heads(res_ref, k_ref)
        for dil, ref in zip(dilations, kc_refs):
            _emit_slabs(res_ref, ref, dil)
        _to_head_slabs(res_ref, v)
        _emit_heads(res_ref, v_ref)
        for dil, ref in zip(dilations, vc_refs):
            _emit_slabs(res_ref, ref, dil)


def _kv_proj(x, gain, w_kv, g_knorm, cosf, sinf, seq, dilations=()):
    m, d = x.shape
    n = w_kv.shape[2] // 2
    n_heads = n // ATT_HEAD_DIM
    tm = min(m, PROJ_ROW_TILE)
    out_spec = pl.BlockSpec((tm * n_heads, ATT_HEAD_DIM), lambda i, j: (i, 0))
    tab_spec = pl.BlockSpec((tm, LANES), lambda i, j: (i, 0))
    cls_specs = [_class_spec(tm, n, seq, dil) for dil in dilations]
    cls_shapes = [jax.ShapeDtypeStruct((m // seq, seq // dil, dil * n), BF16) for dil in dilations]
    outs = pl.pallas_call(
        functools.partial(_kv_proj_body, tm=tm, dilations=tuple(dilations)),
        out_shape=[jax.ShapeDtypeStruct((m * n_heads, ATT_HEAD_DIM), F32)] * 2 + cls_shapes * 2,
        grid=(m // tm, 2),
        in_specs=[pl.BlockSpec(memory_space=pl.ANY),
                  pl.BlockSpec((1, d), lambda i, j: (0, 0)),
                  pl.BlockSpec((None, d, 2 * n), lambda i, j: (0, 0, 0), pipeline_mode=pl.Buffered(1)),
                  pl.BlockSpec((1, ATT_HEAD_DIM), lambda i, j: (0, 0)),
                  tab_spec, tab_spec],
        out_specs=[out_spec, out_spec] + cls_specs * 2,
        scratch_shapes=[pltpu.VMEM((tm, d), F32), pltpu.VMEM((tm, d), BF16),
                        pltpu.VMEM((n_heads, tm, ATT_HEAD_DIM), F32),
                        pltpu.SemaphoreType.DMA((LOAD_CHUNKS,))],
        compiler_params=_params(2),
        name="kv_proj",
    )(x, gain.reshape(1, d), w_kv, g_knorm.reshape(1, ATT_HEAD_DIM), cosf, sinf)
    n_cls = len(dilations)
    return outs[0], outs[1], tuple(outs[2:2 + n_cls]), tuple(outs[2 + n_cls:])


def _q_proj_body(*refs, tm, n_groups, dilations):
    x_hbm, gain_ref, w_ref, qgain_ref, cos_ref, sin_ref = refs[:6]
    n_out = max(len(dilations), 1)
    q_refs = refs[6:6 + n_out]
    xbuf, xn_ref, res_ref, sem = refs[6 + n_out:]
    i = pl.program_id(0)
    j = pl.program_id(1)

    tn = w_ref.shape[1] // n_groups
    for g in range(n_groups + 1):
        @pl.when(j == g)
        def _step(g=g):
            if g == 0:
                _load_norm_rows(x_hbm, xbuf, gain_ref, xn_ref, sem, pl.multiple_of(i * tm, tm), tm)
            if g < n_groups:
                proj = jnp.dot(xn_ref[...], w_ref[:, g * tn:(g + 1) * tn], preferred_element_type=F32)
            if g > 0:
                _norm_rope_to_slabs(xbuf, ((g - 1) % 2) * tn, res_ref, qgain_ref, cos_ref, sin_ref)
                if dilations:
                    _emit_slabs(res_ref, q_refs[g - 1], dilations[g - 1])
                else:
                    _emit_slabs(res_ref, q_refs[0], 1)
            if g < n_groups:
                xbuf[:, (g % 2) * tn:(g % 2 + 1) * tn] = proj


def _q_proj(x, gain, w_q, g_qnorm, cosf, sinf, layer_mix, layer_b, seq, dilations=()):
    m, d = x.shape
    n_groups = g_qnorm.shape[1]
    tn = w_q.shape[2] // n_groups
    tm = min(m, PROJ_ROW_TILE)
    tab_spec = pl.BlockSpec((tm, LANES), lambda i, j: (i, 0))
    if dilations:
        assert len(dilations) == n_groups
        out_shape = [jax.ShapeDtypeStruct((m // seq, seq // dil, dil * tn), BF16) for dil in dilations]
        out_specs = [_class_spec(tm, tn, seq, dil) for dil in dilations]
    else:
        out_shape = [jax.ShapeDtypeStruct((m, n_groups * tn), BF16)]
        out_specs = [pl.BlockSpec((tm, tn), lambda i, j: (i, jnp.maximum(j - 1, 0)))]
    assert d >= 2 * tn
    return pl.pallas_call(
        functools.partial(_q_proj_body, tm=tm, n_groups=n_groups, dilations=tuple(dilations)),
        out_shape=out_shape,
        grid=(m // tm, n_groups + 1),
        in_specs=[pl.BlockSpec(memory_space=pl.ANY),
                  pl.BlockSpec((None, 1, d), lambda i, j: (layer_mix, 0, 0)),
                  pl.BlockSpec((None, d, n_groups * tn), lambda i, j: (layer_b, 0, 0),
                               pipeline_mode=pl.Buffered(1)),
                  pl.BlockSpec((None, None, 1, ATT_HEAD_DIM), lambda i, j: (layer_b, jnp.maximum(j - 1, 0), 0, 0)),
                  tab_spec, tab_spec],
        out_specs=out_specs,
        scratch_shapes=[pltpu.VMEM((tm, d), F32), pltpu.VMEM((tm, d), BF16),
                        pltpu.VMEM((tn // ATT_HEAD_DIM, tm, ATT_HEAD_DIM), F32),
                        pltpu.SemaphoreType.DMA((LOAD_CHUNKS,))],
        compiler_params=_params(2),
        name="q_proj",
    )(x, gain.reshape(gain.shape[0], 1, d), w_q,
      g_qnorm.reshape(g_qnorm.shape[0], n_groups, 1, ATT_HEAD_DIM), cosf, sinf)


def _softmax_pieces(scores, masks):
    masked = [jnp.where(mk, sc, -jnp.inf) for sc, mk in zip(scores, masks)]
    m = functools.reduce(jnp.maximum, [jnp.max(sc, axis=1, keepdims=True) for sc in masked])
    probs = [jnp.exp(sc - m) for sc in masked]
    l = functools.reduce(jnp.add, [jnp.sum(p, axis=1, keepdims=True) for p in probs])
    return probs, l, m


def _attn_prompt_body(*refs, tq, win, dil, has_prev, n_heads):
    if has_prev:
        q_ref, kc_ref, vc_ref, kp_ref, vp_ref, o_ref, lse_ref = refs
    else:
        q_ref, kc_ref, vc_ref, o_ref, lse_ref = refs
    qi = pl.program_id(1)
    res = pl.program_id(2)
    scale = ATT_HEAD_DIM ** -0.5
    row = lax.broadcasted_iota(jnp.int32, (tq, tq), 0)
    col = lax.broadcasted_iota(jnp.int32, (tq, tq), 1)
    mask_c = (col <= row) & (row - col <= win)
    if has_prev:
        rowp = lax.broadcasted_iota(jnp.int32, (tq, win), 0)
        colp = lax.broadcasted_iota(jnp.int32, (tq, win), 1)
        mask_p = (colp >= rowp) & (qi > 0)
    rows = slice(None) if dil == 1 else pl.ds(res, tq, stride=dil)
    for hh in range(n_heads):
        hs = slice(hh * ATT_HEAD_DIM, (hh + 1) * ATT_HEAD_DIM)
        qh = q_ref[:, hs]
        scores = [_dot_nt(qh, _as_bf16(kc_ref[:, hs])) * scale]
        masks = [mask_c]
        values = [_as_bf16(vc_ref[:, hs])]
        if has_prev:
            scores.append(_dot_nt(qh, _as_bf16(kp_ref[:, hs])) * scale)
            masks.append(mask_p)
            values.append(_as_bf16(vp_ref[:, hs]))
        probs, l, m = _softmax_pieces(scores, masks)
        acc = functools.reduce(jnp.add, [jnp.dot(p.astype(BF16), v, preferred_element_type=F32)
                                         for p, v in zip(probs, values)])
        o_ref[hh, rows, :] = acc / l
        lse_ref[hh, rows, :] = jnp.broadcast_to(m + jnp.log(l), (tq, ATT_HEAD_DIM))


def _attn_prompt_group(q, k, v, seq, window, dil):
    b = q.shape[0]
    n = q.shape[2] // dil
    n_heads = n // ATT_HEAD_DIM
    tc = seq // dil
    win = window // dil
    tq = min(tc, ATTN_Q_TILE)
    has_prev = tc > tq
    assert tq % win == 0 and tc % tq == 0
    cur = pl.BlockSpec((None, tq, n), lambda bi, qi, r: (bi, qi, r))
    prev = pl.BlockSpec((None, win, n), lambda bi, qi, r: (bi, jnp.maximum(qi * (tq // win) - 1, 0), r))
    out = pl.BlockSpec((None, n_heads, tq * dil, ATT_HEAD_DIM), lambda bi, qi, r: (bi, 0, qi, 0))
    return pl.pallas_call(
        functools.partial(_attn_prompt_body, tq=tq, win=win, dil=dil, has_prev=has_prev, n_heads=n_heads),
        out_shape=(jax.ShapeDtypeStruct((b, n_heads, seq, ATT_HEAD_DIM), F32),) * 2,
        grid=(b, tc // tq, dil),
        in_specs=[cur, cur, cur] + ([prev, prev] if has_prev else []),
        out_specs=(out, out),
        compiler_params=_params(3),
        name=f"attn_prompt_d{dil}",
    )(q, k, v, *([k, v] if has_prev else []))


def _attn_out_body(*refs, n_groups):
    o_refs = refs[:n_groups]
    l_refs = refs[n_groups:2 * n_groups]
    w_ref, r_ref, out_ref, att_ref = refs[2 * n_groups:]

    @pl.when(pl.program_id(1) == 0)
    def _mix():
        for hh in range(o_refs[0].shape[0]):
            lses = [l[hh] for l in l_refs]
            m = functools.reduce(jnp.maximum, lses)
            es = [jnp.exp(l - m) for l in lses]
            tot = functools.reduce(jnp.add, es)
            att = functools.reduce(jnp.add, [(e / tot) * o[hh] for e, o in zip(es, o_refs)])
            att_ref[:, hh * ATT_HEAD_DIM:(hh + 1) * ATT_HEAD_DIM] = att.astype(BF16)

    out_ref[...] = r_ref[...] + jnp.dot(att_ref[...], w_ref[...], preferred_element_type=F32)


def _attn_out(outs, lses, w_o, resid, layer, *, tm=256):
    b, n_heads, seq, hd = outs[0].shape
    m, n = resid.shape
    k = n_heads * hd
    tn = n
    tm = min(seq, tm)
    tiles = seq // tm
    blk = pl.BlockSpec((None, n_heads, tm, hd), lambda i, j: (i // tiles, 0, i % tiles, 0))
    return pl.pallas_call(
        functools.partial(_attn_out_body, n_groups=len(outs)),
        out_shape=jax.ShapeDtypeStruct((m, n), F32),
        grid=(m // tm, n // tn),
        in_specs=[blk] * (2 * len(outs)) + [pl.BlockSpec((None, k, tn), lambda i, j: (layer, 0, j)),
                                            pl.BlockSpec((tm, tn), lambda i, j: (i, j))],
        out_specs=pl.BlockSpec((tm, tn), lambda i, j: (i, j)),
        scratch_shapes=[pltpu.VMEM((tm, k), BF16)],
        compiler_params=_params(2),
        name="attn_out",
    )(*outs, *lses, w_o, resid)


def _attn_sample_body(q_ref, kc_ref, vc_ref, kn_ref, vn_ref, o_ref, *, groups, n_heads, n_new):
    rows = q_ref.shape[0]
    cache = kc_ref.shape[0] // n_heads
    n = n_heads * ATT_HEAD_DIM
    scale = ATT_HEAD_DIM ** -0.5
    dist_c = cache + lax.broadcasted_iota(jnp.int32, (rows, cache), 0) \
        - lax.broadcasted_iota(jnp.int32, (rows, cache), 1)
    dist_n = lax.broadcasted_iota(jnp.int32, (rows, rows), 0) - lax.broadcasted_iota(jnp.int32, (rows, rows), 1)
    col_n = lax.broadcasted_iota(jnp.int32, (rows, rows), 1)
    for hh in range(n_heads):
        hs = slice(hh * ATT_HEAD_DIM, (hh + 1) * ATT_HEAD_DIM)
        kc = kc_ref[pl.ds(hh, cache, stride=n_heads), :].astype(BF16)
        vc = vc_ref[pl.ds(hh, cache, stride=n_heads), :].astype(BF16)
        kn = kn_ref[:, hs].astype(BF16)
        vn = vn_ref[:, hs].astype(BF16)
        outs, lses = [], []
        for g, (window, dil) in enumerate(groups):
            qh = q_ref[:, g * n + hh * ATT_HEAD_DIM:g * n + (hh + 1) * ATT_HEAD_DIM]
            mask_c = (dist_c <= window) & ((dist_c & (dil - 1)) == 0)
            mask_n = (dist_n >= 0) & (col_n < n_new) & ((dist_n & (dil - 1)) == 0)
            probs, l, m = _softmax_pieces([_dot_nt(qh, kc) * scale, _dot_nt(qh, kn) * scale], [mask_c, mask_n])
            acc = jnp.dot(probs[0].astype(BF16), vc, preferred_element_type=F32) \
                + jnp.dot(probs[1].astype(BF16), vn, preferred_element_type=F32)
            outs.append(acc / l)
            lses.append(m + jnp.log(l))
        m_all = functools.reduce(jnp.maximum, lses)
        es = [jnp.exp(l - m_all) for l in lses]
        tot = functools.reduce(jnp.add, es)
        o_ref[:, hs] = functools.reduce(jnp.add, [(e / tot) * o for e, o in zip(es, outs)]).astype(o_ref.dtype)


def _attn_sample(q, k_new, v_new, cache_k, cache_v, n_new):
    b, rows, n = k_new.shape
    cache, n_heads = cache_k.shape[1], cache_k.shape[2]
    assert cache == MAX_WINDOW and all(d & (d - 1) == 0 for _, d in DILATED_GROUPS)
    row_blk = lambda width: pl.BlockSpec((None, rows, width), lambda bi: (bi, 0, 0))
    cache_blk = pl.BlockSpec((None, cache * n_heads, ATT_HEAD_DIM), lambda bi: (bi, 0, 0))
    cache_k = cache_k.reshape(b, cache * n_heads, ATT_HEAD_DIM)
    cache_v = cache_v.reshape(b, cache * n_heads, ATT_HEAD_DIM)
    return pl.pallas_call(
        functools.partial(_attn_sample_body, groups=DILATED_GROUPS, n_heads=n // ATT_HEAD_DIM, n_new=n_new),
        out_shape=jax.ShapeDtypeStruct((b, rows, n), BF16),
        grid=(b,),
        in_specs=[row_blk(q.shape[2]), cache_blk, cache_blk, row_blk(n), row_blk(n)],
        out_specs=row_blk(n),
        compiler_params=_params(1),
        name="attn_sample",
    )(q, cache_k, cache_v, k_new, v_new)


FFN_ORDER = (("ffn1", 0), ("ffn2", 0), ("ffn1", 1), ("ffn2", 1))
SMALL_WEIGHTS = (("w_in_a_t", True), ("w_out_a", False), ("w_kv", False), ("w_q_b", False), ("w_o_b", False))


def _ffn_stacks(p, step):
    name, _ = FFN_ORDER[step]
    return tuple(p[f"w_{name}_{part}"] for part in ("gate", "up", "down"))


def _run_ffn(x, p, bf16, step):
    name, layer = FFN_ORDER[step]
    if ("ffn", step) not in bf16:
        x, bf16[("ffn", step)], _ = _ffn(x, p[f"norm_{name}"], layer, _ffn_stacks(p, step), layer, emit=True)
        return x
    casts, keys = [], []
    if step + 1 < len(FFN_ORDER) and ("ffn", step + 1) not in bf16:
        gate, up, down = _ffn_stacks(p, step + 1)
        nxt_layer = FFN_ORDER[step + 1][1]
        casts += [(gate, nxt_layer, False), (up, nxt_layer, False), (down, nxt_layer, True)]
        keys += [("ffn", step + 1)] * 3
    for key, rows_follow_j in SMALL_WEIGHTS:
        if key not in bf16:
            assert p[key].shape[0] == 1
            casts.append((p[key], 0, rows_follow_j))
            keys.append(key)
    x, _, cast = _ffn(x, p[f"norm_{name}"], layer, bf16[("ffn", step)], casts=tuple(casts))
    for key, arr in zip(keys, cast):
        if key[0] == "ffn":
            bf16[key] = bf16.get(key, ()) + (arr,)
        else:
            bf16[key] = arr
    return x


def _trunk(x, seq, base, cache_k, cache_v, c0, n0, m0, p, bf16, first_ffn_done=False):
    m_rows, d = x.shape
    b = m_rows // seq
    nh = MLSTM_HEADS
    dk, dv = c0.shape[2], c0.shape[3]
    fresh = cache_k is None
    dilations = tuple(dil for _, dil in DILATED_GROUPS)

    if not first_ffn_done:
        x = _run_ffn(x, p, bf16, 0)
    proj, gates = _in_proj(x, p["norm_mix"], bf16["w_in_a_t"], 2 * nh * dk + nh * dv + d, 0)
    if seq % MLSTM_PROMPT_CHUNK == 0:
        chunk, valid, t_pad = MLSTM_PROMPT_CHUNK, MLSTM_PROMPT_CHUNK, seq
    else:
        assert seq <= BF16_SUBLANES
        chunk, valid, t_pad = BF16_SUBLANES, seq, BF16_SUBLANES
    proj3 = proj.reshape(b, seq, -1)
    gates3 = gates.reshape(b, seq, LANES)[:, :, :2 * nh]
    if t_pad != seq:
        proj3 = jnp.pad(proj3, ((0, 0), (0, t_pad - seq), (0, 0)))
        gates3 = jnp.pad(gates3, ((0, 0), (0, t_pad - seq), (0, 0)))
    gates_row = gates3.reshape(b, t_pad, 2, nh).transpose(0, 3, 2, 1)
    y, c_fin, n_fin, m_fin = _mlstm(proj3, gates_row, p["b_ig"][0], p["b_fg"][0], p["g_mh"][0],
                                    c0, n0, m0, chunk=chunk, valid=valid)
    y = y[:, :seq].reshape(m_rows, nh * dv)
    x = _matmul_resid(y, bf16["w_out_a"], x, 0)
    x = _run_ffn(x, p, bf16, 1)

    cosf, sinf = _rope_tables(m_rows, seq, base)
    cls_dils = dilations if fresh else ()
    k, v, k_cls, v_cls = _kv_proj(x, p["norm_kv"], bf16["w_kv"], p["g_knorm"], cosf, sinf, seq, cls_dils)

    x = _run_ffn(x, p, bf16, 2)
    q = _q_proj(x, p["norm_mix"], bf16["w_q_b"], p["g_qnorm"], cosf, sinf, 1, 0, seq, dilations if fresh else ())
    n_kv = k.shape[0] // m_rows * k.shape[1]
    if fresh:
        outs, lses = [], []
        for q_g, k_g, v_g, (window, dil) in zip(q, k_cls, v_cls, DILATED_GROUPS):
            o_g, l_g = _attn_prompt_group(q_g, k_g, v_g, seq, window, dil)
            outs.append(o_g)
            lses.append(l_g)
        x = _attn_out(outs, lses, bf16["w_o_b"], x, 0)
    else:
        pad = ((0, 0), (0, BF16_SUBLANES - seq), (0, 0))
        att = _attn_sample(jnp.pad(q[0].reshape(b, seq, -1), pad), jnp.pad(k.reshape(b, seq, n_kv), pad),
                           jnp.pad(v.reshape(b, seq, n_kv), pad), cache_k, cache_v, seq)
        x = _matmul_resid(att[:, :seq].reshape(m_rows, n_kv), bf16["w_o_b"], x, 0)
    x = _run_ffn(x, p, bf16, 3)
    return x, k, v, c_fin, n_fin, m_fin


def kernel(x_prompt, x_sample, cache_k, cache_v, state_C, state_n, state_m, norm_ffn1, w_ffn1_gate, w_ffn1_up, w_ffn1_down, norm_mix, norm_ffn2, w_ffn2_gate, w_ffn2_up, w_ffn2_down, w_in_a, b_ig, b_fg, g_mh, w_out_a, norm_kv, w_kv, g_knorm, w_q_b, g_qnorm, w_o_b):
    bp, seq_p, d = x_prompt.shape
    bs, seq_s, _ = x_sample.shape
    past = cache_k.shape[1]
    assert past == min(MAX_WINDOW, PAST_LEN)
    nh = MLSTM_HEADS
    dk, dv = state_C.shape[3], state_C.shape[4]

    p = dict(norm_ffn1=norm_ffn1, norm_mix=norm_mix, norm_ffn2=norm_ffn2, norm_kv=norm_kv,
             w_ffn1_gate=w_ffn1_gate, w_ffn1_up=w_ffn1_up, w_ffn1_down=w_ffn1_down,
             w_ffn2_gate=w_ffn2_gate, w_ffn2_up=w_ffn2_up, w_ffn2_down=w_ffn2_down,
             w_in_a_t=jnp.swapaxes(w_in_a, 1, 2), b_ig=b_ig, b_fg=b_fg, g_mh=g_mh, w_out_a=w_out_a, w_kv=w_kv[None],
             g_knorm=g_knorm, w_q_b=w_q_b, g_qnorm=g_qnorm, w_o_b=w_o_b)
    bf16 = {}
    xs = _run_ffn(x_sample.reshape(bs * seq_s, d), p, bf16, 0)

    zeros = lambda *s: jnp.zeros(s, F32)
    y_p, k_p, v_p, c_p, n_p, m_p = _trunk(
        x_prompt.reshape(bp * seq_p, d), seq_p, 0, None, None,
        zeros(bp, nh, dk, dv), zeros(bp, nh, 1, dk), zeros(bp, nh, 1, LANES), p, bf16)
    y_s, k_s, v_s, c_s, n_s, m_s = _trunk(
        xs, seq_s, PAST_LEN, cache_k, cache_v,
        state_C[:, 0], state_n[:, 0][:, :, None, :],
        jnp.broadcast_to(state_m[:, 0][:, :, None, None], (bs, nh, 1, LANES)), p, bf16, first_ffn_done=True)

    hkv, hd = cache_k.shape[2], cache_k.shape[3]
    keep = min(MAX_WINDOW, seq_p)
    return (y_p.reshape(bp, seq_p, d), y_s.reshape(bs, seq_s, d),
            k_p.reshape(bp, seq_p, hkv, hd)[:, seq_p - keep:], v_p.reshape(bp, seq_p, hkv, hd)[:, seq_p - keep:],
            k_s.reshape(bs, seq_s, hkv, hd), v_s.reshape(bs, seq_s, hkv, hd),
            c_p[:, None], n_p[:, None, :, 0], m_p[:, None, :, 0, 0],
            c_s[:, None], n_s[:, None, :, 0], m_s[:, None, :, 0, 0])
```

```python
import functools

import jax
import jax.numpy as jnp
from jax import lax
from jax.experimental import pallas as pl
from jax.experimental.pallas import tpu as pltpu

F32 = jnp.float32
BF16 = jnp.bfloat16

EPS = 1e-6
MLSTM_HEADS = 8
ATT_HEAD_DIM = 128
DILATED_GROUPS = ((128, 1), (512, 4), (2048, 16))
MAX_WINDOW = max(w for w, _ in DILATED_GROUPS)
ROT_DIM = ATT_HEAD_DIM // 4
ROPE_THETA = 500000.0
PAST_LEN = 16384

V7X_VMEM_LIMIT_BYTES = 56 * 1024 * 1024
LANES = 128
BF16_SUBLANES = 16
NORM_ROWS = 16
NORM_UNROLL = 4
LOAD_CHUNKS = 8
EPI_ROWS = 64
PROMPT_ROW_TILE = 1024
PROJ_ROW_TILE = 512
MLSTM_PROMPT_CHUNK = 256
MLSTM_HEADS_PER_STEP = 4
ATTN_Q_TILE = 256


def _params(n_axes):
    return pltpu.CompilerParams(dimension_semantics=("arbitrary",) * n_axes,
                                vmem_limit_bytes=V7X_VMEM_LIMIT_BYTES)


def _row_tile(m):
    return min(m, PROMPT_ROW_TILE)


def _round_up(x, mult):
    return -(-x // mult) * mult


def _rmsnorm_rows(src_ref, gain_ref, dst_ref, rows):
    step = min(rows, NORM_ROWS)

    def body(r, carry):
        rs = pl.ds(pl.multiple_of(r * step, step), step)
        ms = jnp.mean(jnp.square(src_ref[rs, :]), axis=-1, keepdims=True)
        dst_ref[rs, :] = (src_ref[rs, :] * lax.rsqrt(ms + EPS) * gain_ref[...]).astype(BF16)
        return carry

    trips = rows // step
    lax.fori_loop(0, trips, body, 0, unroll=min(trips, NORM_UNROLL))


def _row_pieces(rows):
    n = LOAD_CHUNKS if rows % (LOAD_CHUNKS * NORM_ROWS) == 0 else 1
    return n, rows // n


def _load_norm_rows(x_hbm, buf_ref, gain_ref, xn_ref, sems, row0, rows, before_piece=None):
    n, piece = _row_pieces(rows)
    copies = [pltpu.make_async_copy(x_hbm.at[pl.ds(row0 + c * piece, piece)],
                                    buf_ref.at[pl.ds(c * piece, piece)], sems.at[c]) for c in range(n)]
    for c, cp in enumerate(copies):
        if before_piece is not None:
            before_piece(c)
        cp.start()
    for c, cp in enumerate(copies):
        cp.wait()
        _rmsnorm_rows(buf_ref.at[pl.ds(c * piece, piece)], gain_ref, xn_ref.at[pl.ds(c * piece, piece)], piece)


def _head_norm(xh, gain):
    return xh * lax.rsqrt(jnp.mean(xh * xh, axis=-1, keepdims=True) + EPS) * gain


def _head_rope(y, cosf, sinf):
    half = ROT_DIM // 2
    lane = lax.broadcasted_iota(jnp.int32, y.shape, 1)
    from_lower = pltpu.roll(y, half, axis=1)
    from_upper = pltpu.roll(y, LANES - half, axis=1)
    rot = jnp.where(lane < half, -from_upper, from_lower)
    return y * cosf + rot * sinf


def _as_bf16(x):
    return x if x.dtype == BF16 else x.astype(BF16)


def _dot_nt(a, b):
    return lax.dot_general(a, b, (((1,), (1,)), ((), ())), preferred_element_type=F32)


def _cast_specs(src, layer, grid, rows_follow_j):
    gi, gj = grid
    _, r, c = src.shape
    if rows_follow_j:
        rb, cb = _round_up(pl.cdiv(r, gj), BF16_SUBLANES), c // gi
        assert c % gi == 0 and cb % LANES == 0
        last = pl.cdiv(r, rb) - 1
        index = lambda lay: (lambda i, j: (lay, jnp.minimum(j, last), i))
    else:
        rb, cb = r // gi, _round_up(pl.cdiv(c, gj), LANES)
        assert r % gi == 0 and rb % BF16_SUBLANES == 0
        last = pl.cdiv(c, cb) - 1
        index = lambda lay: (lambda i, j: (lay, i, jnp.minimum(j, last)))
    return (pl.BlockSpec((None, rb, cb), index(layer)), pl.BlockSpec((None, rb, cb), index(0)),
            jax.ShapeDtypeStruct((1, r, c), BF16))


def _ffn_body(*refs, tm, n_chunk, n_casts, emit):
    x_hbm, gain_ref, wg_ref, wu_ref, wd_ref = refs[:5]
    cast_in = refs[5:5 + n_casts]
    o_hbm = refs[5 + n_casts]
    n_emit = 3 if emit else 0
    emit_refs = refs[6 + n_casts:6 + n_casts + n_emit]
    cast_out = refs[6 + n_casts + n_emit:6 + 2 * n_casts + n_emit]
    acc_ref, xn_ref, sem = refs[6 + 2 * n_casts + n_emit:]

    for src, dst in zip(cast_in, cast_out):
        dst[...] = src[...].astype(BF16)

    i = pl.program_id(0)
    j = pl.program_id(1)
    row0 = pl.multiple_of(i * tm, tm)
    n_pieces, piece = _row_pieces(tm)

    def store_piece(c, tile_row0):
        rows = pl.ds(c * piece, piece)
        return pltpu.make_async_copy(acc_ref.at[rows], o_hbm.at[pl.ds(tile_row0 + c * piece, piece)],
                                     sem.at[LOAD_CHUNKS + c])

    @pl.when(j == 0)
    def _load_and_norm():
        def previous_tile_stored(c):
            @pl.when(i > 0)
            def _():
                store_piece(c, row0 - tm).wait()

        _load_norm_rows(x_hbm, acc_ref, gain_ref, xn_ref, sem, row0, tm, before_piece=previous_tile_stored)

    wg, wu, wd = wg_ref[...], wu_ref[...], wd_ref
    if emit:
        wg, wu, wd = wg.astype(BF16), wu.astype(BF16), wd_ref[...].astype(BF16)
        emit_refs[0][...] = wg
        emit_refs[1][...] = wu
        emit_refs[2][...] = wd
    xn = xn_ref[...]
    g = jnp.dot(xn, wg, preferred_element_type=F32)
    u = jnp.dot(xn, wu, preferred_element_type=F32)
    a = (0.5 * (g * jax.nn.sigmoid(g)) * u).astype(BF16)
    d = acc_ref.shape[1]
    for c in range(d // n_chunk):
        cs = slice(c * n_chunk, (c + 1) * n_chunk)
        acc_ref[:, cs] += jnp.dot(a, wd[:, cs], preferred_element_type=F32)

    @pl.when(j == pl.num_programs(1) - 1)
    def _store():
        for c in range(n_pieces):
            store_piece(c, row0).start()

        @pl.when(i == pl.num_programs(0) - 1)
        def _():
            for c in range(n_pieces):
                store_piece(c, row0).wait()


def _ffn(x, gain, layer, weights, w_layer=0, casts=(), emit=False, *, tf=256):
    wg, wu, wd = weights
    m, d = x.shape
    f = wg.shape[2]
    tm = _row_tile(m)
    grid = (m // tm, f // tf)
    assert not emit or grid[0] == 1
    in_specs = [
        pl.BlockSpec(memory_space=pl.ANY),
        pl.BlockSpec((None, 1, d), lambda i, j: (layer, 0, 0)),
        pl.BlockSpec((None, d, tf), lambda i, j: (w_layer, 0, j)),
        pl.BlockSpec((None, d, tf), lambda i, j: (w_layer, 0, j)),
        pl.BlockSpec((None, tf, d), lambda i, j: (w_layer, j, 0)),
    ]
    out_shape = [jax.ShapeDtypeStruct((m, d), F32)]
    out_specs = [pl.BlockSpec(memory_space=pl.ANY)]
    if emit:
        out_shape += [jax.ShapeDtypeStruct((1, d, f), BF16)] * 2 + [jax.ShapeDtypeStruct((1, f, d), BF16)]
        out_specs += [pl.BlockSpec((None, d, tf), lambda i, j: (0, 0, j))] * 2 \
            + [pl.BlockSpec((None, tf, d), lambda i, j: (0, j, 0))]
    for src, src_layer, rows_follow_j in casts:
        in_spec, out_spec, shape = _cast_specs(src, src_layer, grid, rows_follow_j)
        in_specs.append(in_spec)
        out_specs.append(out_spec)
        out_shape.append(shape)
    outs = pl.pallas_call(
        functools.partial(_ffn_body, tm=tm, n_chunk=min(d, 512), n_casts=len(casts), emit=emit),
        out_shape=out_shape,
        grid=grid,
        in_specs=in_specs,
        out_specs=out_specs,
        scratch_shapes=[pltpu.VMEM((tm, d), F32), pltpu.VMEM((tm, d), BF16),
                        pltpu.SemaphoreType.DMA((2 * LOAD_CHUNKS,))],
        compiler_params=_params(2),
        name="ffn",
    )(x, gain.reshape(gain.shape[0], 1, d), wg, wu, wd, *[src for src, _, _ in casts])
    n_emit = 3 if emit else 0
    return outs[0], tuple(outs[1:1 + n_emit]), tuple(outs[1 + n_emit:])


def _in_proj_body(x_hbm, gain_ref, w_ref, wgate_ref, proj_ref, gates_ref, xbuf, xn_ref, sem, *, tm, n_gates):
    i = pl.program_id(0)
    j = pl.program_id(1)

    @pl.when(j == 0)
    def _load_and_norm():
        _load_norm_rows(x_hbm, xbuf, gain_ref, xn_ref, sem, pl.multiple_of(i * tm, tm), tm)
        row = lax.broadcasted_iota(jnp.int32, wgate_ref.shape, 0)
        wgate = jnp.where(row < n_gates, wgate_ref[...], jnp.zeros_like(wgate_ref))
        gates_ref[...] = _dot_nt(xn_ref[...], wgate)

    proj_ref[...] = _dot_nt(xn_ref[...], w_ref[...])


def _in_proj(x, gain, w_in_t, n_main, layer, *, tn=512):
    m, d = x.shape
    n_gates = w_in_t.shape[1] - n_main
    assert n_main % tn == 0 and n_main % LANES == 0 and 0 < n_gates < LANES
    tm = _row_tile(m)
    return pl.pallas_call(
        functools.partial(_in_proj_body, tm=tm, n_gates=n_gates),
        out_shape=(jax.ShapeDtypeStruct((m, n_main), F32), jax.ShapeDtypeStruct((m, LANES), F32)),
        grid=(m // tm, n_main // tn),
        in_specs=[
            pl.BlockSpec(memory_space=pl.ANY),
            pl.BlockSpec((None, 1, d), lambda i, j: (layer, 0, 0)),
            pl.BlockSpec((None, tn, d), lambda i, j: (layer, j, 0)),
            pl.BlockSpec((None, LANES, d), lambda i, j: (layer, n_main // LANES, 0)),
        ],
        out_specs=(pl.BlockSpec((tm, tn), lambda i, j: (i, j)),
                   pl.BlockSpec((tm, LANES), lambda i, j: (i, 0))),
        scratch_shapes=[pltpu.VMEM((tm, d), F32), pltpu.VMEM((tm, d), BF16),
                        pltpu.SemaphoreType.DMA((LOAD_CHUNKS,))],
        compiler_params=_params(2),
        name="in_proj",
    )(x, gain.reshape(gain.shape[0], 1, d), w_in_t, w_in_t)


def _mlstm_body(bi_ref, bf_ref, q_ref, k_ref, v_ref, o_ref, g_ref, gmh_ref, c0_ref, n0_ref, m0_ref,
                y_ref, c_ref, n_ref, m_ref, *, chunk, valid, heads):
    hg = pl.program_id(1)
    c = pl.program_id(2)
    L = chunk
    dk = q_ref.shape[1] // heads
    dv = v_ref.shape[1] // heads

    @pl.when(c == 0)
    def _init():
        c_ref[...] = c0_ref[...]
        n_ref[...] = n0_ref[...]
        m_ref[...] = m0_ref[...]

    t_idx = lax.broadcasted_iota(jnp.int32, (L, L), 0)
    s_idx = lax.broadcasted_iota(jnp.int32, (L, L), 1)
    causal = s_idx <= t_idx
    diag = s_idx == t_idx
    row = lax.broadcasted_iota(jnp.int32, (L, 1), 0)

    for e in range(heads):
        h = hg * heads + e
        qs = slice(e * dk, (e + 1) * dk)
        vs = slice(e * dv, (e + 1) * dv)
        gi_row = g_ref[e, 0:1, :] + bi_ref[h]
        lf_row = jax.nn.log_sigmoid(g_ref[e, 1:2, :] + bf_ref[h])
        lf_b = jnp.broadcast_to(lf_row, (L, L))
        gi_b = jnp.broadcast_to(gi_row, (L, L))
        lf_col = jnp.sum(jnp.where(diag, lf_b, 0.0), axis=1, keepdims=True)
        gi_col = jnp.sum(jnp.where(diag, gi_b, 0.0), axis=1, keepdims=True)
        b_col = jnp.sum(jnp.where(causal, lf_b, 0.0), axis=1, keepdims=True)
        b_row = jnp.sum(jnp.where(t_idx <= s_idx, jnp.broadcast_to(lf_col, (L, L)), 0.0),
                        axis=0, keepdims=True)

        log_d = jnp.where(causal, b_col + (gi_row - b_row), -jnp.inf)
        m_prev = m_ref[e, 0:1, 0:1]
        inter = b_col + m_prev
        m_t = jnp.maximum(inter, jnp.max(log_d, axis=1, keepdims=True))
        decay = jnp.exp(log_d - m_t)

        qf = q_ref[:, qs]
        kf = k_ref[:, qs] * (dk ** -0.5)
        qb = qf.astype(BF16)
        kb = kf.astype(BF16)
        vb = v_ref[:, vs].astype(BF16)
        s = _dot_nt(qb, kb) * decay
        sc = jnp.exp(inter - m_t)
        num = jnp.dot(s.astype(BF16), vb, preferred_element_type=F32) \
            + sc * jnp.dot(qb, c_ref[e].astype(BF16), preferred_element_type=F32)
        den = jnp.sum(s, axis=1, keepdims=True) + sc * jnp.sum(qf * n_ref[e], axis=1, keepdims=True)
        hc = num / jnp.maximum(jnp.abs(den), jnp.exp(-m_t))
        hn = hc * lax.rsqrt(jnp.mean(hc * hc, axis=1, keepdims=True) + EPS) * gmh_ref[e]
        y_ref[:, vs] = (jax.nn.sigmoid(o_ref[:, vs]) * hn).astype(y_ref.dtype)

        m_new = m_t[valid - 1:valid, :]
        b_last = b_col[valid - 1:valid, :]
        dec = jnp.exp(b_last + m_prev - m_new)
        ws = jnp.where(row < valid, jnp.exp(b_last - b_col + gi_col - m_new), 0.0)
        kw = ws * kf
        c_ref[e] = dec * c_ref[e] + lax.dot_general(kw.astype(BF16), vb, (((0,), (0,)), ((), ())),
                                                    preferred_element_type=F32)
        n_ref[e] = dec * n_ref[e] + jnp.sum(kw, axis=0, keepdims=True)
        m_ref[e] = jnp.broadcast_to(m_new, m_ref.shape[1:])


def _mlstm(proj, gates_row, b_i, b_f, g_mh, c0, n0, m0, *, chunk, valid):
    b, t, _ = proj.shape
    nh, dk, dv = c0.shape[1], c0.shape[2], c0.shape[3]
    hps = MLSTM_HEADS_PER_STEP
    assert nh % hps == 0 and (2 * nh * dk) % (hps * dv) == 0
    k_blk0 = nh // hps
    v_blk0 = 2 * nh * dk // (hps * dv)
    o_blk0 = v_blk0 + nh // hps
    smem = pl.BlockSpec(memory_space=pltpu.SMEM)
    state = lambda r, cdim: pl.BlockSpec((None, hps, r, cdim), lambda bi, hi, ci: (bi, hi, 0, 0))
    return pl.pallas_call(
        functools.partial(_mlstm_body, chunk=chunk, valid=valid, heads=hps),
        out_shape=(jax.ShapeDtypeStruct((b, t, nh * dv), BF16),
                   jax.ShapeDtypeStruct(c0.shape, F32),
                   jax.ShapeDtypeStruct(n0.shape, F32),
                   jax.ShapeDtypeStruct(m0.shape, F32)),
        grid=(b, nh // hps, t // chunk),
        in_specs=[
            smem, smem,
            pl.BlockSpec((None, chunk, hps * dk), lambda bi, hi, ci: (bi, ci, hi)),
            pl.BlockSpec((None, chunk, hps * dk), lambda bi, hi, ci: (bi, ci, k_blk0 + hi)),
            pl.BlockSpec((None, chunk, hps * dv), lambda bi, hi, ci: (bi, ci, v_blk0 + hi)),
            pl.BlockSpec((None, chunk, hps * dv), lambda bi, hi, ci: (bi, ci, o_blk0 + hi)),
            pl.BlockSpec((None, hps, 2, chunk), lambda bi, hi, ci: (bi, hi, 0, ci)),
            pl.BlockSpec((hps, 1, dv), lambda bi, hi, ci: (hi, 0, 0)),
            state(dk, dv), state(1, dk), state(1, LANES),
        ],
        out_specs=(pl.BlockSpec((None, chunk, hps * dv), lambda bi, hi, ci: (bi, ci, hi)),
                   state(dk, dv), state(1, dk), state(1, LANES)),
        compiler_params=_params(3),
        name="mlstm",
    )(b_i, b_f, proj, proj, proj, proj, gates_row, g_mh.reshape(nh, 1, dv), c0, n0, m0)


def _matmul_resid_body(a_ref, w_ref, r_ref, o_ref):
    o_ref[...] = r_ref[...] + jnp.dot(a_ref[...], w_ref[...], preferred_element_type=F32)


def _matmul_resid(a, w, resid, layer, *, tn=512):
    m, k = a.shape
    n = w.shape[2]
    tm = _row_tile(m)
    return pl.pallas_call(
        _matmul_resid_body,
        out_shape=jax.ShapeDtypeStruct((m, n), F32),
        grid=(m // tm, n // tn),
        in_specs=[pl.BlockSpec((tm, k), lambda i, j: (i, 0)),
                  pl.BlockSpec((None, k, tn), lambda i, j: (layer, 0, j)),
                  pl.BlockSpec((tm, tn), lambda i, j: (i, j))],
        out_specs=pl.BlockSpec((tm, tn), lambda i, j: (i, j)),
        compiler_params=_params(2),
        name="matmul_resid",
    )(a, w, resid)


def _rope_table_body(invf_ref, cos_ref, sin_ref, *, seq, base):
    rows = cos_ref.shape[0]
    m = lax.broadcasted_iota(jnp.int32, (rows, LANES), 0) + pl.program_id(0) * rows
    pos = (base + (m & (seq - 1))).astype(F32)
    ang = pos * invf_ref[...]
    cos_ref[...] = jnp.cos(ang)
    sin_ref[...] = jnp.sin(ang)


def _rope_tables(m, seq, base):
    assert seq & (seq - 1) == 0
    half = ROT_DIM // 2
    inv_freq = ROPE_THETA ** (-jnp.arange(half, dtype=F32) / half)
    invf = jnp.zeros((1, LANES), F32).at[0, :ROT_DIM].set(jnp.tile(inv_freq, 2))
    rows = min(m, 512)
    spec = pl.BlockSpec((rows, LANES), lambda i: (i, 0))
    return pl.pallas_call(
        functools.partial(_rope_table_body, seq=seq, base=base),
        out_shape=(jax.ShapeDtypeStruct((m, LANES), F32),) * 2,
        grid=(m // rows,),
        in_specs=[pl.BlockSpec((1, LANES), lambda i: (0, 0))],
        out_specs=(spec, spec),
        compiler_params=_params(1),
        name="rope_tables",
    )(invf)


def _to_head_slabs(res_ref, x):
    for hh in range(res_ref.shape[0]):
        res_ref[hh] = x[:, hh * ATT_HEAD_DIM:(hh + 1) * ATT_HEAD_DIM]


def _norm_rope_to_slabs(nat_ref, col0, res_ref, gain_ref, cos_ref, sin_ref):
    n_heads, rows, _ = res_ref.shape
    step = min(rows, EPI_ROWS)
    for r in range(rows // step):
        rs = slice(r * step, (r + 1) * step)
        for hh in range(n_heads):
            hs = slice(col0 + hh * ATT_HEAD_DIM, col0 + (hh + 1) * ATT_HEAD_DIM)
            res_ref[hh, rs, :] = _head_norm(nat_ref[rs, hs], gain_ref[...])
    for r in range(rows // step):
        rs = slice(r * step, (r + 1) * step)
        cosf = cos_ref[rs, :]
        sinf = sin_ref[rs, :]
        for hh in range(n_heads):
            res_ref[hh, rs, :] = _head_rope(res_ref[hh, rs, :], cosf, sinf)


def _emit_slabs(res_ref, out_ref, dil):
    n_heads, rows, _ = res_ref.shape
    n = n_heads * ATT_HEAD_DIM
    for r in range(dil):
        for hh in range(n_heads):
            col = r * n + hh * ATT_HEAD_DIM
            src = res_ref[hh] if dil == 1 else res_ref[hh, pl.ds(r, rows // dil, stride=dil), :]
            out_ref[:, col:col + ATT_HEAD_DIM] = src.astype(out_ref.dtype)


def _emit_heads(res_ref, out_ref):
    n_heads, rows, _ = res_ref.shape
    for hh in range(n_heads):
        out_ref[pl.ds(hh, rows, stride=n_heads), :] = res_ref[hh]


def _class_spec(tm, n, seq, dil):
    tiles = seq // tm
    assert seq % tm == 0 and tm % (dil * BF16_SUBLANES) == 0
    return pl.BlockSpec((None, tm // dil, dil * n), lambda i, *_: (i // tiles, i % tiles, 0))


def _kv_proj_body(*refs, tm, dilations):
    n_cls = len(dilations)
    x_hbm, gain_ref, w_ref, kgain_ref, cos_ref, sin_ref, k_ref, v_ref = refs[:8]
    kc_refs = refs[8:8 + n_cls]
    vc_refs = refs[8 + n_cls:8 + 2 * n_cls]
    xbuf, xn_ref, res_ref, sem = refs[8 + 2 * n_cls:]
    i = pl.program_id(0)
    j = pl.program_id(1)

    n = w_ref.shape[1] // 2

    @pl.when(j == 0)
    def _k_half():
        _load_norm_rows(x_hbm, xbuf, gain_ref, xn_ref, sem, pl.multiple_of(i * tm, tm), tm)
        xbuf[:, 0:n] = jnp.dot(xn_ref[...], w_ref[:, 0:n], preferred_element_type=F32)

    @pl.when(j == 1)
    def _v_half():
        v = jnp.dot(xn_ref[...], w_ref[:, n:2 * n], preferred_element_type=F32)
        _norm_rope_to_slabs(xbuf, 0, res_ref, kgain_ref, cos_ref, sin_ref)
        _emit_heads(res_ref, k_ref)
        for dil, ref in zip(dilations, kc_refs):
            _emit_slabs(res_ref, ref, dil)
        _to_head_slabs(res_ref, v)
        _emit_heads(res_ref, v_ref)
        for dil, ref in zip(dilations, vc_refs):
            _emit_slabs(res_ref, ref, dil)


def _kv_proj(x, gain, w_kv, g_knorm, cosf, sinf, seq, dilations=()):
    m, d = x.shape
    n = w_kv.shape[2] // 2
    n_heads = n // ATT_HEAD_DIM
    tm = min(m, PROJ_ROW_TILE)
    out_spec = pl.BlockSpec((tm * n_heads, ATT_HEAD_DIM), lambda i, j: (i, 0))
    tab_spec = pl.BlockSpec((tm, LANES), lambda i, j: (i, 0))
    cls_specs = [_class_spec(tm, n, seq, dil) for dil in dilations]
    cls_shapes = [jax.ShapeDtypeStruct((m // seq, seq // dil, dil * n), BF16) for dil in dilations]
    outs = pl.pallas_call(
        functools.partial(_kv_proj_body, tm=tm, dilations=tuple(dilations)),
        out_shape=[jax.ShapeDtypeStruct((m * n_heads, ATT_HEAD_DIM), F32)] * 2 + cls_shapes * 2,
        grid=(m // tm, 2),
        in_specs=[pl.BlockSpec(memory_space=pl.ANY),
                  pl.BlockSpec((1, d), lambda i, j: (0, 0)),
                  pl.BlockSpec((None, d, 2 * n), lambda i, j: (0, 0, 0), pipeline_mode=pl.Buffered(1)),
                  pl.BlockSpec((1, ATT_HEAD_DIM), lambda i, j: (0, 0)),
                  tab_spec, tab_spec],
        out_specs=[out_spec, out_spec] + cls_specs * 2,
        scratch_shapes=[pltpu.VMEM((tm, d), F32), pltpu.VMEM((tm, d), BF16),
                        pltpu.VMEM((n_heads, tm, ATT_HEAD_DIM), F32),
                        pltpu.SemaphoreType.DMA((LOAD_CHUNKS,))],
        compiler_params=_params(2),
        name="kv_proj",
    )(x, gain.reshape(1, d), w_kv, g_knorm.reshape(1, ATT_HEAD_DIM), cosf, sinf)
    n_cls = len(dilations)
    return outs[0], outs[1], tuple(outs[2:2 + n_cls]), tuple(outs[2 + n_cls:])


def _q_proj_body(*refs, tm, n_groups, dilations):
    x_hbm, gain_ref, w_ref, qgain_ref, cos_ref, sin_ref = refs[:6]
    n_out = max(len(dilations), 1)
    q_refs = refs[6:6 + n_out]
    xbuf, xn_ref, res_ref, sem = refs[6 + n_out:]
    i = pl.program_id(0)
    j = pl.program_id(1)

    tn = w_ref.shape[1] // n_groups
    for g in range(n_groups + 1):
        @pl.when(j == g)
        def _step(g=g):
            if g == 0:
                _load_norm_rows(x_hbm, xbuf, gain_ref, xn_ref, sem, pl.multiple_of(i * tm, tm), tm)
            if g < n_groups:
                proj = jnp.dot(xn_ref[...], w_ref[:, g * tn:(g + 1) * tn], preferred_element_type=F32)
            if g > 0:
                _norm_rope_to_slabs(xbuf, ((g - 1) % 2) * tn, res_ref, qgain_ref, cos_ref, sin_ref)
                if dilations:
                    _emit_slabs(res_ref, q_refs[g - 1], dilations[g - 1])
                else:
                    _emit_slabs(res_ref, q_refs[0], 1)
            if g < n_groups:
                xbuf[:, (g % 2) * tn:(g % 2 + 1) * tn] = proj


def _q_proj(x, gain, w_q, g_qnorm, cosf, sinf, layer_mix, layer_b, seq, dilations=()):
    m, d = x.shape
    n_groups = g_qnorm.shape[1]
    tn = w_q.shape[2] // n_groups
    tm = min(m, PROJ_ROW_TILE)
    tab_spec = pl.BlockSpec((tm, LANES), lambda i, j: (i, 0))
    if dilations:
        assert len(dilations) == n_groups
        out_shape = [jax.ShapeDtypeStruct((m // seq, seq // dil, dil * tn), BF16) for dil in dilations]
        out_specs = [_class_spec(tm, tn, seq, dil) for dil in dilations]
    else:
        out_shape = [jax.ShapeDtypeStruct((m, n_groups * tn), BF16)]
        out_specs = [pl.BlockSpec((tm, tn), lambda i, j: (i, jnp.maximum(j - 1, 0)))]
    assert d >= 2 * tn
    return pl.pallas_call(
        functools.partial(_q_proj_body, tm=tm, n_groups=n_groups, dilations=tuple(dilations)),
        out_shape=out_shape,
        grid=(m // tm, n_groups + 1),
        in_specs=[pl.BlockSpec(memory_space=pl.ANY),
                  pl.BlockSpec((None, 1, d), lambda i, j: (layer_mix, 0, 0)),
                  pl.BlockSpec((None, d, n_groups * tn), lambda i, j: (layer_b, 0, 0),
                               pipeline_mode=pl.Buffered(1)),
                  pl.BlockSpec((None, None, 1, ATT_HEAD_DIM), lambda i, j: (layer_b, jnp.maximum(j - 1, 0), 0, 0)),
                  tab_spec, tab_spec],
        out_specs=out_specs,
        scratch_shapes=[pltpu.VMEM((tm, d), F32), pltpu.VMEM((tm, d), BF16),
                        pltpu.VMEM((tn // ATT_HEAD_DIM, tm, ATT_HEAD_DIM), F32),
                        pltpu.SemaphoreType.DMA((LOAD_CHUNKS,))],
        compiler_params=_params(2),
        name="q_proj",
    )(x, gain.reshape(gain.shape[0], 1, d), w_q,
      g_qnorm.reshape(g_qnorm.shape[0], n_groups, 1, ATT_HEAD_DIM), cosf, sinf)


def _softmax_pieces(scores, masks):
    masked = [jnp.where(mk, sc, -jnp.inf) for sc, mk in zip(scores, masks)]
    m = functools.reduce(jnp.maximum, [jnp.max(sc, axis=1, keepdims=True) for sc in masked])
    probs = [jnp.exp(sc - m) for sc in masked]
    l = functools.reduce(jnp.add, [jnp.sum(p, axis=1, keepdims=True) for p in probs])
    return probs, l, m


def _attn_prompt_body(*refs, tq, win, dil, has_prev, n_heads):
    if has_prev:
        q_ref, kc_ref, vc_ref, kp_ref, vp_ref, o_ref, lse_ref = refs
    else:
        q_ref, kc_ref, vc_ref, o_ref, lse_ref = refs
    qi = pl.program_id(1)
    res = pl.program_id(2)
    scale = ATT_HEAD_DIM ** -0.5
    row = lax.broadcasted_iota(jnp.int32, (tq, tq), 0)
    col = lax.broadcasted_iota(jnp.int32, (tq, tq), 1)
    mask_c = (col <= row) & (row - col <= win)
    if has_prev:
        rowp = lax.broadcasted_iota(jnp.int32, (tq, win), 0)
        colp = lax.broadcasted_iota(jnp.int32, (tq, win), 1)
        mask_p = (colp >= rowp) & (qi > 0)
    rows = slice(None) if dil == 1 else pl.ds(res, tq, stride=dil)
    for hh in range(n_heads):
        hs = slice(hh * ATT_HEAD_DIM, (hh + 1) * ATT_HEAD_DIM)
        qh = q_ref[:, hs]
        scores = [_dot_nt(qh, _as_bf16(kc_ref[:, hs])) * scale]
        masks = [mask_c]
        values = [_as_bf16(vc_ref[:, hs])]
        if has_prev:
            scores.append(_dot_nt(qh, _as_bf16(kp_ref[:, hs])) * scale)
            masks.append(mask_p)
            values.append(_as_bf16(vp_ref[:, hs]))
        probs, l, m = _softmax_pieces(scores, masks)
        acc = functools.reduce(jnp.add, [jnp.dot(p.astype(BF16), v, preferred_element_type=F32)
                                         for p, v in zip(probs, values)])
        o_ref[hh, rows, :] = acc / l
        lse_ref[hh, rows, :] = jnp.broadcast_to(m + jnp.log(l), (tq, ATT_HEAD_DIM))


def _attn_prompt_group(q, k, v, seq, window, dil):
    b = q.shape[0]
    n = q.shape[2] // dil
    n_heads = n // ATT_HEAD_DIM
    tc = seq // dil
    win = window // dil
    tq = min(tc, ATTN_Q_TILE)
    has_prev = tc > tq
    assert tq % win == 0 and tc % tq == 0
    cur = pl.BlockSpec((None, tq, n), lambda bi, qi, r: (bi, qi, r))
    prev = pl.BlockSpec((None, win, n), lambda bi, qi, r: (bi, jnp.maximum(qi * (tq // win) - 1, 0), r))
    out = pl.BlockSpec((None, n_heads, tq * dil, ATT_HEAD_DIM), lambda bi, qi, r: (bi, 0, qi, 0))
    return pl.pallas_call(
        functools.partial(_attn_prompt_body, tq=tq, win=win, dil=dil, has_prev=has_prev, n_heads=n_heads),
        out_shape=(jax.ShapeDtypeStruct((b, n_heads, seq, ATT_HEAD_DIM), F32),) * 2,
        grid=(b, tc // tq, dil),
        in_specs=[cur, cur, cur] + ([prev, prev] if has_prev else []),
        out_specs=(out, out),
        compiler_params=_params(3),
        name=f"attn_prompt_d{dil}",
    )(q, k, v, *([k, v] if has_prev else []))


def _attn_out_body(*refs, n_groups):
    o_refs = refs[:n_groups]
    l_refs = refs[n_groups:2 * n_groups]
    w_ref, r_ref, out_ref, att_ref = refs[2 * n_groups:]

    @pl.when(pl.program_id(1) == 0)
    def _mix():
        for hh in range(o_refs[0].shape[0]):
            lses = [l[hh] for l in l_refs]
            m = functools.reduce(jnp.maximum, lses)
            es = [jnp.exp(l - m) for l in lses]
            tot = functools.reduce(jnp.add, es)
            att = functools.reduce(jnp.add, [(e / tot) * o[hh] for e, o in zip(es, o_refs)])
            att_ref[:, hh * ATT_HEAD_DIM:(hh + 1) * ATT_HEAD_DIM] = att.astype(BF16)

    out_ref[...] = r_ref[...] + jnp.dot(att_ref[...], w_ref[...], preferred_element_type=F32)


def _attn_out(outs, lses, w_o, resid, layer, *, tm=256):
    b, n_heads, seq, hd = outs[0].shape
    m, n = resid.shape
    k = n_heads * hd
    tn = n
    tm = min(seq, tm)
    tiles = seq // tm
    blk = pl.BlockSpec((None, n_heads, tm, hd), lambda i, j: (i // tiles, 0, i % tiles, 0))
    return pl.pallas_call(
        functools.partial(_attn_out_body, n_groups=len(outs)),
        out_shape=jax.ShapeDtypeStruct((m, n), F32),
        grid=(m // tm, n // tn),
        in_specs=[blk] * (2 * len(outs)) + [pl.BlockSpec((None, k, tn), lambda i, j: (layer, 0, j)),
                                            pl.BlockSpec((tm, tn), lambda i, j: (i, j))],
        out_specs=pl.BlockSpec((tm, tn), lambda i, j: (i, j)),
        scratch_shapes=[pltpu.VMEM((tm, k), BF16)],
        compiler_params=_params(2),
        name="attn_out",
    )(*outs, *lses, w_o, resid)


def _attn_sample_body(q_ref, kc_ref, vc_ref, kn_ref, vn_ref, o_ref, *, groups, n_heads, n_new):
    rows = q_ref.shape[0]
    cache = kc_ref.shape[0] // n_heads
    n = n_heads * ATT_HEAD_DIM
    scale = ATT_HEAD_DIM ** -0.5
    dist_c = cache + lax.broadcasted_iota(jnp.int32, (rows, cache), 0) \
        - lax.broadcasted_iota(jnp.int32, (rows, cache), 1)
    dist_n = lax.broadcasted_iota(jnp.int32, (rows, rows), 0) - lax.broadcasted_iota(jnp.int32, (rows, rows), 1)
    col_n = lax.broadcasted_iota(jnp.int32, (rows, rows), 1)
    for hh in range(n_heads):
        hs = slice(hh * ATT_HEAD_DIM, (hh + 1) * ATT_HEAD_DIM)
        kc = kc_ref[pl.ds(hh, cache, stride=n_heads), :].astype(BF16)
        vc = vc_ref[pl.ds(hh, cache, stride=n_heads), :].astype(BF16)
        kn = kn_ref[:, hs].astype(BF16)
        vn = vn_ref[:, hs].astype(BF16)
        outs, lses = [], []
        for g, (window, dil) in enumerate(groups):
            qh = q_ref[:, g * n + hh * ATT_HEAD_DIM:g * n + (hh + 1) * ATT_HEAD_DIM]
            mask_c = (dist_c <= window) & ((dist_c & (dil - 1)) == 0)
            mask_n = (dist_n >= 0) & (col_n < n_new) & ((dist_n & (dil - 1)) == 0)
            probs, l, m = _softmax_pieces([_dot_nt(qh, kc) * scale, _dot_nt(qh, kn) * scale], [mask_c, mask_n])
            acc = jnp.dot(probs[0].astype(BF16), vc, preferred_element_type=F32) \
                + jnp.dot(probs[1].astype(BF16), vn, preferred_element_type=F32)
            outs.append(acc / l)
            lses.append(m + jnp.log(l))
        m_all = functools.reduce(jnp.maximum, lses)
        es = [jnp.exp(l - m_all) for l in lses]
        tot = functools.reduce(jnp.add, es)
        o_ref[:, hs] = functools.reduce(jnp.add, [(e / tot) * o for e, o in zip(es, outs)]).astype(o_ref.dtype)


def _attn_sample(q, k_new, v_new, cache_k, cache_v, n_new):
    b, rows, n = k_new.shape
    cache, n_heads = cache_k.shape[1], cache_k.shape[2]
    assert cache == MAX_WINDOW and all(d & (d - 1) == 0 for _, d in DILATED_GROUPS)
    row_blk = lambda width: pl.BlockSpec((None, rows, width), lambda bi: (bi, 0, 0))
    cache_blk = pl.BlockSpec((None, cache * n_heads, ATT_HEAD_DIM), lambda bi: (bi, 0, 0))
    cache_k = cache_k.reshape(b, cache * n_heads, ATT_HEAD_DIM)
    cache_v = cache_v.reshape(b, cache * n_heads, ATT_HEAD_DIM)
    return pl.pallas_call(
        functools.partial(_attn_sample_body, groups=DILATED_GROUPS, n_heads=n // ATT_HEAD_DIM, n_new=n_new),
        out_shape=jax.ShapeDtypeStruct((b, rows, n), BF16),
        grid=(b,),
        in_specs=[row_blk(q.shape[2]), cache_blk, cache_blk, row_blk(n), row_blk(n)],
        out_specs=row_blk(n),
        compiler_params=_params(1),
        name="attn_sample",
    )(q, cache_k, cache_v, k_new, v_new)


FFN_ORDER = (("ffn1", 0), ("ffn2", 0), ("ffn1", 1), ("ffn2", 1))
SMALL_WEIGHTS = (("w_in_a_t", True), ("w_out_a", False), ("w_kv", False), ("w_q_b", False), ("w_o_b", False))


def _ffn_stacks(p, step):
    name, _ = FFN_ORDER[step]
    return tuple(p[f"w_{name}_{part}"] for part in ("gate", "up", "down"))


def _run_ffn(x, p, bf16, step):
    name, layer = FFN_ORDER[step]
    if ("ffn", step) not in bf16:
        x, bf16[("ffn", step)], _ = _ffn(x, p[f"norm_{name}"], layer, _ffn_stacks(p, step), layer, emit=True)
        return x
    casts, keys = [], []
    if step + 1 < len(FFN_ORDER) and ("ffn", step + 1) not in bf16:
        gate, up, down = _ffn_stacks(p, step + 1)
        nxt_layer = FFN_ORDER[step + 1][1]
        casts += [(gate, nxt_layer, False), (up, nxt_layer, False), (down, nxt_layer, True)]
        keys += [("ffn", step + 1)] * 3
    for key, rows_follow_j in SMALL_WEIGHTS:
        if key not in bf16:
            assert p[key].shape[0] == 1
            casts.append((p[key], 0, rows_follow_j))
            keys.append(key)
    x, _, cast = _ffn(x, p[f"norm_{name}"], layer, bf16[("ffn", step)], casts=tuple(casts))
    for key, arr in zip(keys, cast):
        if key[0] == "ffn":
            bf16[key] = bf16.get(key, ()) + (arr,)
        else:
            bf16[key] = arr
    return x


def _trunk(x, seq, base, cache_k, cache_v, c0, n0, m0, p, bf16, first_ffn_done=False):
    m_rows, d = x.shape
    b = m_rows // seq
    nh = MLSTM_HEADS
    dk, dv = c0.shape[2], c0.shape[3]
    fresh = cache_k is None
    dilations = tuple(dil for _, dil in DILATED_GROUPS)

    if not first_ffn_done:
        x = _run_ffn(x, p, bf16, 0)
    proj, gates = _in_proj(x, p["norm_mix"], bf16["w_in_a_t"], 2 * nh * dk + nh * dv + d, 0)
    if seq % MLSTM_PROMPT_CHUNK == 0:
        chunk, valid, t_pad = MLSTM_PROMPT_CHUNK, MLSTM_PROMPT_CHUNK, seq
    else:
        assert seq <= BF16_SUBLANES
        chunk, valid, t_pad = BF16_SUBLANES, seq, BF16_SUBLANES
    proj3 = proj.reshape(b, seq, -1)
    gates3 = gates.reshape(b, seq, LANES)[:, :, :2 * nh]
    if t_pad != seq:
        proj3 = jnp.pad(proj3, ((0, 0), (0, t_pad - seq), (0, 0)))
        gates3 = jnp.pad(gates3, ((0, 0), (0, t_pad - seq), (0, 0)))
    gates_row = gates3.reshape(b, t_pad, 2, nh).transpose(0, 3, 2, 1)
    y, c_fin, n_fin, m_fin = _mlstm(proj3, gates_row, p["b_ig"][0], p["b_fg"][0], p["g_mh"][0],
                                    c0, n0, m0, chunk=chunk, valid=valid)
    y = y[:, :seq].reshape(m_rows, nh * dv)
    x = _matmul_resid(y, bf16["w_out_a"], x, 0)
    x = _run_ffn(x, p, bf16, 1)

    cosf, sinf = _rope_tables(m_rows, seq, base)
    cls_dils = dilations if fresh else ()
    k, v, k_cls, v_cls = _kv_proj(x, p["norm_kv"], bf16["w_kv"], p["g_knorm"], cosf, sinf, seq, cls_dils)

    x = _run_ffn(x, p, bf16, 2)
    q = _q_proj(x, p["norm_mix"], bf16["w_q_b"], p["g_qnorm"], cosf, sinf, 1, 0, seq, dilations if fresh else ())
    n_kv = k.shape[0] // m_rows * k.shape[1]
    if fresh:
        outs, lses = [], []
        for q_g, k_g, v_g, (window, dil) in zip(q, k_cls, v_cls, DILATED_GROUPS):
            o_g, l_g = _attn_prompt_group(q_g, k_g, v_g, seq, window, dil)
            outs.append(o_g)
            lses.append(l_g)
        x = _attn_out(outs, lses, bf16["w_o_b"], x, 0)
    else:
        pad = ((0, 0), (0, BF16_SUBLANES - seq), (0, 0))
        att = _attn_sample(jnp.pad(q[0].reshape(b, seq, -1), pad), jnp.pad(k.reshape(b, seq, n_kv), pad),
                           jnp.pad(v.reshape(b, seq, n_kv), pad), cache_k, cache_v, seq)
        x = _matmul_resid(att[:, :seq].reshape(m_rows, n_kv), bf16["w_o_b"], x, 0)
    x = _run_ffn(x, p, bf16, 3)
    return x, k, v, c_fin, n_fin, m_fin


def kernel(x_prompt, x_sample, cache_k, cache_v, state_C, state_n, state_m, norm_ffn1, w_ffn1_gate, w_ffn1_up, w_ffn1_down, norm_mix, norm_ffn2, w_ffn2_gate, w_ffn2_up, w_ffn2_down, w_in_a, b_ig, b_fg, g_mh, w_out_a, norm_kv, w_kv, g_knorm, w_q_b, g_qnorm, w_o_b):
    bp, seq_p, d = x_prompt.shape
    bs, seq_s, _ = x_sample.shape
    past = cache_k.shape[1]
    assert past == min(MAX_WINDOW, PAST_LEN)
    nh = MLSTM_HEADS
    dk, dv = state_C.shape[3], state_C.shape[4]

    p = dict(norm_ffn1=norm_ffn1, norm_mix=norm_mix, norm_ffn2=norm_ffn2, norm_kv=norm_kv,
             w_ffn1_gate=w_ffn1_gate, w_ffn1_up=w_ffn1_up, w_ffn1_down=w_ffn1_down,
             w_ffn2_gate=w_ffn2_gate, w_ffn2_up=w_ffn2_up, w_ffn2_down=w_ffn2_down,
             w_in_a_t=jnp.swapaxes(w_in_a, 1, 2), b_ig=b_ig, b_fg=b_fg, g_mh=g_mh, w_out_a=w_out_a, w_kv=w_kv[None],
             g_knorm=g_knorm, w_q_b=w_q_b, g_qnorm=g_qnorm, w_o_b=w_o_b)
    bf16 = {}
    xs = _run_ffn(x_sample.reshape(bs * seq_s, d), p, bf16, 0)

    zeros = lambda *s: jnp.zeros(s, F32)
    y_p, k_p, v_p, c_p, n_p, m_p = _trunk(
        x_prompt.reshape(bp * seq_p, d), seq_p, 0, None, None,
        zeros(bp, nh, dk, dv), zeros(bp, nh, 1, dk), zeros(bp, nh, 1, LANES), p, bf16)
    y_s, k_s, v_s, c_s, n_s, m_s = _trunk(
        xs, seq_s, PAST_LEN, cache_k, cache_v,
        state_C[:, 0], state_n[:, 0][:, :, None, :],
        jnp.broadcast_to(state_m[:, 0][:, :, None, None], (bs, nh, 1, LANES)), p, bf16, first_ffn_done=True)

    hkv, hd = cache_k.shape[2], cache_k.shape[3]
    keep = min(MAX_WINDOW, seq_p)
    return (y_p.reshape(bp, seq_p, d), y_s.reshape(bs, seq_s, d),
            k_p.reshape(bp, seq_p, hkv, hd)[:, seq_p - keep:], v_p.reshape(bp, seq_p, hkv, hd)[:, seq_p - keep:],
            k_s.reshape(bs, seq_s, hkv, hd), v_s.reshape(bs, seq_s, hkv, hd),
            c_p[:, None], n_p[:, None, :, 0], m_p[:, None, :, 0, 0],
            c_s[:, None], n_s[:, None, :, 0], m_s[:, None, :, 0, 0])
```

```python
import functools

import jax
import jax.numpy as jnp
from jax import lax
from jax.experimental import pallas as pl
from jax.experimental.pallas import tpu as pltpu

F32 = jnp.float32
BF16 = jnp.bfloat16

EPS = 1e-6
MLSTM_HEADS = 8
ATT_HEAD_DIM = 128
DILATED_GROUPS = ((128, 1), (512, 4), (2048, 16))
MAX_WINDOW = max(w for w, _ in DILATED_GROUPS)
ROT_DIM = ATT_HEAD_DIM // 4
ROPE_THETA = 500000.0
PAST_LEN = 16384

V7X_VMEM_LIMIT_BYTES = 56 * 1024 * 1024
LANES = 128
BF16_SUBLANES = 16
NORM_ROWS = 16
NORM_UNROLL = 4
LOAD_CHUNKS = 8
EPI_ROWS = 64
PROMPT_ROW_TILE = 1024
PROJ_ROW_TILE = 512
MLSTM_PROMPT_CHUNK = 256
MLSTM_HEADS_PER_STEP = 4
ATTN_Q_TILE = 256


def _params(n_axes):
    return pltpu.CompilerParams(dimension_semantics=("arbitrary",) * n_axes,
                                vmem_limit_bytes=V7X_VMEM_LIMIT_BYTES)


def _row_tile(m):
    return min(m, PROMPT_ROW_TILE)


def _round_up(x, mult):
    return -(-x // mult) * mult


def _rmsnorm_rows(src_ref, gain_ref, dst_ref, rows):
    step = min(rows, NORM_ROWS)

    def body(r, carry):
        rs = pl.ds(pl.multiple_of(r * step, step), step)
        ms = jnp.mean(jnp.square(src_ref[rs, :]), axis=-1, keepdims=True)
        dst_ref[rs, :] = (src_ref[rs, :] * lax.rsqrt(ms + EPS) * gain_ref[...]).astype(BF16)
        return carry

    trips = rows // step
    lax.fori_loop(0, trips, body, 0, unroll=min(trips, NORM_UNROLL))


def _load_norm_rows(x_hbm, buf_ref, gain_ref, xn_ref, sems, row0, rows):
    n = LOAD_CHUNKS if rows % (LOAD_CHUNKS * NORM_ROWS) == 0 else 1
    piece = rows // n
    copies = [pltpu.make_async_copy(x_hbm.at[pl.ds(row0 + c * piece, piece)],
                                    buf_ref.at[pl.ds(c * piece, piece)], sems.at[c]) for c in range(n)]
    for cp in copies:
        cp.start()
    for c, cp in enumerate(copies):
        cp.wait()
        _rmsnorm_rows(buf_ref.at[pl.ds(c * piece, piece)], gain_ref, xn_ref.at[pl.ds(c * piece, piece)], piece)


def _head_norm(xh, gain):
    return xh * lax.rsqrt(jnp.mean(xh * xh, axis=-1, keepdims=True) + EPS) * gain


def _head_rope(y, cosf, sinf):
    half = ROT_DIM // 2
    lane = lax.broadcasted_iota(jnp.int32, y.shape, 1)
    from_lower = pltpu.roll(y, half, axis=1)
    from_upper = pltpu.roll(y, LANES - half, axis=1)
    rot = jnp.where(lane < half, -from_upper, from_lower)
    return y * cosf + rot * sinf


def _as_bf16(x):
    return x if x.dtype == BF16 else x.astype(BF16)


def _dot_nt(a, b):
    return lax.dot_general(a, b, (((1,), (1,)), ((), ())), preferred_element_type=F32)


def _cast_specs(src, layer, grid, rows_follow_j):
    gi, gj = grid
    _, r, c = src.shape
    if rows_follow_j:
        rb, cb = _round_up(pl.cdiv(r, gj), BF16_SUBLANES), c // gi
        assert c % gi == 0 and cb % LANES == 0
        last = pl.cdiv(r, rb) - 1
        index = lambda lay: (lambda i, j: (lay, jnp.minimum(j, last), i))
    else:
        rb, cb = r // gi, _round_up(pl.cdiv(c, gj), LANES)
        assert r % gi == 0 and rb % BF16_SUBLANES == 0
        last = pl.cdiv(c, cb) - 1
        index = lambda lay: (lambda i, j: (lay, i, jnp.minimum(j, last)))
    return (pl.BlockSpec((None, rb, cb), index(layer)), pl.BlockSpec((None, rb, cb), index(0)),
            jax.ShapeDtypeStruct((1, r, c), BF16))


def _ffn_body(*refs, tm, n_chunk, n_casts, emit):
    x_hbm, gain_ref, wg_ref, wu_ref, wd_ref = refs[:5]
    cast_in = refs[5:5 + n_casts]
    o_hbm = refs[5 + n_casts]
    n_emit = 3 if emit else 0
    emit_refs = refs[6 + n_casts:6 + n_casts + n_emit]
    cast_out = refs[6 + n_casts + n_emit:6 + 2 * n_casts + n_emit]
    acc_ref, xn_ref, sem = refs[6 + 2 * n_casts + n_emit:]

    for src, dst in zip(cast_in, cast_out):
        dst[...] = src[...].astype(BF16)

    i = pl.program_id(0)
    j = pl.program_id(1)
    row0 = pl.multiple_of(i * tm, tm)

    @pl.when(j == 0)
    def _load_and_norm():
        _load_norm_rows(x_hbm, acc_ref, gain_ref, xn_ref, sem, row0, tm)

    wg, wu, wd = wg_ref[...], wu_ref[...], wd_ref
    if emit:
        wg, wu, wd = wg.astype(BF16), wu.astype(BF16), wd_ref[...].astype(BF16)
        emit_refs[0][...] = wg
        emit_refs[1][...] = wu
        emit_refs[2][...] = wd
    xn = xn_ref[...]
    g = jnp.dot(xn, wg, preferred_element_type=F32)
    u = jnp.dot(xn, wu, preferred_element_type=F32)
    a = (0.5 * (g * jax.nn.sigmoid(g)) * u).astype(BF16)
    d = acc_ref.shape[1]
    for c in range(d // n_chunk):
        cs = slice(c * n_chunk, (c + 1) * n_chunk)
        acc_ref[:, cs] += jnp.dot(a, wd[:, cs], preferred_element_type=F32)

    @pl.when(j == pl.num_programs(1) - 1)
    def _store():
        cp = pltpu.make_async_copy(acc_ref, o_hbm.at[pl.ds(row0, tm)], sem.at[LOAD_CHUNKS])
        cp.start()
        cp.wait()


def _ffn(x, gain, layer, weights, w_layer=0, casts=(), emit=False, *, tf=256):
    wg, wu, wd = weights
    m, d = x.shape
    f = wg.shape[2]
    tm = _row_tile(m)
    grid = (m // tm, f // tf)
    assert not emit or grid[0] == 1
    in_specs = [
        pl.BlockSpec(memory_space=pl.ANY),
        pl.BlockSpec((None, 1, d), lambda i, j: (layer, 0, 0)),
        pl.BlockSpec((None, d, tf), lambda i, j: (w_layer, 0, j)),
        pl.BlockSpec((None, d, tf), lambda i, j: (w_layer, 0, j)),
        pl.BlockSpec((None, tf, d), lambda i, j: (w_layer, j, 0)),
    ]
    out_shape = [jax.ShapeDtypeStruct((m, d), F32)]
    out_specs = [pl.BlockSpec(memory_space=pl.ANY)]
    if emit:
        out_shape += [jax.ShapeDtypeStruct((1, d, f), BF16)] * 2 + [jax.ShapeDtypeStruct((1, f, d), BF16)]
        out_specs += [pl.BlockSpec((None, d, tf), lambda i, j: (0, 0, j))] * 2 \
            + [pl.BlockSpec((None, tf, d), lambda i, j: (0, j, 0))]
    for src, src_layer, rows_follow_j in casts:
        in_spec, out_spec, shape = _cast_specs(src, src_layer, grid, rows_follow_j)
        in_specs.append(in_spec)
        out_specs.append(out_spec)
        out_shape.append(shape)
    outs = pl.pallas_call(
        functools.partial(_ffn_body, tm=tm, n_chunk=min(d, 512), n_casts=len(casts), emit=emit),
        out_shape=out_shape,
        grid=grid,
        in_specs=in_specs,
        out_specs=out_specs,
        scratch_shapes=[pltpu.VMEM((tm, d), F32), pltpu.VMEM((tm, d), BF16),
                        pltpu.SemaphoreType.DMA((LOAD_CHUNKS + 1,))],
        compiler_params=_params(2),
        name="ffn",
    )(x, gain.reshape(gain.shape[0], 1, d), wg, wu, wd, *[src for src, _, _ in casts])
    n_emit = 3 if emit else 0
    return outs[0], tuple(outs[1:1 + n_emit]), tuple(outs[1 + n_emit:])


def _in_proj_body(x_hbm, gain_ref, w_ref, wgate_ref, proj_ref, gates_ref, xbuf, xn_ref, sem, *, tm, n_gates):
    i = pl.program_id(0)
    j = pl.program_id(1)

    @pl.when(j == 0)
    def _load_and_norm():
        _load_norm_rows(x_hbm, xbuf, gain_ref, xn_ref, sem, pl.multiple_of(i * tm, tm), tm)
        row = lax.broadcasted_iota(jnp.int32, wgate_ref.shape, 0)
        wgate = jnp.where(row < n_gates, wgate_ref[...], jnp.zeros_like(wgate_ref))
        gates_ref[...] = _dot_nt(xn_ref[...], wgate)

    proj_ref[...] = _dot_nt(xn_ref[...], w_ref[...])


def _in_proj(x, gain, w_in_t, n_main, layer, *, tn=512):
    m, d = x.shape
    n_gates = w_in_t.shape[1] - n_main
    assert n_main % tn == 0 and n_main % LANES == 0 and 0 < n_gates < LANES
    tm = _row_tile(m)
    return pl.pallas_call(
        functools.partial(_in_proj_body, tm=tm, n_gates=n_gates),
        out_shape=(jax.ShapeDtypeStruct((m, n_main), F32), jax.ShapeDtypeStruct((m, LANES), F32)),
        grid=(m // tm, n_main // tn),
        in_specs=[
            pl.BlockSpec(memory_space=pl.ANY),
            pl.BlockSpec((None, 1, d), lambda i, j: (layer, 0, 0)),
            pl.BlockSpec((None, tn, d), lambda i, j: (layer, j, 0)),
            pl.BlockSpec((None, LANES, d), lambda i, j: (layer, n_main // LANES, 0)),
        ],
        out_specs=(pl.BlockSpec((tm, tn), lambda i, j: (i, j)),
                   pl.BlockSpec((tm, LANES), lambda i, j: (i, 0))),
        scratch_shapes=[pltpu.VMEM((tm, d), F32), pltpu.VMEM((tm, d), BF16),
                        pltpu.SemaphoreType.DMA((LOAD_CHUNKS,))],
        compiler_params=_params(2),
        name="in_proj",
    )(x, gain.reshape(gain.shape[0], 1, d), w_in_t, w_in_t)


def _mlstm_body(bi_ref, bf_ref, q_ref, k_ref, v_ref, o_ref, g_ref, gmh_ref, c0_ref, n0_ref, m0_ref,
                y_ref, c_ref, n_ref, m_ref, *, chunk, valid, heads):
    hg = pl.program_id(1)
    c = pl.program_id(2)
    L = chunk
    dk = q_ref.shape[1] // heads
    dv = v_ref.shape[1] // heads

    @pl.when(c == 0)
    def _init():
        c_ref[...] = c0_ref[...]
        n_ref[...] = n0_ref[...]
        m_ref[...] = m0_ref[...]

    t_idx = lax.broadcasted_iota(jnp.int32, (L, L), 0)
    s_idx = lax.broadcasted_iota(jnp.int32, (L, L), 1)
    causal = s_idx <= t_idx
    diag = s_idx == t_idx
    row = lax.broadcasted_iota(jnp.int32, (L, 1), 0)

    for e in range(heads):
        h = hg * heads + e
        qs = slice(e * dk, (e + 1) * dk)
        vs = slice(e * dv, (e + 1) * dv)
        gi_row = g_ref[e, 0:1, :] + bi_ref[h]
        lf_row = jax.nn.log_sigmoid(g_ref[e, 1:2, :] + bf_ref[h])
        lf_b = jnp.broadcast_to(lf_row, (L, L))
        gi_b = jnp.broadcast_to(gi_row, (L, L))
        lf_col = jnp.sum(jnp.where(diag, lf_b, 0.0), axis=1, keepdims=True)
        gi_col = jnp.sum(jnp.where(diag, gi_b, 0.0), axis=1, keepdims=True)
        b_col = jnp.sum(jnp.where(causal, lf_b, 0.0), axis=1, keepdims=True)
        b_row = jnp.sum(jnp.where(t_idx <= s_idx, jnp.broadcast_to(lf_col, (L, L)), 0.0),
                        axis=0, keepdims=True)

        log_d = jnp.where(causal, b_col + (gi_row - b_row), -jnp.inf)
        m_prev = m_ref[e, 0:1, 0:1]
        inter = b_col + m_prev
        m_t = jnp.maximum(inter, jnp.max(log_d, axis=1, keepdims=True))
        decay = jnp.exp(log_d - m_t)

        qf = q_ref[:, qs]
        kf = k_ref[:, qs] * (dk ** -0.5)
        qb = qf.astype(BF16)
        kb = kf.astype(BF16)
        vb = v_ref[:, vs].astype(BF16)
        s = _dot_nt(qb, kb) * decay
        sc = jnp.exp(inter - m_t)
        num = jnp.dot(s.astype(BF16), vb, preferred_element_type=F32) \
            + sc * jnp.dot(qb, c_ref[e].astype(BF16), preferred_element_type=F32)
        den = jnp.sum(s, axis=1, keepdims=True) + sc * jnp.sum(qf * n_ref[e], axis=1, keepdims=True)
        hc = num / jnp.maximum(jnp.abs(den), jnp.exp(-m_t))
        hn = hc * lax.rsqrt(jnp.mean(hc * hc, axis=1, keepdims=True) + EPS) * gmh_ref[e]
        y_ref[:, vs] = (jax.nn.sigmoid(o_ref[:, vs]) * hn).astype(y_ref.dtype)

        m_new = m_t[valid - 1:valid, :]
        b_last = b_col[valid - 1:valid, :]
        dec = jnp.exp(b_last + m_prev - m_new)
        ws = jnp.where(row < valid, jnp.exp(b_last - b_col + gi_col - m_new), 0.0)
        kw = ws * kf
        c_ref[e] = dec * c_ref[e] + lax.dot_general(kw.astype(BF16), vb, (((0,), (0,)), ((), ())),
                                                    preferred_element_type=F32)
        n_ref[e] = dec * n_ref[e] + jnp.sum(kw, axis=0, keepdims=True)
        m_ref[e] = jnp.broadcast_to(m_new, m_ref.shape[1:])


def _mlstm(proj, gates_row, b_i, b_f, g_mh, c0, n0, m0, *, chunk, valid):
    b, t, _ = proj.shape
    nh, dk, dv = c0.shape[1], c0.shape[2], c0.shape[3]
    hps = MLSTM_HEADS_PER_STEP
    assert nh % hps == 0 and (2 * nh * dk) % (hps * dv) == 0
    k_blk0 = nh // hps
    v_blk0 = 2 * nh * dk // (hps * dv)
    o_blk0 = v_blk0 + nh // hps
    smem = pl.BlockSpec(memory_space=pltpu.SMEM)
    state = lambda r, cdim: pl.BlockSpec((None, hps, r, cdim), lambda bi, hi, ci: (bi, hi, 0, 0))
    return pl.pallas_call(
        functools.partial(_mlstm_body, chunk=chunk, valid=valid, heads=hps),
        out_shape=(jax.ShapeDtypeStruct((b, t, nh * dv), BF16),
                   jax.ShapeDtypeStruct(c0.shape, F32),
                   jax.ShapeDtypeStruct(n0.shape, F32),
                   jax.ShapeDtypeStruct(m0.shape, F32)),
        grid=(b, nh // hps, t // chunk),
        in_specs=[
            smem, smem,
            pl.BlockSpec((None, chunk, hps * dk), lambda bi, hi, ci: (bi, ci, hi)),
            pl.BlockSpec((None, chunk, hps * dk), lambda bi, hi, ci: (bi, ci, k_blk0 + hi)),
            pl.BlockSpec((None, chunk, hps * dv), lambda bi, hi, ci: (bi, ci, v_blk0 + hi)),
            pl.BlockSpec((None, chunk, hps * dv), lambda bi, hi, ci: (bi, ci, o_blk0 + hi)),
            pl.BlockSpec((None, hps, 2, chunk), lambda bi, hi, ci: (bi, hi, 0, ci)),
            pl.BlockSpec((hps, 1, dv), lambda bi, hi, ci: (hi, 0, 0)),
            state(dk, dv), state(1, dk), state(1, LANES),
        ],
        out_specs=(pl.BlockSpec((None, chunk, hps * dv), lambda bi, hi, ci: (bi, ci, hi)),
                   state(dk, dv), state(1, dk), state(1, LANES)),
        compiler_params=_params(3),
        name="mlstm",
    )(b_i, b_f, proj, proj, proj, proj, gates_row, g_mh.reshape(nh, 1, dv), c0, n0, m0)


def _matmul_resid_body(a_ref, w_ref, r_ref, o_ref):
    o_ref[...] = r_ref[...] + jnp.dot(a_ref[...], w_ref[...], preferred_element_type=F32)


def _matmul_resid(a, w, resid, layer, *, tn=512):
    m, k = a.shape
    n = w.shape[2]
    tm = _row_tile(m)
    return pl.pallas_call(
        _matmul_resid_body,
        out_shape=jax.ShapeDtypeStruct((m, n), F32),
        grid=(m // tm, n // tn),
        in_specs=[pl.BlockSpec((tm, k), lambda i, j: (i, 0)),
                  pl.BlockSpec((None, k, tn), lambda i, j: (layer, 0, j)),
                  pl.BlockSpec((tm, tn), lambda i, j: (i, j))],
        out_specs=pl.BlockSpec((tm, tn), lambda i, j: (i, j)),
        compiler_params=_params(2),
        name="matmul_resid",
    )(a, w, resid)


def _rope_table_body(invf_ref, cos_ref, sin_ref, *, seq, base):
    rows = cos_ref.shape[0]
    m = lax.broadcasted_iota(jnp.int32, (rows, LANES), 0) + pl.program_id(0) * rows
    pos = (base + (m & (seq - 1))).astype(F32)
    ang = pos * invf_ref[...]
    cos_ref[...] = jnp.cos(ang)
    sin_ref[...] = jnp.sin(ang)


def _rope_tables(m, seq, base):
    assert seq & (seq - 1) == 0
    half = ROT_DIM // 2
    inv_freq = ROPE_THETA ** (-jnp.arange(half, dtype=F32) / half)
    invf = jnp.zeros((1, LANES), F32).at[0, :ROT_DIM].set(jnp.tile(inv_freq, 2))
    rows = min(m, 512)
    spec = pl.BlockSpec((rows, LANES), lambda i: (i, 0))
    return pl.pallas_call(
        functools.partial(_rope_table_body, seq=seq, base=base),
        out_shape=(jax.ShapeDtypeStruct((m, LANES), F32),) * 2,
        grid=(m // rows,),
        in_specs=[pl.BlockSpec((1, LANES), lambda i: (0, 0))],
        out_specs=(spec, spec),
        compiler_params=_params(1),
        name="rope_tables",
    )(invf)


def _to_head_slabs(res_ref, x):
    for hh in range(res_ref.shape[0]):
        res_ref[hh] = x[:, hh * ATT_HEAD_DIM:(hh + 1) * ATT_HEAD_DIM]


def _norm_rope_to_slabs(nat_ref, col0, res_ref, gain_ref, cos_ref, sin_ref):
    n_heads, rows, _ = res_ref.shape
    step = min(rows, EPI_ROWS)
    for r in range(rows // step):
        rs = slice(r * step, (r + 1) * step)
        for hh in range(n_heads):
            hs = slice(col0 + hh * ATT_HEAD_DIM, col0 + (hh + 1) * ATT_HEAD_DIM)
            res_ref[hh, rs, :] = _head_norm(nat_ref[rs, hs], gain_ref[...])
    for r in range(rows // step):
        rs = slice(r * step, (r + 1) * step)
        cosf = cos_ref[rs, :]
        sinf = sin_ref[rs, :]
        for hh in range(n_heads):
            res_ref[hh, rs, :] = _head_rope(res_ref[hh, rs, :], cosf, sinf)


def _emit_slabs(res_ref, out_ref, dil):
    n_heads, rows, _ = res_ref.shape
    n = n_heads * ATT_HEAD_DIM
    for r in range(dil):
        for hh in range(n_heads):
            col = r * n + hh * ATT_HEAD_DIM
            src = res_ref[hh] if dil == 1 else res_ref[hh, pl.ds(r, rows // dil, stride=dil), :]
            out_ref[:, col:col + ATT_HEAD_DIM] = src.astype(out_ref.dtype)


def _emit_heads(res_ref, out_ref):
    n_heads, rows, _ = res_ref.shape
    for hh in range(n_heads):
        out_ref[pl.ds(hh, rows, stride=n_heads), :] = res_ref[hh]


def _class_spec(tm, n, seq, dil):
    tiles = seq // tm
    assert seq % tm == 0 and tm % (dil * BF16_SUBLANES) == 0
    return pl.BlockSpec((None, tm // dil, dil * n), lambda i, *_: (i // tiles, i % tiles, 0))


def _kv_proj_body(*refs, tm, dilations):
    n_cls = len(dilations)
    x_hbm, gain_ref, w_ref, kgain_ref, cos_ref, sin_ref, k_ref, v_ref = refs[:8]
    kc_refs = refs[8:8 + n_cls]
    vc_refs = refs[8 + n_cls:8 + 2 * n_cls]
    xbuf, xn_ref, res_ref, sem = refs[8 + 2 * n_cls:]
    i = pl.program_id(0)
    j = pl.program_id(1)

    n = w_ref.shape[1] // 2

    @pl.when(j == 0)
    def _k_half():
        _load_norm_rows(x_hbm, xbuf, gain_ref, xn_ref, sem, pl.multiple_of(i * tm, tm), tm)
        xbuf[:, 0:n] = jnp.dot(xn_ref[...], w_ref[:, 0:n], preferred_element_type=F32)

    @pl.when(j == 1)
    def _v_half():
        v = jnp.dot(xn_ref[...], w_ref[:, n:2 * n], preferred_element_type=F32)
        _norm_rope_to_slabs(xbuf, 0, res_ref, kgain_ref, cos_ref, sin_ref)
        _emit_heads(res_ref, k_ref)
        for dil, ref in zip(dilations, kc_refs):
            _emit_slabs(res_ref, ref, dil)
        _to_head_slabs(res_ref, v)
        _emit_heads(res_ref, v_ref)
        for dil, ref in zip(dilations, vc_refs):
            _emit_slabs(res_ref, ref, dil)


def _kv_proj(x, gain, w_kv, g_knorm, cosf, sinf, seq, dilations=()):
    m, d = x.shape
    n = w_kv.shape[2] // 2
    n_heads = n // ATT_HEAD_DIM
    tm = min(m, PROJ_ROW_TILE)
    out_spec = pl.BlockSpec((tm * n_heads, ATT_HEAD_DIM), lambda i, j: (i, 0))
    tab_spec = pl.BlockSpec((tm, LANES), lambda i, j: (i, 0))
    cls_specs = [_class_spec(tm, n, seq, dil) for dil in dilations]
    cls_shapes = [jax.ShapeDtypeStruct((m // seq, seq // dil, dil * n), BF16) for dil in dilations]
    outs = pl.pallas_call(
        functools.partial(_kv_proj_body, tm=tm, dilations=tuple(dilations)),
        out_shape=[jax.ShapeDtypeStruct((m * n_heads, ATT_HEAD_DIM), F32)] * 2 + cls_shapes * 2,
        grid=(m // tm, 2),
        in_specs=[pl.BlockSpec(memory_space=pl.ANY),
                  pl.BlockSpec((1, d), lambda i, j: (0, 0)),
                  pl.BlockSpec((None, d, 2 * n), lambda i, j: (0, 0, 0), pipeline_mode=pl.Buffered(1)),
                  pl.BlockSpec((1, ATT_HEAD_DIM), lambda i, j: (0, 0)),
                  tab_spec, tab_spec],
        out_specs=[out_spec, out_spec] + cls_specs * 2,
        scratch_shapes=[pltpu.VMEM((tm, d), F32), pltpu.VMEM((tm, d), BF16),
                        pltpu.VMEM((n_heads, tm, ATT_HEAD_DIM), F32),
                        pltpu.SemaphoreType.DMA((LOAD_CHUNKS,))],
        compiler_params=_params(2),
        name="kv_proj",
    )(x, gain.reshape(1, d), w_kv, g_knorm.reshape(1, ATT_HEAD_DIM), cosf, sinf)
    n_cls = len(dilations)
    return outs[0], outs[1], tuple(outs[2:2 + n_cls]), tuple(outs[2 + n_cls:])


def _q_proj_body(*refs, tm, n_groups, dilations):
    x_hbm, gain_ref, w_ref, qgain_ref, cos_ref, sin_ref = refs[:6]
    n_out = max(len(dilations), 1)
    q_refs = refs[6:6 + n_out]
    xbuf, xn_ref, res_ref, sem = refs[6 + n_out:]
    i = pl.program_id(0)
    j = pl.program_id(1)

    tn = w_ref.shape[1] // n_groups
    for g in range(n_groups + 1):
        @pl.when(j == g)
        def _step(g=g):
            if g == 0:
                _load_norm_rows(x_hbm, xbuf, gain_ref, xn_ref, sem, pl.multiple_of(i * tm, tm), tm)
            if g < n_groups:
                proj = jnp.dot(xn_ref[...], w_ref[:, g * tn:(g + 1) * tn], preferred_element_type=F32)
            if g > 0:
                _norm_rope_to_slabs(xbuf, ((g - 1) % 2) * tn, res_ref, qgain_ref, cos_ref, sin_ref)
                if dilations:
                    _emit_slabs(res_ref, q_refs[g - 1], dilations[g - 1])
                else:
                    _emit_slabs(res_ref, q_refs[0], 1)
            if g < n_groups:
                xbuf[:, (g % 2) * tn:(g % 2 + 1) * tn] = proj


def _q_proj(x, gain, w_q, g_qnorm, cosf, sinf, layer_mix, layer_b, seq, dilations=()):
    m, d = x.shape
    n_groups = g_qnorm.shape[1]
    tn = w_q.shape[2] // n_groups
    tm = min(m, PROJ_ROW_TILE)
    tab_spec = pl.BlockSpec((tm, LANES), lambda i, j: (i, 0))
    if dilations:
        assert len(dilations) == n_groups
        out_shape = [jax.ShapeDtypeStruct((m // seq, seq // dil, dil * tn), BF16) for dil in dilations]
        out_specs = [_class_spec(tm, tn, seq, dil) for dil in dilations]
    else:
        out_shape = [jax.ShapeDtypeStruct((m, n_groups * tn), BF16)]
        out_specs = [pl.BlockSpec((tm, tn), lambda i, j: (i, jnp.maximum(j - 1, 0)))]
    assert d >= 2 * tn
    return pl.pallas_call(
        functools.partial(_q_proj_body, tm=tm, n_groups=n_groups, dilations=tuple(dilations)),
        out_shape=out_shape,
        grid=(m // tm, n_groups + 1),
        in_specs=[pl.BlockSpec(memory_space=pl.ANY),
                  pl.BlockSpec((None, 1, d), lambda i, j: (layer_mix, 0, 0)),
                  pl.BlockSpec((None, d, n_groups * tn), lambda i, j: (layer_b, 0, 0),
                               pipeline_mode=pl.Buffered(1)),
                  pl.BlockSpec((None, None, 1, ATT_HEAD_DIM), lambda i, j: (layer_b, jnp.maximum(j - 1, 0), 0, 0)),
                  tab_spec, tab_spec],
        out_specs=out_specs,
        scratch_shapes=[pltpu.VMEM((tm, d), F32), pltpu.VMEM((tm, d), BF16),
                        pltpu.VMEM((tn // ATT_HEAD_DIM, tm, ATT_HEAD_DIM), F32),
                        pltpu.SemaphoreType.DMA((LOAD_CHUNKS,))],
        compiler_params=_params(2),
        name="q_proj",
    )(x, gain.reshape(gain.shape[0], 1, d), w_q,
      g_qnorm.reshape(g_qnorm.shape[0], n_groups, 1, ATT_HEAD_DIM), cosf, sinf)


def _softmax_pieces(scores, masks):
    masked = [jnp.where(mk, sc, -jnp.inf) for sc, mk in zip(scores, masks)]
    m = functools.reduce(jnp.maximum, [jnp.max(sc, axis=1, keepdims=True) for sc in masked])
    probs = [jnp.exp(sc - m) for sc in masked]
    l = functools.reduce(jnp.add, [jnp.sum(p, axis=1, keepdims=True) for p in probs])
    return probs, l, m


def _attn_prompt_body(*refs, tq, win, dil, has_prev, n_heads):
    if has_prev:
        q_ref, kc_ref, vc_ref, kp_ref, vp_ref, o_ref, lse_ref = refs
    else:
        q_ref, kc_ref, vc_ref, o_ref, lse_ref = refs
    qi = pl.program_id(1)
    res = pl.program_id(2)
    scale = ATT_HEAD_DIM ** -0.5
    row = lax.broadcasted_iota(jnp.int32, (tq, tq), 0)
    col = lax.broadcasted_iota(jnp.int32, (tq, tq), 1)
    mask_c = (col <= row) & (row - col <= win)
    if has_prev:
        rowp = lax.broadcasted_iota(jnp.int32, (tq, win), 0)
        colp = lax.broadcasted_iota(jnp.int32, (tq, win), 1)
        mask_p = (colp >= rowp) & (qi > 0)
    rows = slice(None) if dil == 1 else pl.ds(res, tq, stride=dil)
    heads = [slice(hh * ATT_HEAD_DIM, (hh + 1) * ATT_HEAD_DIM) for hh in range(n_heads)]
    masks = [mask_c, mask_p] if has_prev else [mask_c]
    scores = []
    for hs in heads:
        qh = q_ref[:, hs]
        sc = [_dot_nt(qh, _as_bf16(kc_ref[:, hs])) * scale]
        if has_prev:
            sc.append(_dot_nt(qh, _as_bf16(kp_ref[:, hs])) * scale)
        scores.append(sc)
    soft = [_softmax_pieces(sc, masks) for sc in scores]
    for hh, (hs, (probs, l, m)) in enumerate(zip(heads, soft)):
        values = [_as_bf16(vc_ref[:, hs])] + ([_as_bf16(vp_ref[:, hs])] if has_prev else [])
        acc = functools.reduce(jnp.add, [jnp.dot(p.astype(BF16), v, preferred_element_type=F32)
                                         for p, v in zip(probs, values)])
        o_ref[hh, rows, :] = acc / l
        lse_ref[hh, rows, :] = jnp.broadcast_to(m + jnp.log(l), (tq, ATT_HEAD_DIM))


def _attn_prompt_group(q, k, v, seq, window, dil):
    b = q.shape[0]
    n = q.shape[2] // dil
    n_heads = n // ATT_HEAD_DIM
    tc = seq // dil
    win = window // dil
    tq = min(tc, ATTN_Q_TILE)
    has_prev = tc > tq
    assert tq % win == 0 and tc % tq == 0
    cur = pl.BlockSpec((None, tq, n), lambda bi, qi, r: (bi, qi, r))
    prev = pl.BlockSpec((None, win, n), lambda bi, qi, r: (bi, jnp.maximum(qi * (tq // win) - 1, 0), r))
    out = pl.BlockSpec((None, n_heads, tq * dil, ATT_HEAD_DIM), lambda bi, qi, r: (bi, 0, qi, 0))
    return pl.pallas_call(
        functools.partial(_attn_prompt_body, tq=tq, win=win, dil=dil, has_prev=has_prev, n_heads=n_heads),
        out_shape=(jax.ShapeDtypeStruct((b, n_heads, seq, ATT_HEAD_DIM), F32),) * 2,
        grid=(b, tc // tq, dil),
        in_specs=[cur, cur, cur] + ([prev, prev] if has_prev else []),
        out_specs=(out, out),
        compiler_params=_params(3),
        name=f"attn_prompt_d{dil}",
    )(q, k, v, *([k, v] if has_prev else []))


def _attn_out_body(*refs, n_groups):
    o_refs = refs[:n_groups]
    l_refs = refs[n_groups:2 * n_groups]
    w_ref, r_ref, out_ref, att_ref = refs[2 * n_groups:]

    @pl.when(pl.program_id(1) == 0)
    def _mix():
        for hh in range(o_refs[0].shape[0]):
            lses = [l[hh] for l in l_refs]
            m = functools.reduce(jnp.maximum, lses)
            es = [jnp.exp(l - m) for l in lses]
            tot = functools.reduce(jnp.add, es)
            att = functools.reduce(jnp.add, [(e / tot) * o[hh] for e, o in zip(es, o_refs)])
            att_ref[:, hh * ATT_HEAD_DIM:(hh + 1) * ATT_HEAD_DIM] = att.astype(BF16)

    out_ref[...] = r_ref[...] + jnp.dot(att_ref[...], w_ref[...], preferred_element_type=F32)


def _attn_out(outs, lses, w_o, resid, layer, *, tm=256):
    b, n_heads, seq, hd = outs[0].shape
    m, n = resid.shape
    k = n_heads * hd
    tn = n
    tm = min(seq, tm)
    tiles = seq // tm
    blk = pl.BlockSpec((None, n_heads, tm, hd), lambda i, j: (i // tiles, 0, i % tiles, 0))
    return pl.pallas_call(
        functools.partial(_attn_out_body, n_groups=len(outs)),
        out_shape=jax.ShapeDtypeStruct((m, n), F32),
        grid=(m // tm, n // tn),
        in_specs=[blk] * (2 * len(outs)) + [pl.BlockSpec((None, k, tn), lambda i, j: (layer, 0, j)),
                                            pl.BlockSpec((tm, tn), lambda i, j: (i, j))],
        out_specs=pl.BlockSpec((tm, tn), lambda i, j: (i, j)),
        scratch_shapes=[pltpu.VMEM((tm, k), BF16)],
        compiler_params=_params(2),
        name="attn_out",
    )(*outs, *lses, w_o, resid)


def _attn_sample_body(q_ref, kc_ref, vc_ref, kn_ref, vn_ref, o_ref, *, groups, n_heads, n_new):
    rows = q_ref.shape[0]
    cache = kc_ref.shape[0] // n_heads
    n = n_heads * ATT_HEAD_DIM
    scale = ATT_HEAD_DIM ** -0.5
    dist_c = cache + lax.broadcasted_iota(jnp.int32, (rows, cache), 0) \
        - lax.broadcasted_iota(jnp.int32, (rows, cache), 1)
    dist_n = lax.broadcasted_iota(jnp.int32, (rows, rows), 0) - lax.broadcasted_iota(jnp.int32, (rows, rows), 1)
    col_n = lax.broadcasted_iota(jnp.int32, (rows, rows), 1)
    masks = [[(dist_c <= window) & ((dist_c & (dil - 1)) == 0),
              (dist_n >= 0) & (col_n < n_new) & ((dist_n & (dil - 1)) == 0)] for window, dil in groups]
    heads = [slice(hh * ATT_HEAD_DIM, (hh + 1) * ATT_HEAD_DIM) for hh in range(n_heads)]
    scores = []
    for hh, hs in enumerate(heads):
        kc = kc_ref[pl.ds(hh, cache, stride=n_heads), :].astype(BF16)
        kn = kn_ref[:, hs].astype(BF16)
        for g in range(len(groups)):
            qh = q_ref[:, g * n + hh * ATT_HEAD_DIM:g * n + (hh + 1) * ATT_HEAD_DIM]
            scores.append([_dot_nt(qh, kc) * scale, _dot_nt(qh, kn) * scale])
    soft = [_softmax_pieces(sc, masks[u % len(groups)]) for u, sc in enumerate(scores)]
    for hh, hs in enumerate(heads):
        vc = vc_ref[pl.ds(hh, cache, stride=n_heads), :].astype(BF16)
        vn = vn_ref[:, hs].astype(BF16)
        outs, lses = [], []
        for probs, l, m in soft[hh * len(groups):(hh + 1) * len(groups)]:
            acc = jnp.dot(probs[0].astype(BF16), vc, preferred_element_type=F32) \
                + jnp.dot(probs[1].astype(BF16), vn, preferred_element_type=F32)
            outs.append(acc / l)
            lses.append(m + jnp.log(l))
        m_all = functools.reduce(jnp.maximum, lses)
        es = [jnp.exp(l - m_all) for l in lses]
        tot = functools.reduce(jnp.add, es)
        o_ref[:, hs] = functools.reduce(jnp.add, [(e / tot) * o for e, o in zip(es, outs)]).astype(o_ref.dtype)


def _attn_sample(q, k_new, v_new, cache_k, cache_v, n_new):
    b, rows, n = k_new.shape
    cache, n_heads = cache_k.shape[1], cache_k.shape[2]
    assert cache == MAX_WINDOW and all(d & (d - 1) == 0 for _, d in DILATED_GROUPS)
    row_blk = lambda width: pl.BlockSpec((None, rows, width), lambda bi: (bi, 0, 0))
    cache_blk = pl.BlockSpec((None, cache * n_heads, ATT_HEAD_DIM), lambda bi: (bi, 0, 0))
    cache_k = cache_k.reshape(b, cache * n_heads, ATT_HEAD_DIM)
    cache_v = cache_v.reshape(b, cache * n_heads, ATT_HEAD_DIM)
    return pl.pallas_call(
        functools.partial(_attn_sample_body, groups=DILATED_GROUPS, n_heads=n // ATT_HEAD_DIM, n_new=n_new),
        out_shape=jax.ShapeDtypeStruct((b, rows, n), BF16),
        grid=(b,),
        in_specs=[row_blk(q.shape[2]), cache_blk, cache_blk, row_blk(n), row_blk(n)],
        out_specs=row_blk(n),
        compiler_params=_params(1),
        name="attn_sample",
    )(q, cache_k, cache_v, k_new, v_new)


FFN_ORDER = (("ffn1", 0), ("ffn2", 0), ("ffn1", 1), ("ffn2", 1))
SMALL_WEIGHTS = (("w_in_a_t", True), ("w_out_a", False), ("w_kv", False), ("w_q_b", False), ("w_o_b", False))


def _ffn_stacks(p, step):
    name, _ = FFN_ORDER[step]
    return tuple(p[f"w_{name}_{part}"] for part in ("gate", "up", "down"))


def _run_ffn(x, p, bf16, step):
    name, layer = FFN_ORDER[step]
    if ("ffn", step) not in bf16:
        x, bf16[("ffn", step)], _ = _ffn(x, p[f"norm_{name}"], layer, _ffn_stacks(p, step), layer, emit=True)
        return x
    casts, keys = [], []
    if step + 1 < len(FFN_ORDER) and ("ffn", step + 1) not in bf16:
        gate, up, down = _ffn_stacks(p, step + 1)
        nxt_layer = FFN_ORDER[step + 1][1]
        casts += [(gate, nxt_layer, False), (up, nxt_layer, False), (down, nxt_layer, True)]
        keys += [("ffn", step + 1)] * 3
    for key, rows_follow_j in SMALL_WEIGHTS:
        if key not in bf16:
            assert p[key].shape[0] == 1
            casts.append((p[key], 0, rows_follow_j))
            keys.append(key)
    x, _, cast = _ffn(x, p[f"norm_{name}"], layer, bf16[("ffn", step)], casts=tuple(casts))
    for key, arr in zip(keys, cast):
        if key[0] == "ffn":
            bf16[key] = bf16.get(key, ()) + (arr,)
        else:
            bf16[key] = arr
    return x


def _trunk(x, seq, base, cache_k, cache_v, c0, n0, m0, p, bf16, first_ffn_done=False):
    m_rows, d = x.shape
    b = m_rows // seq
    nh = MLSTM_HEADS
    dk, dv = c0.shape[2], c0.shape[3]
    fresh = cache_k is None
    dilations = tuple(dil for _, dil in DILATED_GROUPS)

    if not first_ffn_done:
        x = _run_ffn(x, p, bf16, 0)
    proj, gates = _in_proj(x, p["norm_mix"], bf16["w_in_a_t"], 2 * nh * dk + nh * dv + d, 0)
    if seq % MLSTM_PROMPT_CHUNK == 0:
        chunk, valid, t_pad = MLSTM_PROMPT_CHUNK, MLSTM_PROMPT_CHUNK, seq
    else:
        assert seq <= BF16_SUBLANES
        chunk, valid, t_pad = BF16_SUBLANES, seq, BF16_SUBLANES
    proj3 = proj.reshape(b, seq, -1)
    gates3 = gates.reshape(b, seq, LANES)[:, :, :2 * nh]
    if t_pad != seq:
        proj3 = jnp.pad(proj3, ((0, 0), (0, t_pad - seq), (0, 0)))
        gates3 = jnp.pad(gates3, ((0, 0), (0, t_pad - seq), (0, 0)))
    gates_row = gates3.reshape(b, t_pad, 2, nh).transpose(0, 3, 2, 1)
    y, c_fin, n_fin, m_fin = _mlstm(proj3, gates_row, p["b_ig"][0], p["b_fg"][0], p["g_mh"][0],
                                    c0, n0, m0, chunk=chunk, valid=valid)
    y = y[:, :seq].reshape(m_rows, nh * dv)
    x = _matmul_resid(y, bf16["w_out_a"], x, 0)
    x = _run_ffn(x, p, bf16, 1)

    cosf, sinf = _rope_tables(m_rows, seq, base)
    cls_dils = dilations if fresh else ()
    k, v, k_cls, v_cls = _kv_proj(x, p["norm_kv"], bf16["w_kv"], p["g_knorm"], cosf, sinf, seq, cls_dils)

    x = _run_ffn(x, p, bf16, 2)
    q = _q_proj(x, p["norm_mix"], bf16["w_q_b"], p["g_qnorm"], cosf, sinf, 1, 0, seq, dilations if fresh else ())
    n_kv = k.shape[0] // m_rows * k.shape[1]
    if fresh:
        outs, lses = [], []
        for q_g, k_g, v_g, (window, dil) in zip(q, k_cls, v_cls, DILATED_GROUPS):
            o_g, l_g = _attn_prompt_group(q_g, k_g, v_g, seq, window, dil)
            outs.append(o_g)
            lses.append(l_g)
        x = _attn_out(outs, lses, bf16["w_o_b"], x, 0)
    else:
        pad = ((0, 0), (0, BF16_SUBLANES - seq), (0, 0))
        att = _attn_sample(jnp.pad(q[0].reshape(b, seq, -1), pad), jnp.pad(k.reshape(b, seq, n_kv), pad),
                           jnp.pad(v.reshape(b, seq, n_kv), pad), cache_k, cache_v, seq)
        x = _matmul_resid(att[:, :seq].reshape(m_rows, n_kv), bf16["w_o_b"], x, 0)
    x = _run_ffn(x, p, bf16, 3)
    return x, k, v, c_fin, n_fin, m_fin


def kernel(x_prompt, x_sample, cache_k, cache_v, state_C, state_n, state_m, norm_ffn1, w_ffn1_gate, w_ffn1_up, w_ffn1_down, norm_mix, norm_ffn2, w_ffn2_gate, w_ffn2_up, w_ffn2_down, w_in_a, b_ig, b_fg, g_mh, w_out_a, norm_kv, w_kv, g_knorm, w_q_b, g_qnorm, w_o_b):
    bp, seq_p, d = x_prompt.shape
    bs, seq_s, _ = x_sample.shape
    past = cache_k.shape[1]
    assert past == min(MAX_WINDOW, PAST_LEN)
    nh = MLSTM_HEADS
    dk, dv = state_C.shape[3], state_C.shape[4]

    p = dict(norm_ffn1=norm_ffn1, norm_mix=norm_mix, norm_ffn2=norm_ffn2, norm_kv=norm_kv,
             w_ffn1_gate=w_ffn1_gate, w_ffn1_up=w_ffn1_up, w_ffn1_down=w_ffn1_down,
             w_ffn2_gate=w_ffn2_gate, w_ffn2_up=w_ffn2_up, w_ffn2_down=w_ffn2_down,
             w_in_a_t=jnp.swapaxes(w_in_a, 1, 2), b_ig=b_ig, b_fg=b_fg, g_mh=g_mh, w_out_a=w_out_a, w_kv=w_kv[None],
             g_knorm=g_knorm, w_q_b=w_q_b, g_qnorm=g_qnorm, w_o_b=w_o_b)
    bf16 = {}
    xs = _run_ffn(x_sample.reshape(bs * seq_s, d), p, bf16, 0)

    zeros = lambda *s: jnp.zeros(s, F32)
    y_p, k_p, v_p, c_p, n_p, m_p = _trunk(
        x_prompt.reshape(bp * seq_p, d), seq_p, 0, None, None,
        zeros(bp, nh, dk, dv), zeros(bp, nh, 1, dk), zeros(bp, nh, 1, LANES), p, bf16)
    y_s, k_s, v_s, c_s, n_s, m_s = _trunk(
        xs, seq_s, PAST_LEN, cache_k, cache_v,
        state_C[:, 0], state_n[:, 0][:, :, None, :],
        jnp.broadcast_to(state_m[:, 0][:, :, None, None], (bs, nh, 1, LANES)), p, bf16, first_ffn_done=True)

    hkv, hd = cache_k.shape[2], cache_k.shape[3]
    keep = min(MAX_WINDOW, seq_p)
    return (y_p.reshape(bp, seq_p, d), y_s.reshape(bs, seq_s, d),
            k_p.reshape(bp, seq_p, hkv, hd)[:, seq_p - keep:], v_p.reshape(bp, seq_p, hkv, hd)[:, seq_p - keep:],
            k_s.reshape(bs, seq_s, hkv, hd), v_s.reshape(bs, seq_s, hkv, hd),
            c_p[:, None], n_p[:, None, :, 0], m_p[:, None, :, 0, 0],
            c_s[:, None], n_s[:, None, :, 0], m_s[:, None, :, 0, 0])
```

```python
import functools

import jax
import jax.numpy as jnp
from jax import lax
from jax.experimental import pallas as pl
from jax.experimental.pallas import tpu as pltpu

F32 = jnp.float32
BF16 = jnp.bfloat16

EPS = 1e-6
MLSTM_HEADS = 8
ATT_HEAD_DIM = 128
DILATED_GROUPS = ((128, 1), (512, 4), (2048, 16))
MAX_WINDOW = max(w for w, _ in DILATED_GROUPS)
ROT_DIM = ATT_HEAD_DIM // 4
ROPE_THETA = 500000.0
PAST_LEN = 16384

V7X_VMEM_LIMIT_BYTES = 56 * 1024 * 1024
LANES = 128
BF16_SUBLANES = 16
NORM_ROWS = 16
NORM_UNROLL = 4
LOAD_CHUNKS = 8
EPI_ROWS = 128
PROMPT_ROW_TILE = 1024
PROJ_ROW_TILE = 512
MLSTM_PROMPT_CHUNK = 256
MLSTM_HEADS_PER_STEP = 4
ATTN_Q_TILE = 256


def _params(n_axes):
    return pltpu.CompilerParams(dimension_semantics=("arbitrary",) * n_axes,
                                vmem_limit_bytes=V7X_VMEM_LIMIT_BYTES)


def _row_tile(m):
    return min(m, PROMPT_ROW_TILE)


def _round_up(x, mult):
    return -(-x // mult) * mult


def _rmsnorm_rows(src_ref, gain_ref, dst_ref, rows):
    step = min(rows, NORM_ROWS)

    def body(r, carry):
        rs = pl.ds(pl.multiple_of(r * step, step), step)
        ms = jnp.mean(jnp.square(src_ref[rs, :]), axis=-1, keepdims=True)
        dst_ref[rs, :] = (src_ref[rs, :] * lax.rsqrt(ms + EPS) * gain_ref[...]).astype(BF16)
        return carry

    trips = rows // step
    lax.fori_loop(0, trips, body, 0, unroll=min(trips, NORM_UNROLL))


def _load_norm_rows(x_hbm, buf_ref, gain_ref, xn_ref, sems, row0, rows):
    n = LOAD_CHUNKS if rows % (LOAD_CHUNKS * NORM_ROWS) == 0 else 1
    piece = rows // n
    copies = [pltpu.make_async_copy(x_hbm.at[pl.ds(row0 + c * piece, piece)],
                                    buf_ref.at[pl.ds(c * piece, piece)], sems.at[c]) for c in range(n)]
    for cp in copies:
        cp.start()
    for c, cp in enumerate(copies):
        cp.wait()
        _rmsnorm_rows(buf_ref.at[pl.ds(c * piece, piece)], gain_ref, xn_ref.at[pl.ds(c * piece, piece)], piece)


def _head_norm(xh, gain):
    return xh * lax.rsqrt(jnp.mean(xh * xh, axis=-1, keepdims=True) + EPS) * gain


def _head_rope(y, cosf, sinf):
    half = ROT_DIM // 2
    lane = lax.broadcasted_iota(jnp.int32, y.shape, 1)
    from_lower = pltpu.roll(y, half, axis=1)
    from_upper = pltpu.roll(y, LANES - half, axis=1)
    rot = jnp.where(lane < half, -from_upper, from_lower)
    return y * cosf + rot * sinf


def _as_bf16(x):
    return x if x.dtype == BF16 else x.astype(BF16)


def _dot_nt(a, b):
    return lax.dot_general(a, b, (((1,), (1,)), ((), ())), preferred_element_type=F32)


def _cast_specs(src, layer, grid, rows_follow_j):
    gi, gj = grid
    _, r, c = src.shape
    if rows_follow_j:
        rb, cb = _round_up(pl.cdiv(r, gj), BF16_SUBLANES), c // gi
        assert c % gi == 0 and cb % LANES == 0
        last = pl.cdiv(r, rb) - 1
        index = lambda lay: (lambda i, j: (lay, jnp.minimum(j, last), i))
    else:
        rb, cb = r // gi, _round_up(pl.cdiv(c, gj), LANES)
        assert r % gi == 0 and rb % BF16_SUBLANES == 0
        last = pl.cdiv(c, cb) - 1
        index = lambda lay: (lambda i, j: (lay, i, jnp.minimum(j, last)))
    return (pl.BlockSpec((None, rb, cb), index(layer)), pl.BlockSpec((None, rb, cb), index(0)),
            jax.ShapeDtypeStruct((1, r, c), BF16))


def _ffn_body(*refs, tm, n_chunk, n_casts, emit):
    x_hbm, gain_ref, wg_ref, wu_ref, wd_ref = refs[:5]
    cast_in = refs[5:5 + n_casts]
    o_hbm = refs[5 + n_casts]
    n_emit = 3 if emit else 0
    emit_refs = refs[6 + n_casts:6 + n_casts + n_emit]
    cast_out = refs[6 + n_casts + n_emit:6 + 2 * n_casts + n_emit]
    acc_ref, xn_ref, sem = refs[6 + 2 * n_casts + n_emit:]

    for src, dst in zip(cast_in, cast_out):
        dst[...] = src[...].astype(BF16)

    i = pl.program_id(0)
    j = pl.program_id(1)
    row0 = pl.multiple_of(i * tm, tm)

    @pl.when(j == 0)
    def _load_and_norm():
        _load_norm_rows(x_hbm, acc_ref, gain_ref, xn_ref, sem, row0, tm)

    wg, wu, wd = wg_ref[...], wu_ref[...], wd_ref
    if emit:
        wg, wu, wd = wg.astype(BF16), wu.astype(BF16), wd_ref[...].astype(BF16)
        emit_refs[0][...] = wg
        emit_refs[1][...] = wu
        emit_refs[2][...] = wd
    xn = xn_ref[...]
    g = jnp.dot(xn, wg, preferred_element_type=F32)
    u = jnp.dot(xn, wu, preferred_element_type=F32)
    a = (0.5 * (g * jax.nn.sigmoid(g)) * u).astype(BF16)
    d = acc_ref.shape[1]
    for c in range(d // n_chunk):
        cs = slice(c * n_chunk, (c + 1) * n_chunk)
        acc_ref[:, cs] += jnp.dot(a, wd[:, cs], preferred_element_type=F32)

    @pl.when(j == pl.num_programs(1) - 1)
    def _store():
        cp = pltpu.make_async_copy(acc_ref, o_hbm.at[pl.ds(row0, tm)], sem.at[LOAD_CHUNKS])
        cp.start()
        cp.wait()


def _ffn(x, gain, layer, weights, w_layer=0, casts=(), emit=False, *, tf=256):
    wg, wu, wd = weights
    m, d = x.shape
    f = wg.shape[2]
    tm = _row_tile(m)
    grid = (m // tm, f // tf)
    assert not emit or grid[0] == 1
    in_specs = [
        pl.BlockSpec(memory_space=pl.ANY),
        pl.BlockSpec((None, 1, d), lambda i, j: (layer, 0, 0)),
        pl.BlockSpec((None, d, tf), lambda i, j: (w_layer, 0, j)),
        pl.BlockSpec((None, d, tf), lambda i, j: (w_layer, 0, j)),
        pl.BlockSpec((None, tf, d), lambda i, j: (w_layer, j, 0)),
    ]
    out_shape = [jax.ShapeDtypeStruct((m, d), F32)]
    out_specs = [pl.BlockSpec(memory_space=pl.ANY)]
    if emit:
        out_shape += [jax.ShapeDtypeStruct((1, d, f), BF16)] * 2 + [jax.ShapeDtypeStruct((1, f, d), BF16)]
        out_specs += [pl.BlockSpec((None, d, tf), lambda i, j: (0, 0, j))] * 2 \
            + [pl.BlockSpec((None, tf, d), lambda i, j: (0, j, 0))]
    for src, src_layer, rows_follow_j in casts:
        in_spec, out_spec, shape = _cast_specs(src, src_layer, grid, rows_follow_j)
        in_specs.append(in_spec)
        out_specs.append(out_spec)
        out_shape.append(shape)
    outs = pl.pallas_call(
        functools.partial(_ffn_body, tm=tm, n_chunk=min(d, 512), n_casts=len(casts), emit=emit),
        out_shape=out_shape,
        grid=grid,
        in_specs=in_specs,
        out_specs=out_specs,
        scratch_shapes=[pltpu.VMEM((tm, d), F32), pltpu.VMEM((tm, d), BF16),
                        pltpu.SemaphoreType.DMA((LOAD_CHUNKS + 1,))],
        compiler_params=_params(2),
        name="ffn",
    )(x, gain.reshape(gain.shape[0], 1, d), wg, wu, wd, *[src for src, _, _ in casts])
    n_emit = 3 if emit else 0
    return outs[0], tuple(outs[1:1 + n_emit]), tuple(outs[1 + n_emit:])


def _in_proj_body(x_hbm, gain_ref, w_ref, wgate_ref, proj_ref, gates_ref, xbuf, xn_ref, sem, *, tm, n_gates):
    i = pl.program_id(0)
    j = pl.program_id(1)

    @pl.when(j == 0)
    def _load_and_norm():
        _load_norm_rows(x_hbm, xbuf, gain_ref, xn_ref, sem, pl.multiple_of(i * tm, tm), tm)
        row = lax.broadcasted_iota(jnp.int32, wgate_ref.shape, 0)
        wgate = jnp.where(row < n_gates, wgate_ref[...], jnp.zeros_like(wgate_ref))
        gates_ref[...] = _dot_nt(xn_ref[...], wgate)

    proj_ref[...] = _dot_nt(xn_ref[...], w_ref[...])


def _in_proj(x, gain, w_in_t, n_main, layer, *, tn=512):
    m, d = x.shape
    n_gates = w_in_t.shape[1] - n_main
    assert n_main % tn == 0 and n_main % LANES == 0 and 0 < n_gates < LANES
    tm = _row_tile(m)
    return pl.pallas_call(
        functools.partial(_in_proj_body, tm=tm, n_gates=n_gates),
        out_shape=(jax.ShapeDtypeStruct((m, n_main), F32), jax.ShapeDtypeStruct((m, LANES), F32)),
        grid=(m // tm, n_main // tn),
        in_specs=[
            pl.BlockSpec(memory_space=pl.ANY),
            pl.BlockSpec((None, 1, d), lambda i, j: (layer, 0, 0)),
            pl.BlockSpec((None, tn, d), lambda i, j: (layer, j, 0)),
            pl.BlockSpec((None, LANES, d), lambda i, j: (layer, n_main // LANES, 0)),
        ],
        out_specs=(pl.BlockSpec((tm, tn), lambda i, j: (i, j)),
                   pl.BlockSpec((tm, LANES), lambda i, j: (i, 0))),
        scratch_shapes=[pltpu.VMEM((tm, d), F32), pltpu.VMEM((tm, d), BF16),
                        pltpu.SemaphoreType.DMA((LOAD_CHUNKS,))],
        compiler_params=_params(2),
        name="in_proj",
    )(x, gain.reshape(gain.shape[0], 1, d), w_in_t, w_in_t)


def _mlstm_body(bi_ref, bf_ref, q_ref, k_ref, v_ref, o_ref, g_ref, gmh_ref, c0_ref, n0_ref, m0_ref,
                y_ref, c_ref, n_ref, m_ref, *, chunk, valid, heads):
    hg = pl.program_id(1)
    c = pl.program_id(2)
    L = chunk
    dk = q_ref.shape[1] // heads
    dv = v_ref.shape[1] // heads

    @pl.when(c == 0)
    def _init():
        c_ref[...] = c0_ref[...]
        n_ref[...] = n0_ref[...]
        m_ref[...] = m0_ref[...]

    t_idx = lax.broadcasted_iota(jnp.int32, (L, L), 0)
    s_idx = lax.broadcasted_iota(jnp.int32, (L, L), 1)
    causal = s_idx <= t_idx
    diag = s_idx == t_idx
    row = lax.broadcasted_iota(jnp.int32, (L, 1), 0)

    for e in range(heads):
        h = hg * heads + e
        qs = slice(e * dk, (e + 1) * dk)
        vs = slice(e * dv, (e + 1) * dv)
        gi_row = g_ref[e, 0:1, :] + bi_ref[h]
        lf_row = jax.nn.log_sigmoid(g_ref[e, 1:2, :] + bf_ref[h])
        lf_b = jnp.broadcast_to(lf_row, (L, L))
        gi_b = jnp.broadcast_to(gi_row, (L, L))
        lf_col = jnp.sum(jnp.where(diag, lf_b, 0.0), axis=1, keepdims=True)
        gi_col = jnp.sum(jnp.where(diag, gi_b, 0.0), axis=1, keepdims=True)
        b_col = jnp.sum(jnp.where(causal, lf_b, 0.0), axis=1, keepdims=True)
        b_row = jnp.sum(jnp.where(t_idx <= s_idx, jnp.broadcast_to(lf_col, (L, L)), 0.0),
                        axis=0, keepdims=True)

        log_d = jnp.where(causal, b_col + (gi_row - b_row), -jnp.inf)
        m_prev = m_ref[e, 0:1, 0:1]
        inter = b_col + m_prev
        m_t = jnp.maximum(inter, jnp.max(log_d, axis=1, keepdims=True))
        decay = jnp.exp(log_d - m_t)

        qf = q_ref[:, qs]
        kf = k_ref[:, qs] * (dk ** -0.5)
        qb = qf.astype(BF16)
        kb = kf.astype(BF16)
        vb = v_ref[:, vs].astype(BF16)
        s = _dot_nt(qb, kb) * decay
        sc = jnp.exp(inter - m_t)
        num = jnp.dot(s.astype(BF16), vb, preferred_element_type=F32) \
            + sc * jnp.dot(qb, c_ref[e].astype(BF16), preferred_element_type=F32)
        den = jnp.sum(s, axis=1, keepdims=True) + sc * jnp.sum(qf * n_ref[e], axis=1, keepdims=True)
        hc = num / jnp.maximum(jnp.abs(den), jnp.exp(-m_t))
        hn = hc * lax.rsqrt(jnp.mean(hc * hc, axis=1, keepdims=True) + EPS) * gmh_ref[e]
        y_ref[:, vs] = (jax.nn.sigmoid(o_ref[:, vs]) * hn).astype(y_ref.dtype)

        m_new = m_t[valid - 1:valid, :]
        b_last = b_col[valid - 1:valid, :]
        dec = jnp.exp(b_last + m_prev - m_new)
        ws = jnp.where(row < valid, jnp.exp(b_last - b_col + gi_col - m_new), 0.0)
        kw = ws * kf
        c_ref[e] = dec * c_ref[e] + lax.dot_general(kw.astype(BF16), vb, (((0,), (0,)), ((), ())),
                                                    preferred_element_type=F32)
        n_ref[e] = dec * n_ref[e] + jnp.sum(kw, axis=0, keepdims=True)
        m_ref[e] = jnp.broadcast_to(m_new, m_ref.shape[1:])


def _mlstm(proj, gates_row, b_i, b_f, g_mh, c0, n0, m0, *, chunk, valid):
    b, t, _ = proj.shape
    nh, dk, dv = c0.shape[1], c0.shape[2], c0.shape[3]
    hps = MLSTM_HEADS_PER_STEP
    assert nh % hps == 0 and (2 * nh * dk) % (hps * dv) == 0
    k_blk0 = nh // hps
    v_blk0 = 2 * nh * dk // (hps * dv)
    o_blk0 = v_blk0 + nh // hps
    smem = pl.BlockSpec(memory_space=pltpu.SMEM)
    state = lambda r, cdim: pl.BlockSpec((None, hps, r, cdim), lambda bi, hi, ci: (bi, hi, 0, 0))
    return pl.pallas_call(
        functools.partial(_mlstm_body, chunk=chunk, valid=valid, heads=hps),
        out_shape=(jax.ShapeDtypeStruct((b, t, nh * dv), BF16),
                   jax.ShapeDtypeStruct(c0.shape, F32),
                   jax.ShapeDtypeStruct(n0.shape, F32),
                   jax.ShapeDtypeStruct(m0.shape, F32)),
        grid=(b, nh // hps, t // chunk),
        in_specs=[
            smem, smem,
            pl.BlockSpec((None, chunk, hps * dk), lambda bi, hi, ci: (bi, ci, hi)),
            pl.BlockSpec((None, chunk, hps * dk), lambda bi, hi, ci: (bi, ci, k_blk0 + hi)),
            pl.BlockSpec((None, chunk, hps * dv), lambda bi, hi, ci: (bi, ci, v_blk0 + hi)),
            pl.BlockSpec((None, chunk, hps * dv), lambda bi, hi, ci: (bi, ci, o_blk0 + hi)),
            pl.BlockSpec((None, hps, 2, chunk), lambda bi, hi, ci: (bi, hi, 0, ci)),
            pl.BlockSpec((hps, 1, dv), lambda bi, hi, ci: (hi, 0, 0)),
            state(dk, dv), state(1, dk), state(1, LANES),
        ],
        out_specs=(pl.BlockSpec((None, chunk, hps * dv), lambda bi, hi, ci: (bi, ci, hi)),
                   state(dk, dv), state(1, dk), state(1, LANES)),
        compiler_params=_params(3),
        name="mlstm",
    )(b_i, b_f, proj, proj, proj, proj, gates_row, g_mh.reshape(nh, 1, dv), c0, n0, m0)


def _matmul_resid_body(a_ref, w_ref, r_ref, o_ref):
    o_ref[...] = r_ref[...] + jnp.dot(a_ref[...], w_ref[...], preferred_element_type=F32)


def _matmul_resid(a, w, resid, layer, *, tn=512):
    m, k = a.shape
    n = w.shape[2]
    tm = _row_tile(m)
    return pl.pallas_call(
        _matmul_resid_body,
        out_shape=jax.ShapeDtypeStruct((m, n), F32),
        grid=(m // tm, n // tn),
        in_specs=[pl.BlockSpec((tm, k), lambda i, j: (i, 0)),
                  pl.BlockSpec((None, k, tn), lambda i, j: (layer, 0, j)),
                  pl.BlockSpec((tm, tn), lambda i, j: (i, j))],
        out_specs=pl.BlockSpec((tm, tn), lambda i, j: (i, j)),
        compiler_params=_params(2),
        name="matmul_resid",
    )(a, w, resid)


def _rope_table_body(invf_ref, cos_ref, sin_ref, *, seq, base):
    rows = cos_ref.shape[0]
    m = lax.broadcasted_iota(jnp.int32, (rows, LANES), 0) + pl.program_id(0) * rows
    pos = (base + (m & (seq - 1))).astype(F32)
    ang = pos * invf_ref[...]
    cos_ref[...] = jnp.cos(ang)
    sin_ref[...] = jnp.sin(ang)


def _rope_tables(m, seq, base):
    assert seq & (seq - 1) == 0
    half = ROT_DIM // 2
    inv_freq = ROPE_THETA ** (-jnp.arange(half, dtype=F32) / half)
    invf = jnp.zeros((1, LANES), F32).at[0, :ROT_DIM].set(jnp.tile(inv_freq, 2))
    rows = min(m, 512)
    spec = pl.BlockSpec((rows, LANES), lambda i: (i, 0))
    return pl.pallas_call(
        functools.partial(_rope_table_body, seq=seq, base=base),
        out_shape=(jax.ShapeDtypeStruct((m, LANES), F32),) * 2,
        grid=(m // rows,),
        in_specs=[pl.BlockSpec((1, LANES), lambda i: (0, 0))],
        out_specs=(spec, spec),
        compiler_params=_params(1),
        name="rope_tables",
    )(invf)


def _to_head_slabs(res_ref, x):
    for hh in range(res_ref.shape[0]):
        res_ref[hh] = x[:, hh * ATT_HEAD_DIM:(hh + 1) * ATT_HEAD_DIM]


def _norm_rope_to_slabs(nat_ref, col0, res_ref, gain_ref, cos_ref, sin_ref):
    n_heads, rows, _ = res_ref.shape
    step = min(rows, EPI_ROWS)
    for r in range(rows // step):
        rs = slice(r * step, (r + 1) * step)
        for hh in range(n_heads):
            hs = slice(col0 + hh * ATT_HEAD_DIM, col0 + (hh + 1) * ATT_HEAD_DIM)
            res_ref[hh, rs, :] = _head_norm(nat_ref[rs, hs], gain_ref[...])
    for r in range(rows // step):
        rs = slice(r * step, (r + 1) * step)
        cosf = cos_ref[rs, :]
        sinf = sin_ref[rs, :]
        for hh in range(n_heads):
            res_ref[hh, rs, :] = _head_rope(res_ref[hh, rs, :], cosf, sinf)


def _emit_slabs(res_ref, out_ref, dil):
    n_heads, rows, _ = res_ref.shape
    n = n_heads * ATT_HEAD_DIM
    for r in range(dil):
        for hh in range(n_heads):
            col = r * n + hh * ATT_HEAD_DIM
            src = res_ref[hh] if dil == 1 else res_ref[hh, pl.ds(r, rows // dil, stride=dil), :]
            out_ref[:, col:col + ATT_HEAD_DIM] = src.astype(out_ref.dtype)


def _emit_heads(res_ref, out_ref):
    n_heads, rows, _ = res_ref.shape
    for hh in range(n_heads):
        out_ref[pl.ds(hh, rows, stride=n_heads), :] = res_ref[hh]


def _class_spec(tm, n, seq, dil):
    tiles = seq // tm
    assert seq % tm == 0 and tm % (dil * BF16_SUBLANES) == 0
    return pl.BlockSpec((None, tm // dil, dil * n), lambda i, *_: (i // tiles, i % tiles, 0))


def _kv_proj_body(*refs, tm, dilations):
    n_cls = len(dilations)
    x_hbm, gain_ref, w_ref, kgain_ref, cos_ref, sin_ref, k_ref, v_ref = refs[:8]
    kc_refs = refs[8:8 + n_cls]
    vc_refs = refs[8 + n_cls:8 + 2 * n_cls]
    xbuf, xn_ref, res_ref, sem = refs[8 + 2 * n_cls:]
    i = pl.program_id(0)
    j = pl.program_id(1)

    n = w_ref.shape[1] // 2

    @pl.when(j == 0)
    def _k_half():
        _load_norm_rows(x_hbm, xbuf, gain_ref, xn_ref, sem, pl.multiple_of(i * tm, tm), tm)
        xbuf[:, 0:n] = jnp.dot(xn_ref[...], w_ref[:, 0:n], preferred_element_type=F32)

    @pl.when(j == 1)
    def _v_half():
        v = jnp.dot(xn_ref[...], w_ref[:, n:2 * n], preferred_element_type=F32)
        _norm_rope_to_slabs(xbuf, 0, res_ref, kgain_ref, cos_ref, sin_ref)
        _emit_heads(res_ref, k_ref)
        for dil, ref in zip(dilations, kc_refs):
            _emit_slabs(res_ref, ref, dil)
        _to_head_slabs(res_ref, v)
        _emit_heads(res_ref, v_ref)
        for dil, ref in zip(dilations, vc_refs):
            _emit_slabs(res_ref, ref, dil)


def _kv_proj(x, gain, w_kv, g_knorm, cosf, sinf, seq, dilations=()):
    m, d = x.shape
    n = w_kv.shape[2] // 2
    n_heads = n // ATT_HEAD_DIM
    tm = min(m, PROJ_ROW_TILE)
    out_spec = pl.BlockSpec((tm * n_heads, ATT_HEAD_DIM), lambda i, j: (i, 0))
    tab_spec = pl.BlockSpec((tm, LANES), lambda i, j: (i, 0))
    cls_specs = [_class_spec(tm, n, seq, dil) for dil in dilations]
    cls_shapes = [jax.ShapeDtypeStruct((m // seq, seq // dil, dil * n), BF16) for dil in dilations]
    outs = pl.pallas_call(
        functools.partial(_kv_proj_body, tm=tm, dilations=tuple(dilations)),
        out_shape=[jax.ShapeDtypeStruct((m * n_heads, ATT_HEAD_DIM), F32)] * 2 + cls_shapes * 2,
        grid=(m // tm, 2),
        in_specs=[pl.BlockSpec(memory_space=pl.ANY),
                  pl.BlockSpec((1, d), lambda i, j: (0, 0)),
                  pl.BlockSpec((None, d, 2 * n), lambda i, j: (0, 0, 0), pipeline_mode=pl.Buffered(1)),
                  pl.BlockSpec((1, ATT_HEAD_DIM), lambda i, j: (0, 0)),
                  tab_spec, tab_spec],
        out_specs=[out_spec, out_spec] + cls_specs * 2,
        scratch_shapes=[pltpu.VMEM((tm, d), F32), pltpu.VMEM((tm, d), BF16),
                        pltpu.VMEM((n_heads, tm, ATT_HEAD_DIM), F32),
                        pltpu.SemaphoreType.DMA((LOAD_CHUNKS,))],
        compiler_params=_params(2),
        name="kv_proj",
    )(x, gain.reshape(1, d), w_kv, g_knorm.reshape(1, ATT_HEAD_DIM), cosf, sinf)
    n_cls = len(dilations)
    return outs[0], outs[1], tuple(outs[2:2 + n_cls]), tuple(outs[2 + n_cls:])


def _q_proj_body(*refs, tm, n_groups, dilations):
    x_hbm, gain_ref, w_ref, qgain_ref, cos_ref, sin_ref = refs[:6]
    n_out = max(len(dilations), 1)
    q_refs = refs[6:6 + n_out]
    xbuf, xn_ref, res_ref, sem = refs[6 + n_out:]
    i = pl.program_id(0)
    j = pl.program_id(1)

    tn = w_ref.shape[1] // n_groups
    for g in range(n_groups + 1):
        @pl.when(j == g)
        def _step(g=g):
            if g == 0:
                _load_norm_rows(x_hbm, xbuf, gain_ref, xn_ref, sem, pl.multiple_of(i * tm, tm), tm)
            if g < n_groups:
                proj = jnp.dot(xn_ref[...], w_ref[:, g * tn:(g + 1) * tn], preferred_element_type=F32)
            if g > 0:
                _norm_rope_to_slabs(xbuf, ((g - 1) % 2) * tn, res_ref, qgain_ref, cos_ref, sin_ref)
                if dilations:
                    _emit_slabs(res_ref, q_refs[g - 1], dilations[g - 1])
                else:
                    _emit_slabs(res_ref, q_refs[0], 1)
            if g < n_groups:
                xbuf[:, (g % 2) * tn:(g % 2 + 1) * tn] = proj


def _q_proj(x, gain, w_q, g_qnorm, cosf, sinf, layer_mix, layer_b, seq, dilations=()):
    m, d = x.shape
    n_groups = g_qnorm.shape[1]
    tn = w_q.shape[2] // n_groups
    tm = min(m, PROJ_ROW_TILE)
    tab_spec = pl.BlockSpec((tm, LANES), lambda i, j: (i, 0))
    if dilations:
        assert len(dilations) == n_groups
        out_shape = [jax.ShapeDtypeStruct((m // seq, seq // dil, dil * tn), BF16) for dil in dilations]
        out_specs = [_class_spec(tm, tn, seq, dil) for dil in dilations]
    else:
        out_shape = [jax.ShapeDtypeStruct((m, n_groups * tn), BF16)]
        out_specs = [pl.BlockSpec((tm, tn), lambda i, j: (i, jnp.maximum(j - 1, 0)))]
    assert d >= 2 * tn
    return pl.pallas_call(
        functools.partial(_q_proj_body, tm=tm, n_groups=n_groups, dilations=tuple(dilations)),
        out_shape=out_shape,
        grid=(m // tm, n_groups + 1),
        in_specs=[pl.BlockSpec(memory_space=pl.ANY),
                  pl.BlockSpec((None, 1, d), lambda i, j: (layer_mix, 0, 0)),
                  pl.BlockSpec((None, d, n_groups * tn), lambda i, j: (layer_b, 0, 0),
                               pipeline_mode=pl.Buffered(1)),
                  pl.BlockSpec((None, None, 1, ATT_HEAD_DIM), lambda i, j: (layer_b, jnp.maximum(j - 1, 0), 0, 0)),
                  tab_spec, tab_spec],
        out_specs=out_specs,
        scratch_shapes=[pltpu.VMEM((tm, d), F32), pltpu.VMEM((tm, d), BF16),
                        pltpu.VMEM((tn // ATT_HEAD_DIM, tm, ATT_HEAD_DIM), F32),
                        pltpu.SemaphoreType.DMA((LOAD_CHUNKS,))],
        compiler_params=_params(2),
        name="q_proj",
    )(x, gain.reshape(gain.shape[0], 1, d), w_q,
      g_qnorm.reshape(g_qnorm.shape[0], n_groups, 1, ATT_HEAD_DIM), cosf, sinf)


def _softmax_pieces(scores, masks):
    masked = [jnp.where(mk, sc, -jnp.inf) for sc, mk in zip(scores, masks)]
    m = functools.reduce(jnp.maximum, [jnp.max(sc, axis=1, keepdims=True) for sc in masked])
    probs = [jnp.exp(sc - m) for sc in masked]
    l = functools.reduce(jnp.add, [jnp.sum(p, axis=1, keepdims=True) for p in probs])
    return probs, l, m


def _attn_prompt_body(*refs, tq, win, dil, has_prev, n_heads):
    if has_prev:
        q_ref, kc_ref, vc_ref, kp_ref, vp_ref, o_ref, lse_ref = refs
    else:
        q_ref, kc_ref, vc_ref, o_ref, lse_ref = refs
    qi = pl.program_id(1)
    res = pl.program_id(2)
    scale = ATT_HEAD_DIM ** -0.5
    row = lax.broadcasted_iota(jnp.int32, (tq, tq), 0)
    col = lax.broadcasted_iota(jnp.int32, (tq, tq), 1)
    mask_c = (col <= row) & (row - col <= win)
    if has_prev:
        rowp = lax.broadcasted_iota(jnp.int32, (tq, win), 0)
        colp = lax.broadcasted_iota(jnp.int32, (tq, win), 1)
        mask_p = (colp >= rowp) & (qi > 0)
    rows = slice(None) if dil == 1 else pl.ds(res, tq, stride=dil)
    heads = [slice(hh * ATT_HEAD_DIM, (hh + 1) * ATT_HEAD_DIM) for hh in range(n_heads)]
    masks = [mask_c, mask_p] if has_prev else [mask_c]
    scores = []
    for hs in heads:
        qh = q_ref[:, hs]
        sc = [_dot_nt(qh, _as_bf16(kc_ref[:, hs])) * scale]
        if has_prev:
            sc.append(_dot_nt(qh, _as_bf16(kp_ref[:, hs])) * scale)
        scores.append(sc)
    soft = [_softmax_pieces(sc, masks) for sc in scores]
    for hh, (hs, (probs, l, m)) in enumerate(zip(heads, soft)):
        values = [_as_bf16(vc_ref[:, hs])] + ([_as_bf16(vp_ref[:, hs])] if has_prev else [])
        acc = functools.reduce(jnp.add, [jnp.dot(p.astype(BF16), v, preferred_element_type=F32)
                                         for p, v in zip(probs, values)])
        o_ref[hh, rows, :] = acc / l
        lse_ref[hh, rows, :] = jnp.broadcast_to(m + jnp.log(l), (tq, ATT_HEAD_DIM))


def _attn_prompt_group(q, k, v, seq, window, dil):
    b = q.shape[0]
    n = q.shape[2] // dil
    n_heads = n // ATT_HEAD_DIM
    tc = seq // dil
    win = window // dil
    tq = min(tc, ATTN_Q_TILE)
    has_prev = tc > tq
    assert tq % win == 0 and tc % tq == 0
    cur = pl.BlockSpec((None, tq, n), lambda bi, qi, r: (bi, qi, r))
    prev = pl.BlockSpec((None, win, n), lambda bi, qi, r: (bi, jnp.maximum(qi * (tq // win) - 1, 0), r))
    out = pl.BlockSpec((None, n_heads, tq * dil, ATT_HEAD_DIM), lambda bi, qi, r: (bi, 0, qi, 0))
    return pl.pallas_call(
        functools.partial(_attn_prompt_body, tq=tq, win=win, dil=dil, has_prev=has_prev, n_heads=n_heads),
        out_shape=(jax.ShapeDtypeStruct((b, n_heads, seq, ATT_HEAD_DIM), F32),) * 2,
        grid=(b, tc // tq, dil),
        in_specs=[cur, cur, cur] + ([prev, prev] if has_prev else []),
        out_specs=(out, out),
        compiler_params=_params(3),
        name=f"attn_prompt_d{dil}",
    )(q, k, v, *([k, v] if has_prev else []))


def _attn_out_body(*refs, n_groups):
    o_refs = refs[:n_groups]
    l_refs = refs[n_groups:2 * n_groups]
    w_ref, r_ref, out_ref, att_ref = refs[2 * n_groups:]

    @pl.when(pl.program_id(1) == 0)
    def _mix():
        for hh in range(o_refs[0].shape[0]):
            lses = [l[hh] for l in l_refs]
            m = functools.reduce(jnp.maximum, lses)
            es = [jnp.exp(l - m) for l in lses]
            tot = functools.reduce(jnp.add, es)
            att = functools.reduce(jnp.add, [(e / tot) * o[hh] for e, o in zip(es, o_refs)])
            att_ref[:, hh * ATT_HEAD_DIM:(hh + 1) * ATT_HEAD_DIM] = att.astype(BF16)

    out_ref[...] = r_ref[...] + jnp.dot(att_ref[...], w_ref[...], preferred_element_type=F32)


def _attn_out(outs, lses, w_o, resid, layer, *, tm=256):
    b, n_heads, seq, hd = outs[0].shape
    m, n = resid.shape
    k = n_heads * hd
    tn = n
    tm = min(seq, tm)
    tiles = seq // tm
    blk = pl.BlockSpec((None, n_heads, tm, hd), lambda i, j: (i // tiles, 0, i % tiles, 0))
    return pl.pallas_call(
        functools.partial(_attn_out_body, n_groups=len(outs)),
        out_shape=jax.ShapeDtypeStruct((m, n), F32),
        grid=(m // tm, n // tn),
        in_specs=[blk] * (2 * len(outs)) + [pl.BlockSpec((None, k, tn), lambda i, j: (layer, 0, j)),
                                            pl.BlockSpec((tm, tn), lambda i, j: (i, j))],
        out_specs=pl.BlockSpec((tm, tn), lambda i, j: (i, j)),
        scratch_shapes=[pltpu.VMEM((tm, k), BF16)],
        compiler_params=_params(2),
        name="attn_out",
    )(*outs, *lses, w_o, resid)


def _attn_sample_body(q_ref, kc_ref, vc_ref, kn_ref, vn_ref, o_ref, *, groups, n_heads, n_new):
    rows = q_ref.shape[0]
    cache = kc_ref.shape[0] // n_heads
    n = n_heads * ATT_HEAD_DIM
    scale = ATT_HEAD_DIM ** -0.5
    dist_c = cache + lax.broadcasted_iota(jnp.int32, (rows, cache), 0) \
        - lax.broadcasted_iota(jnp.int32, (rows, cache), 1)
    dist_n = lax.broadcasted_iota(jnp.int32, (rows, rows), 0) - lax.broadcasted_iota(jnp.int32, (rows, rows), 1)
    col_n = lax.broadcasted_iota(jnp.int32, (rows, rows), 1)
    masks = [[(dist_c <= window) & ((dist_c & (dil - 1)) == 0),
              (dist_n >= 0) & (col_n < n_new) & ((dist_n & (dil - 1)) == 0)] for window, dil in groups]
    heads = [slice(hh * ATT_HEAD_DIM, (hh + 1) * ATT_HEAD_DIM) for hh in range(n_heads)]
    scores = []
    for hh, hs in enumerate(heads):
        kc = kc_ref[pl.ds(hh, cache, stride=n_heads), :].astype(BF16)
        kn = kn_ref[:, hs].astype(BF16)
        for g in range(len(groups)):
            qh = q_ref[:, g * n + hh * ATT_HEAD_DIM:g * n + (hh + 1) * ATT_HEAD_DIM]
            scores.append([_dot_nt(qh, kc) * scale, _dot_nt(qh, kn) * scale])
    soft = [_softmax_pieces(sc, masks[u % len(groups)]) for u, sc in enumerate(scores)]
    for hh, hs in enumerate(heads):
        vc = vc_ref[pl.ds(hh, cache, stride=n_heads), :].astype(BF16)
        vn = vn_ref[:, hs].astype(BF16)
        outs, lses = [], []
        for probs, l, m in soft[hh * len(groups):(hh + 1) * len(groups)]:
            acc = jnp.dot(probs[0].astype(BF16), vc, preferred_element_type=F32) \
                + jnp.dot(probs[1].astype(BF16), vn, preferred_element_type=F32)
            outs.append(acc / l)
            lses.append(m + jnp.log(l))
        m_all = functools.reduce(jnp.maximum, lses)
        es = [jnp.exp(l - m_all) for l in lses]
        tot = functools.reduce(jnp.add, es)
        o_ref[:, hs] = functools.reduce(jnp.add, [(e / tot) * o for e, o in zip(es, outs)]).astype(o_ref.dtype)


def _attn_sample(q, k_new, v_new, cache_k, cache_v, n_new):
    b, rows, n = k_new.shape
    cache, n_heads = cache_k.shape[1], cache_k.shape[2]
    assert cache == MAX_WINDOW and all(d & (d - 1) == 0 for _, d in DILATED_GROUPS)
    row_blk = lambda width: pl.BlockSpec((None, rows, width), lambda bi: (bi, 0, 0))
    cache_blk = pl.BlockSpec((None, cache * n_heads, ATT_HEAD_DIM), lambda bi: (bi, 0, 0))
    cache_k = cache_k.reshape(b, cache * n_heads, ATT_HEAD_DIM)
    cache_v = cache_v.reshape(b, cache * n_heads, ATT_HEAD_DIM)
    return pl.pallas_call(
        functools.partial(_attn_sample_body, groups=DILATED_GROUPS, n_heads=n // ATT_HEAD_DIM, n_new=n_new),
        out_shape=jax.ShapeDtypeStruct((b, rows, n), BF16),
        grid=(b,),
        in_specs=[row_blk(q.shape[2]), cache_blk, cache_blk, row_blk(n), row_blk(n)],
        out_specs=row_blk(n),
        compiler_params=_params(1),
        name="attn_sample",
    )(q, cache_k, cache_v, k_new, v_new)


FFN_ORDER = (("ffn1", 0), ("ffn2", 0), ("ffn1", 1), ("ffn2", 1))
SMALL_WEIGHTS = (("w_in_a_t", True), ("w_out_a", False), ("w_kv", False), ("w_q_b", False), ("w_o_b", False))


def _ffn_stacks(p, step):
    name, _ = FFN_ORDER[step]
    return tuple(p[f"w_{name}_{part}"] for part in ("gate", "up", "down"))


def _run_ffn(x, p, bf16, step):
    name, layer = FFN_ORDER[step]
    if ("ffn", step) not in bf16:
        x, bf16[("ffn", step)], _ = _ffn(x, p[f"norm_{name}"], layer, _ffn_stacks(p, step), layer, emit=True)
        return x
    casts, keys = [], []
    if step + 1 < len(FFN_ORDER) and ("ffn", step + 1) not in bf16:
        gate, up, down = _ffn_stacks(p, step + 1)
        nxt_layer = FFN_ORDER[step + 1][1]
        casts += [(gate, nxt_layer, False), (up, nxt_layer, False), (down, nxt_layer, True)]
        keys += [("ffn", step + 1)] * 3
    for key, rows_follow_j in SMALL_WEIGHTS:
        if key not in bf16:
            assert p[key].shape[0] == 1
            casts.append((p[key], 0, rows_follow_j))
            keys.append(key)
    x, _, cast = _ffn(x, p[f"norm_{name}"], layer, bf16[("ffn", step)], casts=tuple(casts))
    for key, arr in zip(keys, cast):
        if key[0] == "ffn":
            bf16[key] = bf16.get(key, ()) + (arr,)
        else:
            bf16[key] = arr
    return x


def _trunk(x, seq, base, cache_k, cache_v, c0, n0, m0, p, bf16, first_ffn_done=False):
    m_rows, d = x.shape
    b = m_rows // seq
    nh = MLSTM_HEADS
    dk, dv = c0.shape[2], c0.shape[3]
    fresh = cache_k is None
    dilations = tuple(dil for _, dil in DILATED_GROUPS)

    if not first_ffn_done:
        x = _run_ffn(x, p, bf16, 0)
    proj, gates = _in_proj(x, p["norm_mix"], bf16["w_in_a_t"], 2 * nh * dk + nh * dv + d, 0)
    if seq % MLSTM_PROMPT_CHUNK == 0:
        chunk, valid, t_pad = MLSTM_PROMPT_CHUNK, MLSTM_PROMPT_CHUNK, seq
    else:
        assert seq <= BF16_SUBLANES
        chunk, valid, t_pad = BF16_SUBLANES, seq, BF16_SUBLANES
    proj3 = proj.reshape(b, seq, -1)
    gates3 = gates.reshape(b, seq, LANES)[:, :, :2 * nh]
    if t_pad != seq:
        proj3 = jnp.pad(proj3, ((0, 0), (0, t_pad - seq), (0, 0)))
        gates3 = jnp.pad(gates3, ((0, 0), (0, t_pad - seq), (0, 0)))
    gates_row = gates3.reshape(b, t_pad, 2, nh).transpose(0, 3, 2, 1)
    y, c_fin, n_fin, m_fin = _mlstm(proj3, gates_row, p["b_ig"][0], p["b_fg"][0], p["g_mh"][0],
                                    c0, n0, m0, chunk=chunk, valid=valid)
    y = y[:, :seq].reshape(m_rows, nh * dv)
    x = _matmul_resid(y, bf16["w_out_a"], x, 0)
    x = _run_ffn(x, p, bf16, 1)

    cosf, sinf = _rope_tables(m_rows, seq, base)
    cls_dils = dilations if fresh else ()
    k, v, k_cls, v_cls = _kv_proj(x, p["norm_kv"], bf16["w_kv"], p["g_knorm"], cosf, sinf, seq, cls_dils)

    x = _run_ffn(x, p, bf16, 2)
    q = _q_proj(x, p["norm_mix"], bf16["w_q_b"], p["g_qnorm"], cosf, sinf, 1, 0, seq, dilations if fresh else ())
    n_kv = k.shape[0] // m_rows * k.shape[1]
    if fresh:
        outs, lses = [], []
        for q_g, k_g, v_g, (window, dil) in zip(q, k_cls, v_cls, DILATED_GROUPS):
            o_g, l_g = _attn_prompt_group(q_g, k_g, v_g, seq, window, dil)
            outs.append(o_g)
            lses.append(l_g)
        x = _attn_out(outs, lses, bf16["w_o_b"], x, 0)
    else:
        pad = ((0, 0), (0, BF16_SUBLANES - seq), (0, 0))
        att = _attn_sample(jnp.pad(q[0].reshape(b, seq, -1), pad), jnp.pad(k.reshape(b, seq, n_kv), pad),
                           jnp.pad(v.reshape(b, seq, n_kv), pad), cache_k, cache_v, seq)
        x = _matmul_resid(att[:, :seq].reshape(m_rows, n_kv), bf16["w_o_b"], x, 0)
    x = _run_ffn(x, p, bf16, 3)
    return x, k, v, c_fin, n_fin, m_fin


def kernel(x_prompt, x_sample, cache_k, cache_v, state_C, state_n, state_m, norm_ffn1, w_ffn1_gate, w_ffn1_up, w_ffn1_down, norm_mix, norm_ffn2, w_ffn2_gate, w_ffn2_up, w_ffn2_down, w_in_a, b_ig, b_fg, g_mh, w_out_a, norm_kv, w_kv, g_knorm, w_q_b, g_qnorm, w_o_b):
    bp, seq_p, d = x_prompt.shape
    bs, seq_s, _ = x_sample.shape
    past = cache_k.shape[1]
    assert past == min(MAX_WINDOW, PAST_LEN)
    nh = MLSTM_HEADS
    dk, dv = state_C.shape[3], state_C.shape[4]

    p = dict(norm_ffn1=norm_ffn1, norm_mix=norm_mix, norm_ffn2=norm_ffn2, norm_kv=norm_kv,
             w_ffn1_gate=w_ffn1_gate, w_ffn1_up=w_ffn1_up, w_ffn1_down=w_ffn1_down,
             w_ffn2_gate=w_ffn2_gate, w_ffn2_up=w_ffn2_up, w_ffn2_down=w_ffn2_down,
             w_in_a_t=jnp.swapaxes(w_in_a, 1, 2), b_ig=b_ig, b_fg=b_fg, g_mh=g_mh, w_out_a=w_out_a, w_kv=w_kv[None],
             g_knorm=g_knorm, w_q_b=w_q_b, g_qnorm=g_qnorm, w_o_b=w_o_b)
    bf16 = {}
    xs = _run_ffn(x_sample.reshape(bs * seq_s, d), p, bf16, 0)

    zeros = lambda *s: jnp.zeros(s, F32)
    y_p, k_p, v_p, c_p, n_p, m_p = _trunk(
        x_prompt.reshape(bp * seq_p, d), seq_p, 0, None, None,
        zeros(bp, nh, dk, dv), zeros(bp, nh, 1, dk), zeros(bp, nh, 1, LANES), p, bf16)
    y_s, k_s, v_s, c_s, n_s, m_s = _trunk(
        xs, seq_s, PAST_LEN, cache_k, cache_v,
        state_C[:, 0], state_n[:, 0][:, :, None, :],
        jnp.broadcast_to(state_m[:, 0][:, :, None, None], (bs, nh, 1, LANES)), p, bf16, first_ffn_done=True)

    hkv, hd = cache_k.shape[2], cache_k.shape[3]
    keep = min(MAX_WINDOW, seq_p)
    return (y_p.reshape(bp, seq_p, d), y_s.reshape(bs, seq_s, d),
            k_p.reshape(bp, seq_p, hkv, hd)[:, seq_p - keep:], v_p.reshape(bp, seq_p, hkv, hd)[:, seq_p - keep:],
            k_s.reshape(bs, seq_s, hkv, hd), v_s.reshape(bs, seq_s, hkv, hd),
            c_p[:, None], n_p[:, None, :, 0], m_p[:, None, :, 0, 0],
            c_s[:, None], n_s[:, None, :, 0], m_s[:, None, :, 0, 0])
```
